```python
import math
import jax, jax.numpy as jnp
from jax import lax
import numpy as np

D_MODEL = 1024
BATCH = 8
SEQ = 4096
DEPTH = 2

EPS = 1e-6
MIX_WIDTH = D_MODEL
A_WIDTH = MIX_WIDTH // 2
B_WIDTH = MIX_WIDTH - A_WIDTH
A_GROUPS = 4
A_GROUP_DIM = A_WIDTH // A_GROUPS
CHUNK = 128
B_GROUPS = 8
CONV_W = 3
EVEN_IN_WIDTH = 2 * A_WIDTH + 3 * B_WIDTH
N_HEADS = 16
N_KV_HEADS = 4
Q_PER_KV = N_HEADS // N_KV_HEADS
HEAD_DIM = D_MODEL // N_HEADS
QKV_WIDTH = (N_HEADS + 2 * N_KV_HEADS) * HEAD_DIM
WINDOW = 128
BLOCK = 128
N_BUCKETS = 32
MAX_DISTANCE = 128
D_FF = ((8 * D_MODEL // 3 + 255) // 256) * 256
N_EVEN = (DEPTH + 1) // 2
N_ODD = DEPTH // 2

kernel_name = "hybrid_gmlp_shortconv_swa_encoder"


def rmsnorm(x, g):
    xf = x.astype(jnp.float32)
    y = xf * lax.rsqrt(jnp.mean(xf * xf, axis=-1, keepdims=True) + EPS)
    return (y * g.astype(jnp.float32)).astype(x.dtype)


def layernorm(x, g, b):
    xf = x.astype(jnp.float32)
    mu = jnp.mean(xf, axis=-1, keepdims=True)
    xc = xf - mu
    y = xc * lax.rsqrt(jnp.mean(xc * xc, axis=-1, keepdims=True) + EPS)
    return (y * g.astype(jnp.float32) + b.astype(jnp.float32)).astype(x.dtype)


def t5_buckets(rel):
    nb = N_BUCKETS // 2
    ret = jnp.where(rel > 0, nb, 0)
    n = jnp.abs(rel)
    max_exact = nb // 2
    nf = jnp.maximum(n, 1).astype(jnp.float32)
    large = max_exact + (jnp.log(nf / max_exact) / math.log(MAX_DISTANCE / max_exact)
                         * (nb - max_exact)).astype(jnp.int32)
    large = jnp.minimum(large, nb - 1)
    return ret + jnp.where(n < max_exact, n, large)


def even_mixer(h, w_in, v_ln_g, v_ln_b, w_spatial, b_spatial, conv_w, w_out):
    bsz, s, _ = h.shape
    proj = jnp.einsum('bsd,de->bse', h, w_in)
    a_u, a_v, b_b, b_c, b_h = jnp.split(
        proj, [A_WIDTH, 2 * A_WIDTH, 2 * A_WIDTH + B_WIDTH, 2 * A_WIDTH + 2 * B_WIDTH], axis=-1)
    a_u = jax.nn.gelu(a_u, approximate=False)
    a_v = layernorm(jax.nn.gelu(a_v, approximate=False), v_ln_g, v_ln_b)
    n_chunks = s // CHUNK
    v = a_v.reshape(bsz, n_chunks, CHUNK, A_GROUPS, A_GROUP_DIM)
    mixed = jnp.einsum('gpq,bcqgd->bcpgd', w_spatial, v) + b_spatial.T[None, None, :, :, None]
    a_out = a_u * mixed.reshape(bsz, s, A_WIDTH)
    z = b_c * b_h
    pad = CONV_W // 2
    zp = jnp.pad(z, ((0, 0), (pad, pad), (0, 0)))
    conv = zp[:, 0:s] * conv_w[0]
    for tap in range(1, CONV_W):
        conv = conv + zp[:, tap:tap + s] * conv_w[tap]
    b_out = b_b * conv
    y = jnp.concatenate([a_out, b_out], axis=-1)
    return jnp.einsum('bse,ed->bsd', y, w_out)


def window_attention(h, w_qkv, sink, rel_bias, w_out):
    bsz, s, _ = h.shape
    n_blocks = s // BLOCK
    qkv = jnp.einsum('bsd,de->bse', h, w_qkv)
    q, k, v = jnp.split(qkv, [N_HEADS * HEAD_DIM, (N_HEADS + N_KV_HEADS) * HEAD_DIM], axis=-1)
    q = q.reshape(bsz, n_blocks, BLOCK, N_KV_HEADS, Q_PER_KV, HEAD_DIM).transpose(1, 0, 2, 3, 4, 5)
    k = k.reshape(bsz, s, N_KV_HEADS, HEAD_DIM)
    v = v.reshape(bsz, s, N_KV_HEADS, HEAD_DIM)
    kp = jnp.pad(k, ((0, 0), (BLOCK, BLOCK), (0, 0), (0, 0)))
    vp = jnp.pad(v, ((0, 0), (BLOCK, BLOCK), (0, 0), (0, 0)))
    qi = jnp.arange(BLOCK, dtype=jnp.int32)[:, None]
    kj = jnp.arange(3 * BLOCK, dtype=jnp.int32)[None, :]
    rel = kj - BLOCK - qi
    band = jnp.abs(rel) <= WINDOW
    bias = rel_bias[t5_buckets(rel)].astype(jnp.float32)
    bias = bias.transpose(2, 0, 1).reshape(N_KV_HEADS, Q_PER_KV, BLOCK, 3 * BLOCK)
    sink_l = sink.astype(jnp.float32).reshape(N_KV_HEADS, Q_PER_KV, 1, 1)
    scale = HEAD_DIM ** -0.5

    def attend_block(args):
        n, qb = args
        start = n * BLOCK
        kb = lax.dynamic_slice_in_dim(kp, start, 3 * BLOCK, axis=1)
        vb = lax.dynamic_slice_in_dim(vp, start, 3 * BLOCK, axis=1)
        key_pos = start - BLOCK + kj
        mask = band & (key_pos >= 0) & (key_pos < s)
        sc = jnp.einsum('bqhgd,bkhd->bhgqk', qb, kb).astype(jnp.float32) * scale + bias
        sc = jnp.where(mask, sc, -jnp.inf)
        m = jnp.maximum(jnp.max(sc, axis=-1, keepdims=True), sink_l)
        p = jnp.exp(sc - m)
        p = p / (jnp.sum(p, axis=-1, keepdims=True) + jnp.exp(sink_l - m))
        return jnp.einsum('bhgqk,bkhd->bqhgd', p.astype(vb.dtype), vb)

    out = lax.map(attend_block, (jnp.arange(n_blocks, dtype=jnp.int32), q))
    out = out.transpose(1, 0, 2, 3, 4, 5).reshape(bsz, s, N_HEADS * HEAD_DIM)
    return jnp.einsum('bse,ed->bsd', out, w_out)


def swiglu(h, w_gate, w_up, w_down):
    g = jnp.einsum('bsd,df->bsf', h, w_gate)
    u = jnp.einsum('bsd,df->bsf', h, w_up)
    return jnp.einsum('bsf,fd->bsd', jax.nn.silu(g) * u, w_down)


def setup_inputs(seed: int = 0) -> dict:
    key = jax.random.key(seed)
    ks = jax.random.split(key, 20)
    f32 = jnp.float32
    nrm = lambda k, shape, scale: jax.random.normal(k, shape, f32) * scale
    return {
        "x": nrm(ks[0], (BATCH, SEQ, D_MODEL), 1.0),
        "norm_mix": 1.0 + nrm(ks[1], (DEPTH, D_MODEL), 0.02),
        "norm_ffn": 1.0 + nrm(ks[2], (DEPTH, D_MODEL), 0.02),
        "even_w_in": nrm(ks[3], (N_EVEN, D_MODEL, EVEN_IN_WIDTH), D_MODEL ** -0.5),
        "even_v_ln_g": 1.0 + nrm(ks[4], (N_EVEN, A_WIDTH), 0.02),
        "even_v_ln_b": nrm(ks[5], (N_EVEN, A_WIDTH), 0.02),
        "even_w_spatial": nrm(ks[6], (N_EVEN, A_GROUPS, CHUNK, CHUNK), CHUNK ** -0.5),
        "even_b_spatial": 1.0 + nrm(ks[7], (N_EVEN, A_GROUPS, CHUNK), 0.1),
        "even_conv_w": nrm(ks[8], (N_EVEN, CONV_W, B_WIDTH), CONV_W ** -0.5),
        "even_w_out": nrm(ks[9], (N_EVEN, MIX_WIDTH, D_MODEL), MIX_WIDTH ** -0.5),
        "attn_w_qkv": nrm(ks[10], (N_ODD, D_MODEL, QKV_WIDTH), D_MODEL ** -0.5),
        "attn_sink": nrm(ks[11], (N_ODD, N_HEADS), 0.5),
        "rel_bias": nrm(ks[12], (N_BUCKETS, N_HEADS), 0.5),
        "attn_w_out": nrm(ks[13], (N_ODD, N_HEADS * HEAD_DIM, D_MODEL), (N_HEADS * HEAD_DIM) ** -0.5),
        "ffn_w_gate": nrm(ks[14], (DEPTH, D_MODEL, D_FF), D_MODEL ** -0.5),
        "ffn_w_up": nrm(ks[15], (DEPTH, D_MODEL, D_FF), D_MODEL ** -0.5),
        "ffn_w_down": nrm(ks[16], (DEPTH, D_FF, D_MODEL), D_FF ** -0.5),
        "final_norm": 1.0 + nrm(ks[17], (D_MODEL,), 0.02),
    }


def reference(x, norm_mix, norm_ffn, even_w_in, even_v_ln_g, even_v_ln_b, even_w_spatial,
              even_b_spatial, even_conv_w, even_w_out, attn_w_qkv, attn_sink, rel_bias,
              attn_w_out, ffn_w_gate, ffn_w_up, ffn_w_down, final_norm):
    for layer in range(DEPTH):
        i = layer // 2
        h = rmsnorm(x, norm_mix[layer])
        if layer % 2 == 0:
            x = x + even_mixer(h, even_w_in[i], even_v_ln_g[i], even_v_ln_b[i], even_w_spatial[i],
                               even_b_spatial[i], even_conv_w[i], even_w_out[i])
        else:
            x = x + window_attention(h, attn_w_qkv[i], attn_sink[i], rel_bias, attn_w_out[i])
        h = rmsnorm(x, norm_ffn[layer])
        x = x + swiglu(h, ffn_w_gate[layer], ffn_w_up[layer], ffn_w_down[layer])
    return rmsnorm(x, final_norm)
```

```python
import functools
import math

import jax
import jax.numpy as jnp
from jax import lax
from jax.experimental import pallas as pl
from jax.experimental.pallas import tpu as pltpu

D_MODEL = 1024
EPS = 1e-6
A_WIDTH = 512
B_WIDTH = 512
A_GROUPS = 4
A_GROUP_DIM = 128
CHUNK = 128
CONV_W = 3
N_HEADS = 16
N_KV_HEADS = 4
Q_PER_KV = 4
HEAD_DIM = 64
WINDOW = 128
BLOCK = 128
N_BUCKETS = 32
MAX_DISTANCE = 128
D_FF = 2816

TOKEN_TILE = 512
HALO = 16
FF_CHUNK = 256
N_FF_CHUNKS = D_FF // FF_CHUNK
SUB_BLOCKS = TOKEN_TILE // BLOCK
KEY_SPAN = 3 * BLOCK
MASK_VALUE = -1e30
VMEM_LIMIT_BYTES = 56 * 1024 * 1024

_BF16 = jnp.bfloat16
_F32 = jnp.float32


def _dot(a, b):
    return jnp.dot(a, b, preferred_element_type=_F32)


def _dot_nt(a, b):
    return lax.dot_general(a, b, (((1,), (1,)), ((), ())), preferred_element_type=_F32)


def _dot_tn(a, b):
    return lax.dot_general(a, b, (((0,), (0,)), ((), ())), preferred_element_type=_F32)


def _gelu(x):
    return 0.5 * x * (1.0 + lax.erf(x * math.sqrt(0.5)))


def _rms(x, g):
    return (x * lax.rsqrt(jnp.mean(x * x, axis=-1, keepdims=True) + EPS)) * g


def _const_spec(shape):
    zeros = (0,) * len(shape)
    return pl.BlockSpec(shape, lambda i: zeros, pipeline_mode=pl.Buffered(1))


def _params():
    return pltpu.CompilerParams(
        dimension_semantics=("arbitrary",), vmem_limit_bytes=VMEM_LIMIT_BYTES)


def _even_kernel(tiles_per_seq, x_ref, xp_ref, xn_ref, g_ref, wa_ref, wc_ref, lng_ref, lnb_ref,
                 wsp_ref, bsp_ref, cw_ref, wo_ref, o_ref, hs_ref, y_ref):
    t = TOKEN_TILE
    i = pl.program_id(0)
    pos = i % tiles_per_seq
    g = g_ref[...]
    x = x_ref[...]
    h = _rms(x, g).astype(_BF16)
    hp = jnp.where(pos == 0, 0.0, _rms(xp_ref[...], g)).astype(_BF16)
    hn = jnp.where(pos == tiles_per_seq - 1, 0.0, _rms(xn_ref[...], g)).astype(_BF16)
    hs_ref[0:HALO, :] = hp
    hs_ref[HALO:HALO + t, :] = h
    hs_ref[HALO + t:, :] = hn

    pa = _dot(h, wa_ref[...])
    pc = _dot(hs_ref[...], wc_ref[...])

    u = _gelu(pa[:, :A_WIDTH])
    v = _gelu(pa[:, A_WIDTH:2 * A_WIDTH])
    mu = jnp.mean(v, axis=-1, keepdims=True)
    vc = v - mu
    vn = vc * lax.rsqrt(jnp.mean(vc * vc, axis=-1, keepdims=True) + EPS)
    vb = (vn * lng_ref[...] + lnb_ref[...]).astype(_BF16)
    for c in range(t // CHUNK):
        rows = slice(c * CHUNK, (c + 1) * CHUNK)
        for grp in range(A_GROUPS):
            cols = slice(grp * A_GROUP_DIM, (grp + 1) * A_GROUP_DIM)
            mixed = _dot(wsp_ref[grp], vb[rows, cols]) + bsp_ref[grp]
            y_ref[rows, cols] = (u[rows, cols] * mixed).astype(_BF16)

    z = pc[:, :B_WIDTH] * pc[:, B_WIDTH:]
    n_rows = t + 2 * HALO
    z_prev = pltpu.roll(z, 1, 0)[HALO:HALO + t]
    z_next = pltpu.roll(z, n_rows - 1, 0)[HALO:HALO + t]
    cw = cw_ref[...]
    conv = z_prev * cw[0:1] + z[HALO:HALO + t] * cw[1:2] + z_next * cw[2:3]
    y_ref[:, A_WIDTH:] = (pa[:, 2 * A_WIDTH:] * conv).astype(_BF16)

    o_ref[...] = x + _dot(y_ref[...], wo_ref[...])


def _even_layer(x2, g, w_in, ln_g, ln_b, w_sp, b_sp, conv_w, w_out, seq_len):
    n, d = x2.shape
    t = TOKEN_TILE
    halo_blocks_per_tile = t // HALO
    n_halo_blocks = n // HALO
    in_width_a = 2 * A_WIDTH + B_WIDTH
    wa = w_in[:, :in_width_a].astype(_BF16)
    wc = w_in[:, in_width_a:].astype(_BF16)
    return pl.pallas_call(
        functools.partial(_even_kernel, seq_len // t),
        grid=(n // t,),
        in_specs=[
            pl.BlockSpec((t, d), lambda i: (i, 0)),
            pl.BlockSpec((HALO, d), lambda i: (jnp.maximum(i * halo_blocks_per_tile - 1, 0), 0)),
            pl.BlockSpec((HALO, d),
                         lambda i: (jnp.minimum((i + 1) * halo_blocks_per_tile, n_halo_blocks - 1), 0)),
            _const_spec((1, d)),
            _const_spec(wa.shape),
            _const_spec(wc.shape),
            _const_spec((1, A_WIDTH)),
            _const_spec((1, A_WIDTH)),
            _const_spec((A_GROUPS, CHUNK, CHUNK)),
            _const_spec((A_GROUPS, CHUNK, 1)),
            _const_spec((CONV_W, B_WIDTH)),
            _const_spec((d, d)),
        ],
        out_specs=pl.BlockSpec((t, d), lambda i: (i, 0)),
        out_shape=jax.ShapeDtypeStruct((n, d), _F32),
        scratch_shapes=[
            pltpu.VMEM((t + 2 * HALO, d), _BF16),
            pltpu.VMEM((t, d), _BF16),
        ],
        compiler_params=_params(),
        name="even_mixer",
    )(x2, x2, x2, g.reshape(1, d), wa, wc, ln_g.reshape(1, -1), ln_b.reshape(1, -1),
      w_sp.astype(_BF16), b_sp[..., None], conv_w, w_out.astype(_BF16))


def _ffn_kernel(has_final_norm, x_ref, g_ref, wg_ref, wu_ref, wd_ref, *rest):
    if has_final_norm:
        gf_ref, o_ref, h_ref, acc_ref = rest
    else:
        o_ref, h_ref, acc_ref = rest
    x = x_ref[...]
    h_ref[...] = _rms(x, g_ref[...]).astype(_BF16)
    acc_ref[...] = x

    def chunk(c, carry):
        h = h_ref[...]
        gate = _dot(h, wg_ref[c])
        up = _dot(h, wu_ref[c])
        act = (jax.nn.silu(gate) * up).astype(_BF16)
        acc_ref[...] += _dot(act, wd_ref[c])
        return carry

    lax.fori_loop(0, N_FF_CHUNKS, chunk, 0)
    out = acc_ref[...]
    if has_final_norm:
        out = _rms(out, gf_ref[...])
    o_ref[...] = out


def _chunk_columns(w):
    d = w.shape[0]
    return w.astype(_BF16).reshape(d, N_FF_CHUNKS, FF_CHUNK).transpose(1, 0, 2)


def _ffn_layer(x2, g, w_gate, w_up, w_down, final_g=None):
    n, d = x2.shape
    t = TOKEN_TILE
    wg = _chunk_columns(w_gate)
    wu = _chunk_columns(w_up)
    wd = w_down.astype(_BF16).reshape(N_FF_CHUNKS, FF_CHUNK, d)
    in_specs = [
        pl.BlockSpec((t, d), lambda i: (i, 0)),
        _const_spec((1, d)),
        _const_spec(wg.shape),
        _const_spec(wu.shape),
        _const_spec(wd.shape),
    ]
    args = [x2, g.reshape(1, d), wg, wu, wd]
    if final_g is not None:
        in_specs.append(_const_spec((1, d)))
        args.append(final_g.reshape(1, d))
    return pl.pallas_call(
        functools.partial(_ffn_kernel, final_g is not None),
        grid=(n // t,),
        in_specs=in_specs,
        out_specs=pl.BlockSpec((t, d), lambda i: (i, 0)),
        out_shape=jax.ShapeDtypeStruct((n, d), _F32),
        scratch_shapes=[pltpu.VMEM((t, d), _BF16), pltpu.VMEM((t, d), _F32)],
        compiler_params=_params(),
        name="swiglu_ffn",
    )(*args)


def _qkv_kernel(x_ref, g_ref, wqv_ref, wk_ref, qt_ref, k_ref, vt_ref):
    h = _rms(x_ref[...], g_ref[...]).astype(_BF16)
    qv = _dot_nt(wqv_ref[...], h)
    q_rows = N_HEADS * HEAD_DIM
    scale = HEAD_DIM ** -0.5
    for j in range(SUB_BLOCKS):
        lanes = slice(j * BLOCK, (j + 1) * BLOCK)
        qt_ref[j] = (qv[:q_rows, lanes] * scale).astype(_BF16)
        vt_ref[j] = qv[q_rows:, lanes].astype(_BF16)
    k_ref[...] = _dot(h, wk_ref[...]).astype(_BF16)


def _qkv_layer(x2, g, w_qkv):
    n, d = x2.shape
    t = TOKEN_TILE
    q_cols = N_HEADS * HEAD_DIM
    kv_cols = N_KV_HEADS * HEAD_DIM
    wq = w_qkv[:, :q_cols]
    wk = w_qkv[:, q_cols:q_cols + kv_cols]
    wv = w_qkv[:, q_cols + kv_cols:]
    wqv_t = jnp.concatenate([wq, wv], axis=1).T.astype(_BF16)
    n_blocks = n // BLOCK
    return pl.pallas_call(
        _qkv_kernel,
        grid=(n // t,),
        in_specs=[
            pl.BlockSpec((t, d), lambda i: (i, 0)),
            _const_spec((1, d)),
            _const_spec(wqv_t.shape),
            _const_spec((d, kv_cols)),
        ],
        out_specs=[
            pl.BlockSpec((SUB_BLOCKS, q_cols, BLOCK), lambda i: (i, 0, 0)),
            pl.BlockSpec((t, kv_cols), lambda i: (i, 0)),
            pl.BlockSpec((SUB_BLOCKS, kv_cols, BLOCK), lambda i: (i, 0, 0)),
        ],
        out_shape=[
            jax.ShapeDtypeStruct((n_blocks, q_cols, BLOCK), _BF16),
            jax.ShapeDtypeStruct((n, kv_cols), _BF16),
            jax.ShapeDtypeStruct((n_blocks, kv_cols, BLOCK), _BF16),
        ],
        compiler_params=_params(),
        name="qkv_proj",
    )(x2, g.reshape(1, d), wqv_t, wk.astype(_BF16))


def _attn_kernel(tiles_per_seq, sink_ref, rb_ref, x_ref, qt_ref, kc_ref, kp_ref, kn_ref,
                 vc_ref, vp_ref, vn_ref, bkt_ref, wo_ref, o_ref,
                 bias_ref, kall_ref, vall_ref, sc_ref, p_ref, at_ref):
    i = pl.program_id(0)
    pos = i % tiles_per_seq

    @pl.when(i == 0)
    def _init():
        bkt = bkt_ref[...]
        rel = (lax.broadcasted_iota(jnp.int32, (KEY_SPAN, BLOCK), 0) - BLOCK
               - lax.broadcasted_iota(jnp.int32, (KEY_SPAN, BLOCK), 1))
        band = jnp.abs(rel) <= WINDOW
        for hd in range(N_HEADS):
            tbl = jnp.zeros((KEY_SPAN, BLOCK), _F32)
            for b in range(N_BUCKETS):
                tbl = jnp.where(bkt == b, rb_ref[b * N_HEADS + hd], tbl)
            bias_ref[hd] = jnp.where(band, tbl, MASK_VALUE)

    kall_ref[0:BLOCK, :] = kp_ref[...]
    kall_ref[BLOCK:BLOCK + TOKEN_TILE, :] = kc_ref[...]
    kall_ref[BLOCK + TOKEN_TILE:, :] = kn_ref[...]
    vall_ref[0] = vp_ref[0]
    for j in range(SUB_BLOCKS):
        vall_ref[1 + j] = vc_ref[j]
    vall_ref[1 + SUB_BLOCKS] = vn_ref[0]

    key_row = lax.broadcasted_iota(jnp.int32, (KEY_SPAN, 1), 0)
    pen_first = jnp.where((key_row < BLOCK) & (pos == 0), MASK_VALUE, 0.0)
    pen_last = jnp.where((key_row >= 2 * BLOCK) & (pos == tiles_per_seq - 1), MASK_VALUE, 0.0)

    pair_lanes = 2 * HEAD_DIM
    for j in range(SUB_BLOCKS):
        vwin = jnp.concatenate([vall_ref[j], vall_ref[j + 1], vall_ref[j + 2]], axis=1)
        for kvh in range(N_KV_HEADS):
            pair = kvh // 2
            half = kvh % 2
            q_heads = jnp.concatenate(
                [qt_ref[j, (kvh * Q_PER_KV + grp) * HEAD_DIM:(kvh * Q_PER_KV + grp + 1) * HEAD_DIM, :]
                 for grp in range(Q_PER_KV)], axis=1)
            q_zero = jnp.zeros_like(q_heads)
            q2 = jnp.concatenate([q_heads, q_zero] if half == 0 else [q_zero, q_heads], axis=0)
            kwin = kall_ref[j * BLOCK:j * BLOCK + KEY_SPAN, pair * pair_lanes:(pair + 1) * pair_lanes]
            sc_ref[...] = _dot(kwin, q2)
            inv_l = []
            for grp in range(Q_PER_KV):
                hd = kvh * Q_PER_KV + grp
                lanes = slice(grp * BLOCK, (grp + 1) * BLOCK)
                s = sc_ref[:, lanes] + bias_ref[hd]
                if j == 0:
                    s = s + pen_first
                if j == SUB_BLOCKS - 1:
                    s = s + pen_last
                sink = sink_ref[hd]
                m = jnp.maximum(jnp.max(s, axis=0, keepdims=True), sink)
                p = jnp.exp(s - m)
                l = jnp.sum(p, axis=0, keepdims=True) + jnp.exp(sink - m)
                inv_l.append(1.0 / l)
                p_ref[:, lanes] = p.astype(_BF16)
            ot = _dot(vwin[kvh * HEAD_DIM:(kvh + 1) * HEAD_DIM, :], p_ref[...])
            for grp in range(Q_PER_KV):
                hd = kvh * Q_PER_KV + grp
                lanes = slice(grp * BLOCK, (grp + 1) * BLOCK)
                at_ref[j, hd * HEAD_DIM:(hd + 1) * HEAD_DIM, :] = (ot[:, lanes] * inv_l[grp]).astype(_BF16)

    for j in range(SUB_BLOCKS):
        rows = slice(j * BLOCK, (j + 1) * BLOCK)
        o_ref[rows, :] = x_ref[rows, :] + _dot_tn(at_ref[j], wo_ref[...])


def _t5_buckets(rel):
    nb = N_BUCKETS // 2
    ret = jnp.where(rel > 0, nb, 0)
    n = jnp.abs(rel)
    max_exact = nb // 2
    nf = jnp.maximum(n, 1).astype(jnp.float32)
    large = max_exact + (jnp.log(nf / max_exact) / math.log(MAX_DISTANCE / max_exact)
                         * (nb - max_exact)).astype(jnp.int32)
    large = jnp.minimum(large, nb - 1)
    return ret + jnp.where(n < max_exact, n, large)


def _attn_layer(x2, qt, k, vt, sink, rel_bias, w_out, seq_len):
    n, d = x2.shape
    t = TOKEN_TILE
    n_blocks = n // BLOCK
    kv_cols = N_KV_HEADS * HEAD_DIM
    q_cols = N_HEADS * HEAD_DIM
    kj = jnp.arange(KEY_SPAN, dtype=jnp.int32)[:, None]
    qi = jnp.arange(BLOCK, dtype=jnp.int32)[None, :]
    buckets_t = _t5_buckets(kj - BLOCK - qi).astype(jnp.int32)

    def prev_block(i):
        return jnp.maximum(i * SUB_BLOCKS - 1, 0)

    def next_block(i):
        return jnp.minimum((i + 1) * SUB_BLOCKS, n_blocks - 1)

    smem = pl.BlockSpec(memory_space=pltpu.SMEM)
    return pl.pallas_call(
        functools.partial(_attn_kernel, seq_len // t),
        grid=(n // t,),
        in_specs=[
            smem,
            smem,
            pl.BlockSpec((t, d), lambda i: (i, 0)),
            pl.BlockSpec((SUB_BLOCKS, q_cols, BLOCK), lambda i: (i, 0, 0)),
            pl.BlockSpec((t, kv_cols), lambda i: (i, 0)),
            pl.BlockSpec((BLOCK, kv_cols), lambda i: (prev_block(i), 0)),
            pl.BlockSpec((BLOCK, kv_cols), lambda i: (next_block(i), 0)),
            pl.BlockSpec((SUB_BLOCKS, kv_cols, BLOCK), lambda i: (i, 0, 0)),
            pl.BlockSpec((1, kv_cols, BLOCK), lambda i: (prev_block(i), 0, 0)),
            pl.BlockSpec((1, kv_cols, BLOCK), lambda i: (next_block(i), 0, 0)),
            _const_spec((KEY_SPAN, BLOCK)),
            _const_spec((d, d)),
        ],
        out_specs=pl.BlockSpec((t, d), lambda i: (i, 0)),
        out_shape=jax.ShapeDtypeStruct((n, d), _F32),
        scratch_shapes=[
            pltpu.VMEM((N_HEADS, KEY_SPAN, BLOCK), _F32),
            pltpu.VMEM((TOKEN_TILE + 2 * BLOCK, kv_cols), _BF16),
            pltpu.VMEM((SUB_BLOCKS + 2, kv_cols, BLOCK), _BF16),
            pltpu.VMEM((KEY_SPAN, Q_PER_KV * BLOCK), _F32),
            pltpu.VMEM((KEY_SPAN, Q_PER_KV * BLOCK), _BF16),
            pltpu.VMEM((SUB_BLOCKS, q_cols, BLOCK), _BF16),
        ],
        compiler_params=_params(),
        name="window_attention",
    )(sink, rel_bias.reshape(-1), x2, qt, k, k, k, vt, vt, vt, buckets_t, w_out.astype(_BF16))


def kernel(x, norm_mix, norm_ffn, even_w_in, even_v_ln_g, even_v_ln_b, even_w_spatial, even_b_spatial,
           even_conv_w, even_w_out, attn_w_qkv, attn_sink, rel_bias, attn_w_out, ffn_w_gate, ffn_w_up,
           ffn_w_down, final_norm):
    bsz, seq_len, d = x.shape
    assert d == D_MODEL and seq_len % TOKEN_TILE == 0
    x2 = x.reshape(bsz * seq_len, d)
    x2 = _even_layer(x2, norm_mix[0], even_w_in[0], even_v_ln_g[0], even_v_ln_b[0], even_w_spatial[0],
                     even_b_spatial[0], even_conv_w[0], even_w_out[0], seq_len)
    x2 = _ffn_layer(x2, norm_ffn[0], ffn_w_gate[0], ffn_w_up[0], ffn_w_down[0])
    qt, k, vt = _qkv_layer(x2, norm_mix[1], attn_w_qkv[0])
    x2 = _attn_layer(x2, qt, k, vt, attn_sink[0], rel_bias, attn_w_out[0], seq_len)
    x2 = _ffn_layer(x2, norm_ffn[1], ffn_w_gate[1], ffn_w_up[1], ffn_w_down[1], final_g=final_norm)
    return x2.reshape(bsz, seq_len, d)
```

```python
import functools
import math

import jax
import jax.numpy as jnp
from jax import lax
from jax.experimental import pallas as pl
from jax.experimental.pallas import tpu as pltpu

D_MODEL = 1024
EPS = 1e-6
A_WIDTH = 512
B_WIDTH = 512
A_GROUPS = 4
A_GROUP_DIM = 128
CHUNK = 128
CONV_W = 3
N_HEADS = 16
N_KV_HEADS = 4
Q_PER_KV = 4
HEAD_DIM = 64
WINDOW = 128
BLOCK = 128
N_BUCKETS = 32
MAX_DISTANCE = 128
D_FF = 2816

TOKEN_TILE = 512
HALO = 16
FF_CHUNK = 256
N_FF_CHUNKS = D_FF // FF_CHUNK
SUB_BLOCKS = TOKEN_TILE // BLOCK
KEY_SPAN = 3 * BLOCK
MASK_VALUE = -1e30
VMEM_LIMIT_BYTES = 56 * 1024 * 1024

_BF16 = jnp.bfloat16
_F32 = jnp.float32


def _dot(a, b):
    return jnp.dot(a, b, preferred_element_type=_F32)


def _dot_nt(a, b):
    return lax.dot_general(a, b, (((1,), (1,)), ((), ())), preferred_element_type=_F32)


def _dot_tn(a, b):
    return lax.dot_general(a, b, (((0,), (0,)), ((), ())), preferred_element_type=_F32)


def _gelu(x):
    return 0.5 * x * (1.0 + lax.erf(x * math.sqrt(0.5)))


def _rms(x, g):
    return (x * lax.rsqrt(jnp.mean(x * x, axis=-1, keepdims=True) + EPS)) * g


def _const_spec(shape):
    zeros = (0,) * len(shape)
    return pl.BlockSpec(shape, lambda i: zeros, pipeline_mode=pl.Buffered(1))


def _params():
    return pltpu.CompilerParams(
        dimension_semantics=("arbitrary",), vmem_limit_bytes=VMEM_LIMIT_BYTES)


def _even_kernel(tiles_per_seq, x_ref, xp_ref, xn_ref, g_ref, wa_ref, wc_ref, lng_ref, lnb_ref,
                 wsp_ref, bsp_ref, cw_ref, wo_ref, o_ref, hs_ref, y_ref):
    t = TOKEN_TILE
    i = pl.program_id(0)
    pos = i % tiles_per_seq
    g = g_ref[...]
    x = x_ref[...]
    h = _rms(x, g).astype(_BF16)
    hp = jnp.where(pos == 0, 0.0, _rms(xp_ref[...], g)).astype(_BF16)
    hn = jnp.where(pos == tiles_per_seq - 1, 0.0, _rms(xn_ref[...], g)).astype(_BF16)
    hs_ref[0:HALO, :] = hp
    hs_ref[HALO:HALO + t, :] = h
    hs_ref[HALO + t:, :] = hn

    pa = _dot(h, wa_ref[...])
    pc = _dot(hs_ref[...], wc_ref[...])

    u = _gelu(pa[:, :A_WIDTH])
    v = _gelu(pa[:, A_WIDTH:2 * A_WIDTH])
    mu = jnp.mean(v, axis=-1, keepdims=True)
    vc = v - mu
    vn = vc * lax.rsqrt(jnp.mean(vc * vc, axis=-1, keepdims=True) + EPS)
    vb = (vn * lng_ref[...] + lnb_ref[...]).astype(_BF16)
    for c in range(t // CHUNK):
        rows = slice(c * CHUNK, (c + 1) * CHUNK)
        for grp in range(A_GROUPS):
            cols = slice(grp * A_GROUP_DIM, (grp + 1) * A_GROUP_DIM)
            mixed = _dot(wsp_ref[grp], vb[rows, cols]) + bsp_ref[grp]
            y_ref[rows, cols] = (u[rows, cols] * mixed).astype(_BF16)

    z = pc[:, :B_WIDTH] * pc[:, B_WIDTH:]
    n_rows = t + 2 * HALO
    z_prev = pltpu.roll(z, 1, 0)[HALO:HALO + t]
    z_next = pltpu.roll(z, n_rows - 1, 0)[HALO:HALO + t]
    cw = cw_ref[...]
    conv = z_prev * cw[0:1] + z[HALO:HALO + t] * cw[1:2] + z_next * cw[2:3]
    y_ref[:, A_WIDTH:] = (pa[:, 2 * A_WIDTH:] * conv).astype(_BF16)

    o_ref[...] = x + _dot(y_ref[...], wo_ref[...])


def _even_layer(x2, g, w_in, ln_g, ln_b, w_sp, b_sp, conv_w, w_out, seq_len):
    n, d = x2.shape
    t = TOKEN_TILE
    halo_blocks_per_tile = t // HALO
    n_halo_blocks = n // HALO
    in_width_a = 2 * A_WIDTH + B_WIDTH
    wa = w_in[:, :in_width_a].astype(_BF16)
    wc = w_in[:, in_width_a:].astype(_BF16)
    return pl.pallas_call(
        functools.partial(_even_kernel, seq_len // t),
        grid=(n // t,),
        in_specs=[
            pl.BlockSpec((t, d), lambda i: (i, 0)),
            pl.BlockSpec((HALO, d), lambda i: (jnp.maximum(i * halo_blocks_per_tile - 1, 0), 0)),
            pl.BlockSpec((HALO, d),
                         lambda i: (jnp.minimum((i + 1) * halo_blocks_per_tile, n_halo_blocks - 1), 0)),
            _const_spec((1, d)),
            _const_spec(wa.shape),
            _const_spec(wc.shape),
            _const_spec((1, A_WIDTH)),
            _const_spec((1, A_WIDTH)),
            _const_spec((A_GROUPS, CHUNK, CHUNK)),
            _const_spec((A_GROUPS, CHUNK, 1)),
            _const_spec((CONV_W, B_WIDTH)),
            _const_spec((d, d)),
        ],
        out_specs=pl.BlockSpec((t, d), lambda i: (i, 0)),
        out_shape=jax.ShapeDtypeStruct((n, d), _F32),
        scratch_shapes=[
            pltpu.VMEM((t + 2 * HALO, d), _BF16),
            pltpu.VMEM((t, d), _BF16),
        ],
        compiler_params=_params(),
        name="even_mixer",
    )(x2, x2, x2, g.reshape(1, d), wa, wc, ln_g.reshape(1, -1), ln_b.reshape(1, -1),
      w_sp.astype(_BF16), b_sp[..., None], conv_w, w_out.astype(_BF16))


def _ffn_kernel(has_final_norm, x_ref, g_ref, wg_ref, wu_ref, wd_ref, *rest):
    if has_final_norm:
        gf_ref, o_ref, h_ref, acc_ref = rest
    else:
        o_ref, h_ref, acc_ref = rest
    x = x_ref[...]
    h_ref[...] = _rms(x, g_ref[...]).astype(_BF16)
    acc_ref[...] = x

    def chunk(c, carry):
        h = h_ref[...]
        gate = _dot(h, wg_ref[c])
        up = _dot(h, wu_ref[c])
        act = (jax.nn.silu(gate) * up).astype(_BF16)
        acc_ref[...] += _dot(act, wd_ref[c])
        return carry

    lax.fori_loop(0, N_FF_CHUNKS, chunk, 0, unroll=True)
    out = acc_ref[...]
    if has_final_norm:
        out = _rms(out, gf_ref[...])
    o_ref[...] = out


def _chunk_columns(w):
    d = w.shape[0]
    return w.astype(_BF16).reshape(d, N_FF_CHUNKS, FF_CHUNK).transpose(1, 0, 2)


def _ffn_layer(x2, g, w_gate, w_up, w_down, final_g=None):
    n, d = x2.shape
    t = TOKEN_TILE
    wg = _chunk_columns(w_gate)
    wu = _chunk_columns(w_up)
    wd = w_down.astype(_BF16).reshape(N_FF_CHUNKS, FF_CHUNK, d)
    in_specs = [
        pl.BlockSpec((t, d), lambda i: (i, 0)),
        _const_spec((1, d)),
        _const_spec(wg.shape),
        _const_spec(wu.shape),
        _const_spec(wd.shape),
    ]
    args = [x2, g.reshape(1, d), wg, wu, wd]
    if final_g is not None:
        in_specs.append(_const_spec((1, d)))
        args.append(final_g.reshape(1, d))
    return pl.pallas_call(
        functools.partial(_ffn_kernel, final_g is not None),
        grid=(n // t,),
        in_specs=in_specs,
        out_specs=pl.BlockSpec((t, d), lambda i: (i, 0)),
        out_shape=jax.ShapeDtypeStruct((n, d), _F32),
        scratch_shapes=[pltpu.VMEM((t, d), _BF16), pltpu.VMEM((t, d), _F32)],
        compiler_params=_params(),
        name="swiglu_ffn",
    )(*args)


def _qkv_kernel(x_ref, g_ref, wqv_ref, wk_ref, qt_ref, k_ref, vt_ref):
    h = _rms(x_ref[...], g_ref[...]).astype(_BF16)
    qv = _dot_nt(wqv_ref[...], h)
    q_rows = N_HEADS * HEAD_DIM
    scale = HEAD_DIM ** -0.5
    for j in range(SUB_BLOCKS):
        lanes = slice(j * BLOCK, (j + 1) * BLOCK)
        qt_ref[j] = (qv[:q_rows, lanes] * scale).astype(_BF16)
        vt_ref[j] = qv[q_rows:, lanes].astype(_BF16)
    k_ref[...] = _dot(h, wk_ref[...]).astype(_BF16)


def _qkv_layer(x2, g, w_qkv):
    n, d = x2.shape
    t = TOKEN_TILE
    q_cols = N_HEADS * HEAD_DIM
    kv_cols = N_KV_HEADS * HEAD_DIM
    wq = w_qkv[:, :q_cols]
    wk = w_qkv[:, q_cols:q_cols + kv_cols]
    wv = w_qkv[:, q_cols + kv_cols:]
    wqv_t = jnp.concatenate([wq, wv], axis=1).T.astype(_BF16)
    n_blocks = n // BLOCK
    return pl.pallas_call(
        _qkv_kernel,
        grid=(n // t,),
        in_specs=[
            pl.BlockSpec((t, d), lambda i: (i, 0)),
            _const_spec((1, d)),
            _const_spec(wqv_t.shape),
            _const_spec((d, kv_cols)),
        ],
        out_specs=[
            pl.BlockSpec((SUB_BLOCKS, q_cols, BLOCK), lambda i: (i, 0, 0)),
            pl.BlockSpec((t, kv_cols), lambda i: (i, 0)),
            pl.BlockSpec((SUB_BLOCKS, kv_cols, BLOCK), lambda i: (i, 0, 0)),
        ],
        out_shape=[
            jax.ShapeDtypeStruct((n_blocks, q_cols, BLOCK), _BF16),
            jax.ShapeDtypeStruct((n, kv_cols), _BF16),
            jax.ShapeDtypeStruct((n_blocks, kv_cols, BLOCK), _BF16),
        ],
        compiler_params=_params(),
        name="qkv_proj",
    )(x2, g.reshape(1, d), wqv_t, wk.astype(_BF16))


def _attn_kernel(tiles_per_seq, sink_ref, rb_ref, x_ref, qt_ref, kc_ref, kp_ref, kn_ref,
                 vc_ref, vp_ref, vn_ref, bkt_ref, wo_ref, o_ref,
                 bias_ref, kall_ref, vall_ref, sc_ref, p_ref, at_ref):
    i = pl.program_id(0)
    pos = i % tiles_per_seq

    @pl.when(i == 0)
    def _init():
        bkt = bkt_ref[...]
        rel = (lax.broadcasted_iota(jnp.int32, (KEY_SPAN, BLOCK), 0) - BLOCK
               - lax.broadcasted_iota(jnp.int32, (KEY_SPAN, BLOCK), 1))
        band = jnp.abs(rel) <= WINDOW
        for hd in range(N_HEADS):
            tbl = jnp.zeros((KEY_SPAN, BLOCK), _F32)
            for b in range(N_BUCKETS):
                tbl = jnp.where(bkt == b, rb_ref[b * N_HEADS + hd], tbl)
            bias_ref[hd] = jnp.where(band, tbl, MASK_VALUE)

    kall_ref[0:BLOCK, :] = kp_ref[...]
    kall_ref[BLOCK:BLOCK + TOKEN_TILE, :] = kc_ref[...]
    kall_ref[BLOCK + TOKEN_TILE:, :] = kn_ref[...]
    vall_ref[0] = vp_ref[0]
    for j in range(SUB_BLOCKS):
        vall_ref[1 + j] = vc_ref[j]
    vall_ref[1 + SUB_BLOCKS] = vn_ref[0]

    key_row = lax.broadcasted_iota(jnp.int32, (KEY_SPAN, 1), 0)
    pen_first = jnp.where((key_row < BLOCK) & (pos == 0), MASK_VALUE, 0.0)
    pen_last = jnp.where((key_row >= 2 * BLOCK) & (pos == tiles_per_seq - 1), MASK_VALUE, 0.0)

    pair_lanes = 2 * HEAD_DIM
    for j in range(SUB_BLOCKS):
        vwin = jnp.concatenate([vall_ref[j], vall_ref[j + 1], vall_ref[j + 2]], axis=1)
        for kvh in range(N_KV_HEADS):
            pair = kvh // 2
            half = kvh % 2
            q_heads = jnp.concatenate(
                [qt_ref[j, (kvh * Q_PER_KV + grp) * HEAD_DIM:(kvh * Q_PER_KV + grp + 1) * HEAD_DIM, :]
                 for grp in range(Q_PER_KV)], axis=1)
            q_zero = jnp.zeros_like(q_heads)
            q2 = jnp.concatenate([q_heads, q_zero] if half == 0 else [q_zero, q_heads], axis=0)
            kwin = kall_ref[j * BLOCK:j * BLOCK + KEY_SPAN, pair * pair_lanes:(pair + 1) * pair_lanes]
            sc_ref[...] = _dot(kwin, q2)
            inv_l = []
            for grp in range(Q_PER_KV):
                hd = kvh * Q_PER_KV + grp
                lanes = slice(grp * BLOCK, (grp + 1) * BLOCK)
                s = sc_ref[:, lanes] + bias_ref[hd]
                if j == 0:
                    s = s + pen_first
                if j == SUB_BLOCKS - 1:
                    s = s + pen_last
                sink = sink_ref[hd]
                m = jnp.maximum(jnp.max(s, axis=0, keepdims=True), sink)
                p = jnp.exp(s - m)
                l = jnp.sum(p, axis=0, keepdims=True) + jnp.exp(sink - m)
                inv_l.append(1.0 / l)
                p_ref[:, lanes] = p.astype(_BF16)
            ot = _dot(vwin[kvh * HEAD_DIM:(kvh + 1) * HEAD_DIM, :], p_ref[...])
            for grp in range(Q_PER_KV):
                hd = kvh * Q_PER_KV + grp
                lanes = slice(grp * BLOCK, (grp + 1) * BLOCK)
                at_ref[j, hd * HEAD_DIM:(hd + 1) * HEAD_DIM, :] = (ot[:, lanes] * inv_l[grp]).astype(_BF16)

    for j in range(SUB_BLOCKS):
        rows = slice(j * BLOCK, (j + 1) * BLOCK)
        o_ref[rows, :] = x_ref[rows, :] + _dot_tn(at_ref[j], wo_ref[...])


def _t5_buckets(rel):
    nb = N_BUCKETS // 2
    ret = jnp.where(rel > 0, nb, 0)
    n = jnp.abs(rel)
    max_exact = nb // 2
    nf = jnp.maximum(n, 1).astype(jnp.float32)
    large = max_exact + (jnp.log(nf / max_exact) / math.log(MAX_DISTANCE / max_exact)
                         * (nb - max_exact)).astype(jnp.int32)
    large = jnp.minimum(large, nb - 1)
    return ret + jnp.where(n < max_exact, n, large)


def _attn_layer(x2, qt, k, vt, sink, rel_bias, w_out, seq_len):
    n, d = x2.shape
    t = TOKEN_TILE
    n_blocks = n // BLOCK
    kv_cols = N_KV_HEADS * HEAD_DIM
    q_cols = N_HEADS * HEAD_DIM
    kj = jnp.arange(KEY_SPAN, dtype=jnp.int32)[:, None]
    qi = jnp.arange(BLOCK, dtype=jnp.int32)[None, :]
    buckets_t = _t5_buckets(kj - BLOCK - qi).astype(jnp.int32)

    def prev_block(i):
        return jnp.maximum(i * SUB_BLOCKS - 1, 0)

    def next_block(i):
        return jnp.minimum((i + 1) * SUB_BLOCKS, n_blocks - 1)

    smem = pl.BlockSpec(memory_space=pltpu.SMEM)
    return pl.pallas_call(
        functools.partial(_attn_kernel, seq_len // t),
        grid=(n // t,),
        in_specs=[
            smem,
            smem,
            pl.BlockSpec((t, d), lambda i: (i, 0)),
            pl.BlockSpec((SUB_BLOCKS, q_cols, BLOCK), lambda i: (i, 0, 0)),
            pl.BlockSpec((t, kv_cols), lambda i: (i, 0)),
            pl.BlockSpec((BLOCK, kv_cols), lambda i: (prev_block(i), 0)),
            pl.BlockSpec((BLOCK, kv_cols), lambda i: (next_block(i), 0)),
            pl.BlockSpec((SUB_BLOCKS, kv_cols, BLOCK), lambda i: (i, 0, 0)),
            pl.BlockSpec((1, kv_cols, BLOCK), lambda i: (prev_block(i), 0, 0)),
            pl.BlockSpec((1, kv_cols, BLOCK), lambda i: (next_block(i), 0, 0)),
            _const_spec((KEY_SPAN, BLOCK)),
            _const_spec((d, d)),
        ],
        out_specs=pl.BlockSpec((t, d), lambda i: (i, 0)),
        out_shape=jax.ShapeDtypeStruct((n, d), _F32),
        scratch_shapes=[
            pltpu.VMEM((N_HEADS, KEY_SPAN, BLOCK), _F32),
            pltpu.VMEM((TOKEN_TILE + 2 * BLOCK, kv_cols), _BF16),
            pltpu.VMEM((SUB_BLOCKS + 2, kv_cols, BLOCK), _BF16),
            pltpu.VMEM((KEY_SPAN, Q_PER_KV * BLOCK), _F32),
            pltpu.VMEM((KEY_SPAN, Q_PER_KV * BLOCK), _BF16),
            pltpu.VMEM((SUB_BLOCKS, q_cols, BLOCK), _BF16),
        ],
        compiler_params=_params(),
        name="window_attention",
    )(sink, rel_bias.reshape(-1), x2, qt, k, k, k, vt, vt, vt, buckets_t, w_out.astype(_BF16))


def kernel(x, norm_mix, norm_ffn, even_w_in, even_v_ln_g, even_v_ln_b, even_w_spatial, even_b_spatial,
           even_conv_w, even_w_out, attn_w_qkv, attn_sink, rel_bias, attn_w_out, ffn_w_gate, ffn_w_up,
           ffn_w_down, final_norm):
    bsz, seq_len, d = x.shape
    assert d == D_MODEL and seq_len % TOKEN_TILE == 0
    x2 = x.reshape(bsz * seq_len, d)
    x2 = _even_layer(x2, norm_mix[0], even_w_in[0], even_v_ln_g[0], even_v_ln_b[0], even_w_spatial[0],
                     even_b_spatial[0], even_conv_w[0], even_w_out[0], seq_len)
    x2 = _ffn_layer(x2, norm_ffn[0], ffn_w_gate[0], ffn_w_up[0], ffn_w_down[0])
    qt, k, vt = _qkv_layer(x2, norm_mix[1], attn_w_qkv[0])
    x2 = _attn_layer(x2, qt, k, vt, attn_sink[0], rel_bias, attn_w_out[0], seq_len)
    x2 = _ffn_layer(x2, norm_ffn[1], ffn_w_gate[1], ffn_w_up[1], ffn_w_down[1], final_g=final_norm)
    return x2.reshape(bsz, seq_len, d)
```

```python
import functools
import math

import jax
import jax.numpy as jnp
from jax import lax
from jax.experimental import pallas as pl
from jax.experimental.pallas import tpu as pltpu

D_MODEL = 1024
EPS = 1e-6
A_WIDTH = 512
B_WIDTH = 512
A_GROUPS = 4
A_GROUP_DIM = 128
CHUNK = 128
CONV_W = 3
N_HEADS = 16
N_KV_HEADS = 4
Q_PER_KV = 4
HEAD_DIM = 64
WINDOW = 128
BLOCK = 128
N_BUCKETS = 32
MAX_DISTANCE = 128
D_FF = 2816

TOKEN_TILE = 512
HALO = 16
FF_CHUNK = 256
N_FF_CHUNKS = D_FF // FF_CHUNK
SUB_BLOCKS = TOKEN_TILE // BLOCK
KEY_SPAN = 3 * BLOCK
MASK_VALUE = -1e30
ROW_SLAB = 64
VMEM_LIMIT_BYTES = 56 * 1024 * 1024

_BF16 = jnp.bfloat16
_F32 = jnp.float32


def _dot(a, b):
    return jnp.dot(a, b, preferred_element_type=_F32)


def _dot_nt(a, b):
    return lax.dot_general(a, b, (((1,), (1,)), ((), ())), preferred_element_type=_F32)


def _dot_tn(a, b):
    return lax.dot_general(a, b, (((0,), (0,)), ((), ())), preferred_element_type=_F32)


def _gelu(x):
    return 0.5 * x * (1.0 + lax.erf(x * math.sqrt(0.5)))


def _rms(x, g):
    return (x * lax.rsqrt(jnp.mean(x * x, axis=-1, keepdims=True) + EPS)) * g


def _const_spec(shape):
    zeros = (0,) * len(shape)
    return pl.BlockSpec(shape, lambda i: zeros, pipeline_mode=pl.Buffered(1))


def _params():
    return pltpu.CompilerParams(
        dimension_semantics=("arbitrary",), vmem_limit_bytes=VMEM_LIMIT_BYTES)


def _even_kernel(tiles_per_seq, x_ref, xp_ref, xn_ref, g_ref, wa_ref, wc_ref, lng_ref, lnb_ref,
                 wsp_ref, bsp_ref, cw_ref, wo_ref, o_ref, hs_ref, y_ref):
    t = TOKEN_TILE
    i = pl.program_id(0)
    pos = i % tiles_per_seq
    g = g_ref[...]
    x = x_ref[...]
    h = _rms(x, g).astype(_BF16)
    hp = jnp.where(pos == 0, 0.0, _rms(xp_ref[...], g)).astype(_BF16)
    hn = jnp.where(pos == tiles_per_seq - 1, 0.0, _rms(xn_ref[...], g)).astype(_BF16)
    hs_ref[0:HALO, :] = hp
    hs_ref[HALO:HALO + t, :] = h
    hs_ref[HALO + t:, :] = hn

    pa = _dot(h, wa_ref[...])
    pc = _dot(hs_ref[...], wc_ref[...])

    u = _gelu(pa[:, :A_WIDTH])
    v = _gelu(pa[:, A_WIDTH:2 * A_WIDTH])
    mu = jnp.mean(v, axis=-1, keepdims=True)
    vc = v - mu
    vn = vc * lax.rsqrt(jnp.mean(vc * vc, axis=-1, keepdims=True) + EPS)
    vb = (vn * lng_ref[...] + lnb_ref[...]).astype(_BF16)
    for c in range(t // CHUNK):
        rows = slice(c * CHUNK, (c + 1) * CHUNK)
        for grp in range(A_GROUPS):
            cols = slice(grp * A_GROUP_DIM, (grp + 1) * A_GROUP_DIM)
            mixed = _dot(wsp_ref[grp], vb[rows, cols]) + bsp_ref[grp]
            y_ref[rows, cols] = (u[rows, cols] * mixed).astype(_BF16)

    z = pc[:, :B_WIDTH] * pc[:, B_WIDTH:]
    n_rows = t + 2 * HALO
    z_prev = pltpu.roll(z, 1, 0)[HALO:HALO + t]
    z_next = pltpu.roll(z, n_rows - 1, 0)[HALO:HALO + t]
    cw = cw_ref[...]
    conv = z_prev * cw[0:1] + z[HALO:HALO + t] * cw[1:2] + z_next * cw[2:3]
    y_ref[:, A_WIDTH:] = (pa[:, 2 * A_WIDTH:] * conv).astype(_BF16)

    o_ref[...] = x + _dot(y_ref[...], wo_ref[...])


def _even_layer(x2, g, w_in, ln_g, ln_b, w_sp, b_sp, conv_w, w_out, seq_len):
    n, d = x2.shape
    t = TOKEN_TILE
    halo_blocks_per_tile = t // HALO
    n_halo_blocks = n // HALO
    in_width_a = 2 * A_WIDTH + B_WIDTH
    wa = w_in[:, :in_width_a].astype(_BF16)
    wc = w_in[:, in_width_a:].astype(_BF16)
    return pl.pallas_call(
        functools.partial(_even_kernel, seq_len // t),
        grid=(n // t,),
        in_specs=[
            pl.BlockSpec((t, d), lambda i: (i, 0)),
            pl.BlockSpec((HALO, d), lambda i: (jnp.maximum(i * halo_blocks_per_tile - 1, 0), 0)),
            pl.BlockSpec((HALO, d),
                         lambda i: (jnp.minimum((i + 1) * halo_blocks_per_tile, n_halo_blocks - 1), 0)),
            _const_spec((1, d)),
            _const_spec(wa.shape),
            _const_spec(wc.shape),
            _const_spec((1, A_WIDTH)),
            _const_spec((1, A_WIDTH)),
            _const_spec((A_GROUPS, CHUNK, CHUNK)),
            _const_spec((A_GROUPS, CHUNK, 1)),
            _const_spec((CONV_W, B_WIDTH)),
            _const_spec((d, d)),
        ],
        out_specs=pl.BlockSpec((t, d), lambda i: (i, 0)),
        out_shape=jax.ShapeDtypeStruct((n, d), _F32),
        scratch_shapes=[
            pltpu.VMEM((t + 2 * HALO, d), _BF16),
            pltpu.VMEM((t, d), _BF16),
        ],
        compiler_params=_params(),
        name="even_mixer",
    )(x2, x2, x2, g.reshape(1, d), wa, wc, ln_g.reshape(1, -1), ln_b.reshape(1, -1),
      w_sp.astype(_BF16), b_sp[..., None], conv_w, w_out.astype(_BF16))


def _ffn_kernel(has_final_norm, x_ref, g_ref, wg_ref, wu_ref, wd_ref, *rest):
    if has_final_norm:
        gf_ref, o_ref, h_ref, acc_ref = rest
    else:
        o_ref, h_ref, acc_ref = rest
    x = x_ref[...]
    h_ref[...] = _rms(x, g_ref[...]).astype(_BF16)
    acc_ref[...] = x

    def chunk(c, carry):
        h = h_ref[...]
        gate = _dot(h, wg_ref[c])
        up = _dot(h, wu_ref[c])
        act = (jax.nn.silu(gate) * up).astype(_BF16)
        acc_ref[...] += _dot(act, wd_ref[c])
        return carry

    lax.fori_loop(0, N_FF_CHUNKS, chunk, 0, unroll=True)
    out = acc_ref[...]
    if has_final_norm:
        out = _rms(out, gf_ref[...])
    o_ref[...] = out


def _chunk_columns(w):
    d = w.shape[0]
    return w.astype(_BF16).reshape(d, N_FF_CHUNKS, FF_CHUNK).transpose(1, 0, 2)


def _ffn_layer(x2, g, w_gate, w_up, w_down, final_g=None):
    n, d = x2.shape
    t = TOKEN_TILE
    wg = _chunk_columns(w_gate)
    wu = _chunk_columns(w_up)
    wd = w_down.astype(_BF16).reshape(N_FF_CHUNKS, FF_CHUNK, d)
    in_specs = [
        pl.BlockSpec((t, d), lambda i: (i, 0)),
        _const_spec((1, d)),
        _const_spec(wg.shape),
        _const_spec(wu.shape),
        _const_spec(wd.shape),
    ]
    args = [x2, g.reshape(1, d), wg, wu, wd]
    if final_g is not None:
        in_specs.append(_const_spec((1, d)))
        args.append(final_g.reshape(1, d))
    return pl.pallas_call(
        functools.partial(_ffn_kernel, final_g is not None),
        grid=(n // t,),
        in_specs=in_specs,
        out_specs=pl.BlockSpec((t, d), lambda i: (i, 0)),
        out_shape=jax.ShapeDtypeStruct((n, d), _F32),
        scratch_shapes=[pltpu.VMEM((t, d), _BF16), pltpu.VMEM((t, d), _F32)],
        compiler_params=_params(),
        name="swiglu_ffn",
    )(*args)


def _qkv_kernel(x_ref, g_ref, wqv_ref, wk_ref, qt_ref, k_ref, vt_ref):
    h = _rms(x_ref[...], g_ref[...]).astype(_BF16)
    qv = _dot_nt(wqv_ref[...], h)
    q_rows = N_HEADS * HEAD_DIM
    scale = HEAD_DIM ** -0.5
    for j in range(SUB_BLOCKS):
        lanes = slice(j * BLOCK, (j + 1) * BLOCK)
        qt_ref[j] = (qv[:q_rows, lanes] * scale).astype(_BF16)
        vt_ref[j] = qv[q_rows:, lanes].astype(_BF16)
    k_ref[...] = _dot(h, wk_ref[...]).astype(_BF16)


def _qkv_layer(x2, g, w_qkv):
    n, d = x2.shape
    t = TOKEN_TILE
    q_cols = N_HEADS * HEAD_DIM
    kv_cols = N_KV_HEADS * HEAD_DIM
    wq = w_qkv[:, :q_cols]
    wk = w_qkv[:, q_cols:q_cols + kv_cols]
    wv = w_qkv[:, q_cols + kv_cols:]
    wqv_t = jnp.concatenate([wq, wv], axis=1).T.astype(_BF16)
    n_blocks = n // BLOCK
    return pl.pallas_call(
        _qkv_kernel,
        grid=(n // t,),
        in_specs=[
            pl.BlockSpec((t, d), lambda i: (i, 0)),
            _const_spec((1, d)),
            _const_spec(wqv_t.shape),
            _const_spec((d, kv_cols)),
        ],
        out_specs=[
            pl.BlockSpec((SUB_BLOCKS, q_cols, BLOCK), lambda i: (i, 0, 0)),
            pl.BlockSpec((t, kv_cols), lambda i: (i, 0)),
            pl.BlockSpec((SUB_BLOCKS, kv_cols, BLOCK), lambda i: (i, 0, 0)),
        ],
        out_shape=[
            jax.ShapeDtypeStruct((n_blocks, q_cols, BLOCK), _BF16),
            jax.ShapeDtypeStruct((n, kv_cols), _BF16),
            jax.ShapeDtypeStruct((n_blocks, kv_cols, BLOCK), _BF16),
        ],
        compiler_params=_params(),
        name="qkv_proj",
    )(x2, g.reshape(1, d), wqv_t, wk.astype(_BF16))


def _attn_kernel(tiles_per_seq, sink_ref, rb_ref, x_ref, qt_ref, kc_ref, kp_ref, kn_ref,
                 vc_ref, vp_ref, vn_ref, bkt_ref, wo_ref, o_ref,
                 bias_ref, kall_ref, vall_ref, sc_ref, p_ref, at_ref):
    i = pl.program_id(0)
    pos = i % tiles_per_seq

    @pl.when(i == 0)
    def _init():
        bkt = bkt_ref[...]
        rel = (lax.broadcasted_iota(jnp.int32, (KEY_SPAN, BLOCK), 0) - BLOCK
               - lax.broadcasted_iota(jnp.int32, (KEY_SPAN, BLOCK), 1))
        band = jnp.abs(rel) <= WINDOW
        for hd in range(N_HEADS):
            tbl = jnp.zeros((KEY_SPAN, BLOCK), _F32)
            for b in range(N_BUCKETS):
                tbl = jnp.where(bkt == b, rb_ref[b * N_HEADS + hd], tbl)
            bias_ref[hd] = jnp.where(band, tbl, MASK_VALUE)

    kall_ref[0:BLOCK, :] = kp_ref[...]
    kall_ref[BLOCK:BLOCK + TOKEN_TILE, :] = kc_ref[...]
    kall_ref[BLOCK + TOKEN_TILE:, :] = kn_ref[...]
    vall_ref[0] = vp_ref[0]
    for j in range(SUB_BLOCKS):
        vall_ref[1 + j] = vc_ref[j]
    vall_ref[1 + SUB_BLOCKS] = vn_ref[0]

    pen_first = jnp.where(pos == 0, MASK_VALUE, 0.0)
    pen_last = jnp.where(pos == tiles_per_seq - 1, MASK_VALUE, 0.0)

    pair_lanes = 2 * HEAD_DIM
    slabs = [slice(r0, r0 + ROW_SLAB) for r0 in range(0, KEY_SPAN, ROW_SLAB)]
    heads = [(j, kvh) for j in range(SUB_BLOCKS) for kvh in range(N_KV_HEADS)]

    def scores(t):
        j, kvh = heads[t]
        pair, half = kvh // 2, kvh % 2
        q_heads = jnp.concatenate(
            [qt_ref[j, (kvh * Q_PER_KV + grp) * HEAD_DIM:(kvh * Q_PER_KV + grp + 1) * HEAD_DIM, :]
             for grp in range(Q_PER_KV)], axis=1)
        q_zero = jnp.zeros_like(q_heads)
        q2 = jnp.concatenate([q_heads, q_zero] if half == 0 else [q_zero, q_heads], axis=0)
        kwin = kall_ref[j * BLOCK:j * BLOCK + KEY_SPAN, pair * pair_lanes:(pair + 1) * pair_lanes]
        sc_ref[t % 2] = _dot(kwin, q2)

    def softmax(t):
        j, kvh = heads[t]
        sc_buf = sc_ref.at[t % 2]
        p_buf = p_ref.at[t % 2]
        inv_l = []
        for grp in range(Q_PER_KV):
            hd = kvh * Q_PER_KV + grp
            lanes = slice(grp * BLOCK, (grp + 1) * BLOCK)
            mx = None
            for rows in slabs:
                s = sc_buf[rows, lanes] + bias_ref[hd, rows, :]
                if j == 0 and rows.stop <= BLOCK:
                    s = s + pen_first
                if j == SUB_BLOCKS - 1 and rows.start >= 2 * BLOCK:
                    s = s + pen_last
                sc_buf[rows, lanes] = s
                mx = s if mx is None else jnp.maximum(mx, s)
            sink = sink_ref[hd]
            m = jnp.maximum(jnp.max(mx, axis=0, keepdims=True), sink)
            acc = None
            for rows in slabs:
                p = jnp.exp(sc_buf[rows, lanes] - m)
                acc = p if acc is None else acc + p
                p_buf[rows, lanes] = p.astype(_BF16)
            l = jnp.sum(acc, axis=0, keepdims=True) + jnp.exp(sink - m)
            inv_l.append(1.0 / l)
        return inv_l

    def weighted_values(t, inv_l):
        j, kvh = heads[t]
        vwin = jnp.concatenate(
            [vall_ref[j + b, kvh * HEAD_DIM:(kvh + 1) * HEAD_DIM, :] for b in range(3)], axis=1)
        ot = _dot(vwin, p_ref[t % 2])
        for grp in range(Q_PER_KV):
            hd = kvh * Q_PER_KV + grp
            at_ref[j, hd * HEAD_DIM:(hd + 1) * HEAD_DIM, :] = (
                ot[:, grp * BLOCK:(grp + 1) * BLOCK] * inv_l[grp]).astype(_BF16)

    scores(0)
    for t, (j, kvh) in enumerate(heads):
        if t + 1 < len(heads):
            scores(t + 1)
        inv_l = softmax(t)
        weighted_values(t, inv_l)
        if kvh == N_KV_HEADS - 1:
            rows = slice(j * BLOCK, (j + 1) * BLOCK)
            o_ref[rows, :] = x_ref[rows, :] + _dot_tn(at_ref[j], wo_ref[...])


def _t5_buckets(rel):
    nb = N_BUCKETS // 2
    ret = jnp.where(rel > 0, nb, 0)
    n = jnp.abs(rel)
    max_exact = nb // 2
    nf = jnp.maximum(n, 1).astype(jnp.float32)
    large = max_exact + (jnp.log(nf / max_exact) / math.log(MAX_DISTANCE / max_exact)
                         * (nb - max_exact)).astype(jnp.int32)
    large = jnp.minimum(large, nb - 1)
    return ret + jnp.where(n < max_exact, n, large)


def _attn_layer(x2, qt, k, vt, sink, rel_bias, w_out, seq_len):
    n, d = x2.shape
    t = TOKEN_TILE
    n_blocks = n // BLOCK
    kv_cols = N_KV_HEADS * HEAD_DIM
    q_cols = N_HEADS * HEAD_DIM
    kj = jnp.arange(KEY_SPAN, dtype=jnp.int32)[:, None]
    qi = jnp.arange(BLOCK, dtype=jnp.int32)[None, :]
    buckets_t = _t5_buckets(kj - BLOCK - qi).astype(jnp.int32)

    def prev_block(i):
        return jnp.maximum(i * SUB_BLOCKS - 1, 0)

    def next_block(i):
        return jnp.minimum((i + 1) * SUB_BLOCKS, n_blocks - 1)

    smem = pl.BlockSpec(memory_space=pltpu.SMEM)
    return pl.pallas_call(
        functools.partial(_attn_kernel, seq_len // t),
        grid=(n // t,),
        in_specs=[
            smem,
            smem,
            pl.BlockSpec((t, d), lambda i: (i, 0)),
            pl.BlockSpec((SUB_BLOCKS, q_cols, BLOCK), lambda i: (i, 0, 0)),
            pl.BlockSpec((t, kv_cols), lambda i: (i, 0)),
            pl.BlockSpec((BLOCK, kv_cols), lambda i: (prev_block(i), 0)),
            pl.BlockSpec((BLOCK, kv_cols), lambda i: (next_block(i), 0)),
            pl.BlockSpec((SUB_BLOCKS, kv_cols, BLOCK), lambda i: (i, 0, 0)),
            pl.BlockSpec((1, kv_cols, BLOCK), lambda i: (prev_block(i), 0, 0)),
            pl.BlockSpec((1, kv_cols, BLOCK), lambda i: (next_block(i), 0, 0)),
            _const_spec((KEY_SPAN, BLOCK)),
            _const_spec((d, d)),
        ],
        out_specs=pl.BlockSpec((t, d), lambda i: (i, 0)),
        out_shape=jax.ShapeDtypeStruct((n, d), _F32),
        scratch_shapes=[
            pltpu.VMEM((N_HEADS, KEY_SPAN, BLOCK), _F32),
            pltpu.VMEM((TOKEN_TILE + 2 * BLOCK, kv_cols), _BF16),
            pltpu.VMEM((SUB_BLOCKS + 2, kv_cols, BLOCK), _BF16),
            pltpu.VMEM((2, KEY_SPAN, Q_PER_KV * BLOCK), _F32),
            pltpu.VMEM((2, KEY_SPAN, Q_PER_KV * BLOCK), _BF16),
            pltpu.VMEM((SUB_BLOCKS, q_cols, BLOCK), _BF16),
        ],
        compiler_params=_params(),
        name="window_attention",
    )(sink, rel_bias.reshape(-1), x2, qt, k, k, k, vt, vt, vt, buckets_t, w_out.astype(_BF16))


def kernel(x, norm_mix, norm_ffn, even_w_in, even_v_ln_g, even_v_ln_b, even_w_spatial, even_b_spatial,
           even_conv_w, even_w_out, attn_w_qkv, attn_sink, rel_bias, attn_w_out, ffn_w_gate, ffn_w_up,
           ffn_w_down, final_norm):
    bsz, seq_len, d = x.shape
    assert d == D_MODEL and seq_len % TOKEN_TILE == 0
    x2 = x.reshape(bsz * seq_len, d)
    x2 = _even_layer(x2, norm_mix[0], even_w_in[0], even_v_ln_g[0], even_v_ln_b[0], even_w_spatial[0],
                     even_b_spatial[0], even_conv_w[0], even_w_out[0], seq_len)
    x2 = _ffn_layer(x2, norm_ffn[0], ffn_w_gate[0], ffn_w_up[0], ffn_w_down[0])
    qt, k, vt = _qkv_layer(x2, norm_mix[1], attn_w_qkv[0])
    x2 = _attn_layer(x2, qt, k, vt, attn_sink[0], rel_bias, attn_w_out[0], seq_len)
    x2 = _ffn_layer(x2, norm_ffn[1], ffn_w_gate[1], ffn_w_up[1], ffn_w_down[1], final_g=final_norm)
    return x2.reshape(bsz, seq_len, d)
```

```python
import functools
import math

import jax
import jax.numpy as jnp
from jax import lax
from jax.experimental import pallas as pl
from jax.experimental.pallas import tpu as pltpu

D_MODEL = 1024
EPS = 1e-6
A_WIDTH = 512
B_WIDTH = 512
A_GROUPS = 4
A_GROUP_DIM = 128
CHUNK = 128
CONV_W = 3
N_HEADS = 16
N_KV_HEADS = 4
Q_PER_KV = 4
HEAD_DIM = 64
WINDOW = 128
BLOCK = 128
N_BUCKETS = 32
MAX_DISTANCE = 128
D_FF = 2816

TOKEN_TILE = 512
HALO = 16
FF_CHUNK = 256
N_FF_CHUNKS = D_FF // FF_CHUNK
SUB_BLOCKS = TOKEN_TILE // BLOCK
KEY_SPAN = 3 * BLOCK
MASK_VALUE = -1e30
ROW_SLAB = 64
VMEM_LIMIT_BYTES = 56 * 1024 * 1024

_BF16 = jnp.bfloat16
_F32 = jnp.float32


def _dot(a, b):
    return jnp.dot(a, b, preferred_element_type=_F32)


def _dot_nt(a, b):
    return lax.dot_general(a, b, (((1,), (1,)), ((), ())), preferred_element_type=_F32)


def _dot_tn(a, b):
    return lax.dot_general(a, b, (((0,), (0,)), ((), ())), preferred_element_type=_F32)


def _gelu(x):
    return 0.5 * x * (1.0 + lax.erf(x * math.sqrt(0.5)))


def _rms(x, g):
    return (x * lax.rsqrt(jnp.mean(x * x, axis=-1, keepdims=True) + EPS)) * g


def _const_spec(shape):
    zeros = (0,) * len(shape)
    return pl.BlockSpec(shape, lambda i: zeros, pipeline_mode=pl.Buffered(1))


def _params():
    return pltpu.CompilerParams(
        dimension_semantics=("arbitrary",), vmem_limit_bytes=VMEM_LIMIT_BYTES)


def _even_kernel(tiles_per_seq, x_ref, xp_ref, xn_ref, g_ref, win_ref, lng_ref, lnb_ref,
                 wsp_ref, bsp_ref, cw_ref, wo_ref, o_ref, hs_ref, y_ref):
    t = TOKEN_TILE
    i = pl.program_id(0)
    pos = i % tiles_per_seq
    g = g_ref[...]
    x = x_ref[...]
    h = _rms(x, g).astype(_BF16)
    hp = jnp.where(pos == 0, 0.0, _rms(xp_ref[...], g)).astype(_BF16)
    hn = jnp.where(pos == tiles_per_seq - 1, 0.0, _rms(xn_ref[...], g)).astype(_BF16)
    hs_ref[0:HALO, :] = hp
    hs_ref[HALO:HALO + t, :] = h
    hs_ref[HALO + t:, :] = hn

    in_width_a = 2 * A_WIDTH + B_WIDTH
    pa = _dot(h, win_ref[:, :in_width_a])
    pc = _dot(hs_ref[...], win_ref[:, in_width_a:])

    u = _gelu(pa[:, :A_WIDTH])
    v = _gelu(pa[:, A_WIDTH:2 * A_WIDTH])
    mu = jnp.mean(v, axis=-1, keepdims=True)
    vc = v - mu
    vn = vc * lax.rsqrt(jnp.mean(vc * vc, axis=-1, keepdims=True) + EPS)
    vb = (vn * lng_ref[...] + lnb_ref[...]).astype(_BF16)
    for c in range(t // CHUNK):
        rows = slice(c * CHUNK, (c + 1) * CHUNK)
        for grp in range(A_GROUPS):
            cols = slice(grp * A_GROUP_DIM, (grp + 1) * A_GROUP_DIM)
            mixed = _dot(wsp_ref[grp], vb[rows, cols]) + bsp_ref[grp]
            y_ref[rows, cols] = (u[rows, cols] * mixed).astype(_BF16)

    z = pc[:, :B_WIDTH] * pc[:, B_WIDTH:]
    n_rows = t + 2 * HALO
    z_prev = pltpu.roll(z, 1, 0)[HALO:HALO + t]
    z_next = pltpu.roll(z, n_rows - 1, 0)[HALO:HALO + t]
    cw = cw_ref[...]
    conv = z_prev * cw[0:1] + z[HALO:HALO + t] * cw[1:2] + z_next * cw[2:3]
    y_ref[:, A_WIDTH:] = (pa[:, 2 * A_WIDTH:] * conv).astype(_BF16)

    o_ref[...] = x + _dot(y_ref[...], wo_ref[...])


def _even_layer(x2, g, w_in, ln_g, ln_b, w_sp, b_sp, conv_w, w_out, seq_len):
    n, d = x2.shape
    t = TOKEN_TILE
    halo_blocks_per_tile = t // HALO
    n_halo_blocks = n // HALO
    return pl.pallas_call(
        functools.partial(_even_kernel, seq_len // t),
        grid=(n // t,),
        in_specs=[
            pl.BlockSpec((t, d), lambda i: (i, 0)),
            pl.BlockSpec((HALO, d), lambda i: (jnp.maximum(i * halo_blocks_per_tile - 1, 0), 0)),
            pl.BlockSpec((HALO, d),
                         lambda i: (jnp.minimum((i + 1) * halo_blocks_per_tile, n_halo_blocks - 1), 0)),
            _const_spec((1, d)),
            _const_spec(w_in.shape),
            _const_spec((1, A_WIDTH)),
            _const_spec((1, A_WIDTH)),
            _const_spec((A_GROUPS, CHUNK, CHUNK)),
            _const_spec((A_GROUPS, CHUNK, 1)),
            _const_spec((CONV_W, B_WIDTH)),
            _const_spec((d, d)),
        ],
        out_specs=pl.BlockSpec((t, d), lambda i: (i, 0)),
        out_shape=jax.ShapeDtypeStruct((n, d), _F32),
        scratch_shapes=[
            pltpu.VMEM((t + 2 * HALO, d), _BF16),
            pltpu.VMEM((t, d), _BF16),
        ],
        compiler_params=_params(),
        name="even_mixer",
    )(x2, x2, x2, g.reshape(1, d), w_in.astype(_BF16), ln_g.reshape(1, -1), ln_b.reshape(1, -1),
      w_sp.astype(_BF16), b_sp[..., None], conv_w, w_out.astype(_BF16))


def _ffn_kernel(has_final_norm, x_ref, g_ref, wg_ref, wu_ref, wd_ref, *rest):
    if has_final_norm:
        gf_ref, o_ref, h_ref, acc_ref = rest
    else:
        o_ref, h_ref, acc_ref = rest
    x = x_ref[...]
    h_ref[...] = _rms(x, g_ref[...]).astype(_BF16)
    acc_ref[...] = x

    h = h_ref[...]
    for c in range(N_FF_CHUNKS):
        cols = slice(c * FF_CHUNK, (c + 1) * FF_CHUNK)
        gate = _dot(h, wg_ref[:, cols])
        up = _dot(h, wu_ref[:, cols])
        act = (jax.nn.silu(gate) * up).astype(_BF16)
        acc_ref[...] += _dot(act, wd_ref[cols, :])
    out = acc_ref[...]
    if has_final_norm:
        out = _rms(out, gf_ref[...])
    o_ref[...] = out


def _ffn_layer(x2, g, w_gate, w_up, w_down, final_g=None):
    n, d = x2.shape
    t = TOKEN_TILE
    wg = w_gate.astype(_BF16)
    wu = w_up.astype(_BF16)
    wd = w_down.astype(_BF16)
    in_specs = [
        pl.BlockSpec((t, d), lambda i: (i, 0)),
        _const_spec((1, d)),
        _const_spec(wg.shape),
        _const_spec(wu.shape),
        _const_spec(wd.shape),
    ]
    args = [x2, g.reshape(1, d), wg, wu, wd]
    if final_g is not None:
        in_specs.append(_const_spec((1, d)))
        args.append(final_g.reshape(1, d))
    return pl.pallas_call(
        functools.partial(_ffn_kernel, final_g is not None),
        grid=(n // t,),
        in_specs=in_specs,
        out_specs=pl.BlockSpec((t, d), lambda i: (i, 0)),
        out_shape=jax.ShapeDtypeStruct((n, d), _F32),
        scratch_shapes=[pltpu.VMEM((t, d), _BF16), pltpu.VMEM((t, d), _F32)],
        compiler_params=_params(),
        name="swiglu_ffn",
    )(*args)


def _qkv_kernel(x_ref, g_ref, wqv_ref, wk_ref, qt_ref, k_ref, vt_ref):
    h = _rms(x_ref[...], g_ref[...]).astype(_BF16)
    qv = _dot_nt(wqv_ref[...], h)
    q_rows = N_HEADS * HEAD_DIM
    scale = HEAD_DIM ** -0.5
    for j in range(SUB_BLOCKS):
        lanes = slice(j * BLOCK, (j + 1) * BLOCK)
        qt_ref[j] = (qv[:q_rows, lanes] * scale).astype(_BF16)
        vt_ref[j] = qv[q_rows:, lanes].astype(_BF16)
    k_ref[...] = _dot(h, wk_ref[...]).astype(_BF16)


def _qkv_layer(x2, g, w_qkv):
    n, d = x2.shape
    t = TOKEN_TILE
    q_cols = N_HEADS * HEAD_DIM
    kv_cols = N_KV_HEADS * HEAD_DIM
    wq = w_qkv[:, :q_cols]
    wk = w_qkv[:, q_cols:q_cols + kv_cols]
    wv = w_qkv[:, q_cols + kv_cols:]
    wqv_t = jnp.concatenate([wq, wv], axis=1).T.astype(_BF16)
    n_blocks = n // BLOCK
    return pl.pallas_call(
        _qkv_kernel,
        grid=(n // t,),
        in_specs=[
            pl.BlockSpec((t, d), lambda i: (i, 0)),
            _const_spec((1, d)),
            _const_spec(wqv_t.shape),
            _const_spec((d, kv_cols)),
        ],
        out_specs=[
            pl.BlockSpec((SUB_BLOCKS, q_cols, BLOCK), lambda i: (i, 0, 0)),
            pl.BlockSpec((t, kv_cols), lambda i: (i, 0)),
            pl.BlockSpec((SUB_BLOCKS, kv_cols, BLOCK), lambda i: (i, 0, 0)),
        ],
        out_shape=[
            jax.ShapeDtypeStruct((n_blocks, q_cols, BLOCK), _BF16),
            jax.ShapeDtypeStruct((n, kv_cols), _BF16),
            jax.ShapeDtypeStruct((n_blocks, kv_cols, BLOCK), _BF16),
        ],
        compiler_params=_params(),
        name="qkv_proj",
    )(x2, g.reshape(1, d), wqv_t, wk.astype(_BF16))


def _attn_kernel(tiles_per_seq, sink_ref, rb_ref, x_ref, qt_ref, kc_ref, kp_ref, kn_ref,
                 vc_ref, vp_ref, vn_ref, bkt_ref, wo_ref, o_ref,
                 bias_ref, kall_ref, vall_ref, sc_ref, p_ref, at_ref):
    i = pl.program_id(0)
    pos = i % tiles_per_seq

    @pl.when(i == 0)
    def _init():
        bkt = bkt_ref[...]
        rel = (lax.broadcasted_iota(jnp.int32, (KEY_SPAN, BLOCK), 0) - BLOCK
               - lax.broadcasted_iota(jnp.int32, (KEY_SPAN, BLOCK), 1))
        band = jnp.abs(rel) <= WINDOW
        for hd in range(N_HEADS):
            tbl = jnp.zeros((KEY_SPAN, BLOCK), _F32)
            for b in range(N_BUCKETS):
                tbl = jnp.where(bkt == b, rb_ref[b * N_HEADS + hd], tbl)
            bias_ref[hd] = jnp.where(band, tbl, MASK_VALUE)

    kall_ref[0:BLOCK, :] = kp_ref[...]
    kall_ref[BLOCK:BLOCK + TOKEN_TILE, :] = kc_ref[...]
    kall_ref[BLOCK + TOKEN_TILE:, :] = kn_ref[...]
    vall_ref[0] = vp_ref[0]
    for j in range(SUB_BLOCKS):
        vall_ref[1 + j] = vc_ref[j]
    vall_ref[1 + SUB_BLOCKS] = vn_ref[0]

    pen_first = jnp.where(pos == 0, MASK_VALUE, 0.0)
    pen_last = jnp.where(pos == tiles_per_seq - 1, MASK_VALUE, 0.0)

    pair_lanes = 2 * HEAD_DIM
    slabs = [slice(r0, r0 + ROW_SLAB) for r0 in range(0, KEY_SPAN, ROW_SLAB)]
    heads = [(j, kvh) for j in range(SUB_BLOCKS) for kvh in range(N_KV_HEADS)]

    def scores(t):
        j, kvh = heads[t]
        pair, half = kvh // 2, kvh % 2
        q_heads = jnp.concatenate(
            [qt_ref[j, (kvh * Q_PER_KV + grp) * HEAD_DIM:(kvh * Q_PER_KV + grp + 1) * HEAD_DIM, :]
             for grp in range(Q_PER_KV)], axis=1)
        q_zero = jnp.zeros_like(q_heads)
        q2 = jnp.concatenate([q_heads, q_zero] if half == 0 else [q_zero, q_heads], axis=0)
        kwin = kall_ref[j * BLOCK:j * BLOCK + KEY_SPAN, pair * pair_lanes:(pair + 1) * pair_lanes]
        raw = _dot(kwin, q2)
        sc_buf = sc_ref.at[t % 2]
        maxes = []
        for grp in range(Q_PER_KV):
            hd = kvh * Q_PER_KV + grp
            lanes = slice(grp * BLOCK, (grp + 1) * BLOCK)
            mx = None
            for rows in slabs:
                s = raw[rows, lanes] + bias_ref[hd, rows, :]
                if j == 0 and rows.stop <= BLOCK:
                    s = s + pen_first
                if j == SUB_BLOCKS - 1 and rows.start >= 2 * BLOCK:
                    s = s + pen_last
                sc_buf[rows, lanes] = s
                mx = s if mx is None else jnp.maximum(mx, s)
            maxes.append(jnp.maximum(jnp.max(mx, axis=0, keepdims=True), sink_ref[hd]))
        return maxes

    def softmax(t, maxes):
        j, kvh = heads[t]
        sc_buf = sc_ref.at[t % 2]
        p_buf = p_ref.at[t % 2]
        inv_l = []
        for grp in range(Q_PER_KV):
            hd = kvh * Q_PER_KV + grp
            lanes = slice(grp * BLOCK, (grp + 1) * BLOCK)
            m = maxes[grp]
            acc = None
            for rows in slabs:
                p = jnp.exp(sc_buf[rows, lanes] - m)
                acc = p if acc is None else acc + p
                p_buf[rows, lanes] = p.astype(_BF16)
            l = jnp.sum(acc, axis=0, keepdims=True) + jnp.exp(sink_ref[hd] - m)
            inv_l.append(1.0 / l)
        return inv_l

    def weighted_values(t, inv_l):
        j, kvh = heads[t]
        vwin = jnp.concatenate(
            [vall_ref[j + b, kvh * HEAD_DIM:(kvh + 1) * HEAD_DIM, :] for b in range(3)], axis=1)
        ot = _dot(vwin, p_ref[t % 2])
        for grp in range(Q_PER_KV):
            hd = kvh * Q_PER_KV + grp
            at_ref[j, hd * HEAD_DIM:(hd + 1) * HEAD_DIM, :] = (
                ot[:, grp * BLOCK:(grp + 1) * BLOCK] * inv_l[grp]).astype(_BF16)

    maxes = scores(0)
    for t, (j, kvh) in enumerate(heads):
        next_maxes = scores(t + 1) if t + 1 < len(heads) else None
        inv_l = softmax(t, maxes)
        maxes = next_maxes
        weighted_values(t, inv_l)
        if kvh == N_KV_HEADS - 1:
            rows = slice(j * BLOCK, (j + 1) * BLOCK)
            o_ref[rows, :] = x_ref[rows, :] + _dot_tn(at_ref[j], wo_ref[...])


def _t5_buckets(rel):
    nb = N_BUCKETS // 2
    ret = jnp.where(rel > 0, nb, 0)
    n = jnp.abs(rel)
    max_exact = nb // 2
    nf = jnp.maximum(n, 1).astype(jnp.float32)
    large = max_exact + (jnp.log(nf / max_exact) / math.log(MAX_DISTANCE / max_exact)
                         * (nb - max_exact)).astype(jnp.int32)
    large = jnp.minimum(large, nb - 1)
    return ret + jnp.where(n < max_exact, n, large)


def _attn_layer(x2, qt, k, vt, sink, rel_bias, w_out, seq_len):
    n, d = x2.shape
    t = TOKEN_TILE
    n_blocks = n // BLOCK
    kv_cols = N_KV_HEADS * HEAD_DIM
    q_cols = N_HEADS * HEAD_DIM
    kj = jnp.arange(KEY_SPAN, dtype=jnp.int32)[:, None]
    qi = jnp.arange(BLOCK, dtype=jnp.int32)[None, :]
    buckets_t = _t5_buckets(kj - BLOCK - qi).astype(jnp.int32)

    def prev_block(i):
        return jnp.maximum(i * SUB_BLOCKS - 1, 0)

    def next_block(i):
        return jnp.minimum((i + 1) * SUB_BLOCKS, n_blocks - 1)

    smem = pl.BlockSpec(memory_space=pltpu.SMEM)
    return pl.pallas_call(
        functools.partial(_attn_kernel, seq_len // t),
        grid=(n // t,),
        in_specs=[
            smem,
            smem,
            pl.BlockSpec((t, d), lambda i: (i, 0)),
            pl.BlockSpec((SUB_BLOCKS, q_cols, BLOCK), lambda i: (i, 0, 0)),
            pl.BlockSpec((t, kv_cols), lambda i: (i, 0)),
            pl.BlockSpec((BLOCK, kv_cols), lambda i: (prev_block(i), 0)),
            pl.BlockSpec((BLOCK, kv_cols), lambda i: (next_block(i), 0)),
            pl.BlockSpec((SUB_BLOCKS, kv_cols, BLOCK), lambda i: (i, 0, 0)),
            pl.BlockSpec((1, kv_cols, BLOCK), lambda i: (prev_block(i), 0, 0)),
            pl.BlockSpec((1, kv_cols, BLOCK), lambda i: (next_block(i), 0, 0)),
            _const_spec((KEY_SPAN, BLOCK)),
            _const_spec((d, d)),
        ],
        out_specs=pl.BlockSpec((t, d), lambda i: (i, 0)),
        out_shape=jax.ShapeDtypeStruct((n, d), _F32),
        scratch_shapes=[
            pltpu.VMEM((N_HEADS, KEY_SPAN, BLOCK), _F32),
            pltpu.VMEM((TOKEN_TILE + 2 * BLOCK, kv_cols), _BF16),
            pltpu.VMEM((SUB_BLOCKS + 2, kv_cols, BLOCK), _BF16),
            pltpu.VMEM((2, KEY_SPAN, Q_PER_KV * BLOCK), _F32),
            pltpu.VMEM((2, KEY_SPAN, Q_PER_KV * BLOCK), _BF16),
            pltpu.VMEM((SUB_BLOCKS, q_cols, BLOCK), _BF16),
        ],
        compiler_params=_params(),
        name="window_attention",
    )(sink, rel_bias.reshape(-1), x2, qt, k, k, k, vt, vt, vt, buckets_t, w_out.astype(_BF16))


def kernel(x, norm_mix, norm_ffn, even_w_in, even_v_ln_g, even_v_ln_b, even_w_spatial, even_b_spatial,
           even_conv_w, even_w_out, attn_w_qkv, attn_sink, rel_bias, attn_w_out, ffn_w_gate, ffn_w_up,
           ffn_w_down, final_norm):
    bsz, seq_len, d = x.shape
    assert d == D_MODEL and seq_len % TOKEN_TILE == 0
    x2 = x.reshape(bsz * seq_len, d)
    x2 = _even_layer(x2, norm_mix[0], even_w_in[0], even_v_ln_g[0], even_v_ln_b[0], even_w_spatial[0],
                     even_b_spatial[0], even_conv_w[0], even_w_out[0], seq_len)
    x2 = _ffn_layer(x2, norm_ffn[0], ffn_w_gate[0], ffn_w_up[0], ffn_w_down[0])
    qt, k, vt = _qkv_layer(x2, norm_mix[1], attn_w_qkv[0])
    x2 = _attn_layer(x2, qt, k, vt, attn_sink[0], rel_bias, attn_w_out[0], seq_len)
    x2 = _ffn_layer(x2, norm_ffn[1], ffn_w_gate[1], ffn_w_up[1], ffn_w_down[1], final_g=final_norm)
    return x2.reshape(bsz, seq_len, d)
```

```python
import functools
import math

import jax
import jax.numpy as jnp
from jax import lax
from jax.experimental import pallas as pl
from jax.experimental.pallas import tpu as pltpu

D_MODEL = 1024
EPS = 1e-6
A_WIDTH = 512
B_WIDTH = 512
A_GROUPS = 4
A_GROUP_DIM = 128
CHUNK = 128
CONV_W = 3
N_HEADS = 16
N_KV_HEADS = 4
Q_PER_KV = 4
HEAD_DIM = 64
WINDOW = 128
BLOCK = 128
N_BUCKETS = 32
MAX_DISTANCE = 128
D_FF = 2816

TOKEN_TILE = 512
HALO = 16
FF_CHUNK = 256
N_FF_CHUNKS = D_FF // FF_CHUNK
SUB_BLOCKS = TOKEN_TILE // BLOCK
KEY_SPAN = 3 * BLOCK
MASK_VALUE = -1e30
ROW_SLAB = 64
VMEM_LIMIT_BYTES = 56 * 1024 * 1024

_BF16 = jnp.bfloat16
_F32 = jnp.float32


def _dot(a, b):
    return jnp.dot(a, b, preferred_element_type=_F32)


def _dot_nt(a, b):
    return lax.dot_general(a, b, (((1,), (1,)), ((), ())), preferred_element_type=_F32)


def _dot_tn(a, b):
    return lax.dot_general(a, b, (((0,), (0,)), ((), ())), preferred_element_type=_F32)


def _gelu(x):
    return 0.5 * x * (1.0 + lax.erf(x * math.sqrt(0.5)))


def _rms(x, g):
    return (x * lax.rsqrt(jnp.mean(x * x, axis=-1, keepdims=True) + EPS)) * g


def _const_spec(shape):
    zeros = (0,) * len(shape)
    return pl.BlockSpec(shape, lambda i: zeros, pipeline_mode=pl.Buffered(1))


def _params():
    return pltpu.CompilerParams(
        dimension_semantics=("arbitrary",), vmem_limit_bytes=VMEM_LIMIT_BYTES)


def _even_kernel(tiles_per_seq, x_ref, xp_ref, xn_ref, g_ref, win_ref, lng_ref, lnb_ref,
                 wsp_ref, bsp_ref, cw_ref, wo_ref, o_ref, hs_ref, y_ref):
    t = TOKEN_TILE
    i = pl.program_id(0)
    pos = i % tiles_per_seq
    g = g_ref[...]
    x = x_ref[...]
    h = _rms(x, g).astype(_BF16)
    hp = jnp.where(pos == 0, 0.0, _rms(xp_ref[...], g)).astype(_BF16)
    hn = jnp.where(pos == tiles_per_seq - 1, 0.0, _rms(xn_ref[...], g)).astype(_BF16)
    hs_ref[0:HALO, :] = hp
    hs_ref[HALO:HALO + t, :] = h
    hs_ref[HALO + t:, :] = hn

    in_width_a = 2 * A_WIDTH + B_WIDTH
    pa = _dot(h, win_ref[:, :in_width_a])
    pc = _dot(hs_ref[...], win_ref[:, in_width_a:])

    u = _gelu(pa[:, :A_WIDTH])
    v = _gelu(pa[:, A_WIDTH:2 * A_WIDTH])
    mu = jnp.mean(v, axis=-1, keepdims=True)
    vc = v - mu
    vn = vc * lax.rsqrt(jnp.mean(vc * vc, axis=-1, keepdims=True) + EPS)
    vb = (vn * lng_ref[...] + lnb_ref[...]).astype(_BF16)
    for c in range(t // CHUNK):
        rows = slice(c * CHUNK, (c + 1) * CHUNK)
        for grp in range(A_GROUPS):
            cols = slice(grp * A_GROUP_DIM, (grp + 1) * A_GROUP_DIM)
            mixed = _dot(wsp_ref[grp], vb[rows, cols]) + bsp_ref[grp]
            y_ref[rows, cols] = (u[rows, cols] * mixed).astype(_BF16)

    z = pc[:, :B_WIDTH] * pc[:, B_WIDTH:]
    n_rows = t + 2 * HALO
    z_prev = pltpu.roll(z, 1, 0)[HALO:HALO + t]
    z_next = pltpu.roll(z, n_rows - 1, 0)[HALO:HALO + t]
    cw = cw_ref[...]
    conv = z_prev * cw[0:1] + z[HALO:HALO + t] * cw[1:2] + z_next * cw[2:3]
    y_ref[:, A_WIDTH:] = (pa[:, 2 * A_WIDTH:] * conv).astype(_BF16)

    o_ref[...] = x + _dot(y_ref[...], wo_ref[...])


def _even_layer(x2, g, w_in, ln_g, ln_b, w_sp, b_sp, conv_w, w_out, seq_len):
    n, d = x2.shape
    t = TOKEN_TILE
    halo_blocks_per_tile = t // HALO
    n_halo_blocks = n // HALO
    return pl.pallas_call(
        functools.partial(_even_kernel, seq_len // t),
        grid=(n // t,),
        in_specs=[
            pl.BlockSpec((t, d), lambda i: (i, 0)),
            pl.BlockSpec((HALO, d), lambda i: (jnp.maximum(i * halo_blocks_per_tile - 1, 0), 0)),
            pl.BlockSpec((HALO, d),
                         lambda i: (jnp.minimum((i + 1) * halo_blocks_per_tile, n_halo_blocks - 1), 0)),
            _const_spec((1, d)),
            _const_spec(w_in.shape),
            _const_spec((1, A_WIDTH)),
            _const_spec((1, A_WIDTH)),
            _const_spec((A_GROUPS, CHUNK, CHUNK)),
            _const_spec((A_GROUPS, CHUNK, 1)),
            _const_spec((CONV_W, B_WIDTH)),
            _const_spec((d, d)),
        ],
        out_specs=pl.BlockSpec((t, d), lambda i: (i, 0)),
        out_shape=jax.ShapeDtypeStruct((n, d), _F32),
        scratch_shapes=[
            pltpu.VMEM((t + 2 * HALO, d), _BF16),
            pltpu.VMEM((t, d), _BF16),
        ],
        compiler_params=_params(),
        name="even_mixer",
    )(x2, x2, x2, g.reshape(1, d), w_in.astype(_BF16), ln_g.reshape(1, -1), ln_b.reshape(1, -1),
      w_sp.astype(_BF16), b_sp[..., None], conv_w, w_out.astype(_BF16))


def _ffn_kernel(has_final_norm, x_ref, g_ref, wg_ref, wu_ref, wd_ref, *rest):
    if has_final_norm:
        gf_ref, o_ref, h_ref, acc_ref = rest
    else:
        o_ref, h_ref, acc_ref = rest
    x = x_ref[...]
    h_ref[...] = _rms(x, g_ref[...]).astype(_BF16)
    acc_ref[...] = x

    h = h_ref[...]
    for c in range(N_FF_CHUNKS):
        cols = slice(c * FF_CHUNK, (c + 1) * FF_CHUNK)
        gate = _dot(h, wg_ref[:, cols])
        up = _dot(h, wu_ref[:, cols])
        act = (jax.nn.silu(gate) * up).astype(_BF16)
        acc_ref[...] += _dot(act, wd_ref[cols, :])
    out = acc_ref[...]
    if has_final_norm:
        out = _rms(out, gf_ref[...])
    o_ref[...] = out


def _layer_spec(stacked_shape, layer):
    zeros = (0,) * (len(stacked_shape) - 1)
    return pl.BlockSpec((None,) + tuple(stacked_shape[1:]), lambda i: (layer,) + zeros,
                        pipeline_mode=pl.Buffered(1))


def _ffn_layer(x2, g, wg, wu, wd, layer, final_g=None):
    n, d = x2.shape
    t = TOKEN_TILE
    in_specs = [
        pl.BlockSpec((t, d), lambda i: (i, 0)),
        _const_spec((1, d)),
        _layer_spec(wg.shape, layer),
        _layer_spec(wu.shape, layer),
        _layer_spec(wd.shape, layer),
    ]
    args = [x2, g.reshape(1, d), wg, wu, wd]
    if final_g is not None:
        in_specs.append(_const_spec((1, d)))
        args.append(final_g.reshape(1, d))
    return pl.pallas_call(
        functools.partial(_ffn_kernel, final_g is not None),
        grid=(n // t,),
        in_specs=in_specs,
        out_specs=pl.BlockSpec((t, d), lambda i: (i, 0)),
        out_shape=jax.ShapeDtypeStruct((n, d), _F32),
        scratch_shapes=[pltpu.VMEM((t, d), _BF16), pltpu.VMEM((t, d), _F32)],
        compiler_params=_params(),
        name="swiglu_ffn",
    )(*args)


def _qkv_kernel(x_ref, g_ref, wqv_ref, wk_ref, qt_ref, k_ref, vt_ref):
    h = _rms(x_ref[...], g_ref[...]).astype(_BF16)
    qv = _dot_nt(wqv_ref[...], h)
    q_rows = N_HEADS * HEAD_DIM
    scale = HEAD_DIM ** -0.5
    for j in range(SUB_BLOCKS):
        lanes = slice(j * BLOCK, (j + 1) * BLOCK)
        qt_ref[j] = (qv[:q_rows, lanes] * scale).astype(_BF16)
        vt_ref[j] = qv[q_rows:, lanes].astype(_BF16)
    k_ref[...] = _dot(h, wk_ref[...]).astype(_BF16)


def _qkv_layer(x2, g, w_qkv):
    n, d = x2.shape
    t = TOKEN_TILE
    q_cols = N_HEADS * HEAD_DIM
    kv_cols = N_KV_HEADS * HEAD_DIM
    wq = w_qkv[:, :q_cols]
    wk = w_qkv[:, q_cols:q_cols + kv_cols]
    wv = w_qkv[:, q_cols + kv_cols:]
    wqv_t = jnp.concatenate([wq, wv], axis=1).T.astype(_BF16)
    n_blocks = n // BLOCK
    return pl.pallas_call(
        _qkv_kernel,
        grid=(n // t,),
        in_specs=[
            pl.BlockSpec((t, d), lambda i: (i, 0)),
            _const_spec((1, d)),
            _const_spec(wqv_t.shape),
            _const_spec((d, kv_cols)),
        ],
        out_specs=[
            pl.BlockSpec((SUB_BLOCKS, q_cols, BLOCK), lambda i: (i, 0, 0)),
            pl.BlockSpec((t, kv_cols), lambda i: (i, 0)),
            pl.BlockSpec((SUB_BLOCKS, kv_cols, BLOCK), lambda i: (i, 0, 0)),
        ],
        out_shape=[
            jax.ShapeDtypeStruct((n_blocks, q_cols, BLOCK), _BF16),
            jax.ShapeDtypeStruct((n, kv_cols), _BF16),
            jax.ShapeDtypeStruct((n_blocks, kv_cols, BLOCK), _BF16),
        ],
        compiler_params=_params(),
        name="qkv_proj",
    )(x2, g.reshape(1, d), wqv_t, wk.astype(_BF16))


def _attn_kernel(tiles_per_seq, sink_ref, rb_ref, x_ref, qt_ref, kc_ref, kp_ref, kn_ref,
                 vc_ref, vp_ref, vn_ref, bkt_ref, wo_ref, o_ref,
                 bias_ref, kall_ref, vall_ref, sc_ref, p_ref, at_ref):
    i = pl.program_id(0)
    pos = i % tiles_per_seq

    @pl.when(i == 0)
    def _init():
        bkt = bkt_ref[...]
        rel = (lax.broadcasted_iota(jnp.int32, (KEY_SPAN, BLOCK), 0) - BLOCK
               - lax.broadcasted_iota(jnp.int32, (KEY_SPAN, BLOCK), 1))
        band = jnp.abs(rel) <= WINDOW
        for hd in range(N_HEADS):
            tbl = jnp.zeros((KEY_SPAN, BLOCK), _F32)
            for b in range(N_BUCKETS):
                tbl = jnp.where(bkt == b, rb_ref[b * N_HEADS + hd], tbl)
            bias_ref[hd] = jnp.where(band, tbl, MASK_VALUE)

    kall_ref[0:BLOCK, :] = kp_ref[...]
    kall_ref[BLOCK:BLOCK + TOKEN_TILE, :] = kc_ref[...]
    kall_ref[BLOCK + TOKEN_TILE:, :] = kn_ref[...]
    vall_ref[0] = vp_ref[0]
    for j in range(SUB_BLOCKS):
        vall_ref[1 + j] = vc_ref[j]
    vall_ref[1 + SUB_BLOCKS] = vn_ref[0]

    pen_first = jnp.where(pos == 0, MASK_VALUE, 0.0)
    pen_last = jnp.where(pos == tiles_per_seq - 1, MASK_VALUE, 0.0)

    pair_lanes = 2 * HEAD_DIM
    slabs = [slice(r0, r0 + ROW_SLAB) for r0 in range(0, KEY_SPAN, ROW_SLAB)]
    heads = [(j, kvh) for j in range(SUB_BLOCKS) for kvh in range(N_KV_HEADS)]

    def scores(t):
        j, kvh = heads[t]
        pair, half = kvh // 2, kvh % 2
        q_heads = jnp.concatenate(
            [qt_ref[j, (kvh * Q_PER_KV + grp) * HEAD_DIM:(kvh * Q_PER_KV + grp + 1) * HEAD_DIM, :]
             for grp in range(Q_PER_KV)], axis=1)
        q_zero = jnp.zeros_like(q_heads)
        q2 = jnp.concatenate([q_heads, q_zero] if half == 0 else [q_zero, q_heads], axis=0)
        kwin = kall_ref[j * BLOCK:j * BLOCK + KEY_SPAN, pair * pair_lanes:(pair + 1) * pair_lanes]
        raw = _dot(kwin, q2)
        sc_buf = sc_ref.at[t % 2]
        maxes = []
        for grp in range(Q_PER_KV):
            hd = kvh * Q_PER_KV + grp
            lanes = slice(grp * BLOCK, (grp + 1) * BLOCK)
            mx = None
            for rows in slabs:
                s = raw[rows, lanes] + bias_ref[hd, rows, :]
                if j == 0 and rows.stop <= BLOCK:
                    s = s + pen_first
                if j == SUB_BLOCKS - 1 and rows.start >= 2 * BLOCK:
                    s = s + pen_last
                sc_buf[rows, lanes] = s
                mx = s if mx is None else jnp.maximum(mx, s)
            maxes.append(jnp.maximum(jnp.max(mx, axis=0, keepdims=True), sink_ref[hd]))
        return maxes

    def softmax(t, maxes):
        j, kvh = heads[t]
        sc_buf = sc_ref.at[t % 2]
        p_buf = p_ref.at[t % 2]
        inv_l = []
        for grp in range(Q_PER_KV):
            hd = kvh * Q_PER_KV + grp
            lanes = slice(grp * BLOCK, (grp + 1) * BLOCK)
            m = maxes[grp]
            acc = None
            for rows in slabs:
                p = jnp.exp(sc_buf[rows, lanes] - m)
                acc = p if acc is None else acc + p
                p_buf[rows, lanes] = p.astype(_BF16)
            l = jnp.sum(acc, axis=0, keepdims=True) + jnp.exp(sink_ref[hd] - m)
            inv_l.append(1.0 / l)
        return inv_l

    def weighted_values(t, inv_l):
        j, kvh = heads[t]
        vwin = jnp.concatenate(
            [vall_ref[j + b, kvh * HEAD_DIM:(kvh + 1) * HEAD_DIM, :] for b in range(3)], axis=1)
        ot = _dot(vwin, p_ref[t % 2])
        for grp in range(Q_PER_KV):
            hd = kvh * Q_PER_KV + grp
            at_ref[j, hd * HEAD_DIM:(hd + 1) * HEAD_DIM, :] = (
                ot[:, grp * BLOCK:(grp + 1) * BLOCK] * inv_l[grp]).astype(_BF16)

    maxes = scores(0)
    for t, (j, kvh) in enumerate(heads):
        next_maxes = scores(t + 1) if t + 1 < len(heads) else None
        inv_l = softmax(t, maxes)
        maxes = next_maxes
        weighted_values(t, inv_l)
        if kvh == N_KV_HEADS - 1:
            rows = slice(j * BLOCK, (j + 1) * BLOCK)
            o_ref[rows, :] = x_ref[rows, :] + _dot_tn(at_ref[j], wo_ref[...])


def _t5_buckets(rel):
    nb = N_BUCKETS // 2
    ret = jnp.where(rel > 0, nb, 0)
    n = jnp.abs(rel)
    max_exact = nb // 2
    nf = jnp.maximum(n, 1).astype(jnp.float32)
    large = max_exact + (jnp.log(nf / max_exact) / math.log(MAX_DISTANCE / max_exact)
                         * (nb - max_exact)).astype(jnp.int32)
    large = jnp.minimum(large, nb - 1)
    return ret + jnp.where(n < max_exact, n, large)


def _attn_layer(x2, qt, k, vt, sink, rel_bias, w_out, seq_len):
    n, d = x2.shape
    t = TOKEN_TILE
    n_blocks = n // BLOCK
    kv_cols = N_KV_HEADS * HEAD_DIM
    q_cols = N_HEADS * HEAD_DIM
    kj = jnp.arange(KEY_SPAN, dtype=jnp.int32)[:, None]
    qi = jnp.arange(BLOCK, dtype=jnp.int32)[None, :]
    buckets_t = _t5_buckets(kj - BLOCK - qi).astype(jnp.int32)

    def prev_block(i):
        return jnp.maximum(i * SUB_BLOCKS - 1, 0)

    def next_block(i):
        return jnp.minimum((i + 1) * SUB_BLOCKS, n_blocks - 1)

    smem = pl.BlockSpec(memory_space=pltpu.SMEM)
    return pl.pallas_call(
        functools.partial(_attn_kernel, seq_len // t),
        grid=(n // t,),
        in_specs=[
            smem,
            smem,
            pl.BlockSpec((t, d), lambda i: (i, 0)),
            pl.BlockSpec((SUB_BLOCKS, q_cols, BLOCK), lambda i: (i, 0, 0)),
            pl.BlockSpec((t, kv_cols), lambda i: (i, 0)),
            pl.BlockSpec((BLOCK, kv_cols), lambda i: (prev_block(i), 0)),
            pl.BlockSpec((BLOCK, kv_cols), lambda i: (next_block(i), 0)),
            pl.BlockSpec((SUB_BLOCKS, kv_cols, BLOCK), lambda i: (i, 0, 0)),
            pl.BlockSpec((1, kv_cols, BLOCK), lambda i: (prev_block(i), 0, 0)),
            pl.BlockSpec((1, kv_cols, BLOCK), lambda i: (next_block(i), 0, 0)),
            _const_spec((KEY_SPAN, BLOCK)),
            _const_spec((d, d)),
        ],
        out_specs=pl.BlockSpec((t, d), lambda i: (i, 0)),
        out_shape=jax.ShapeDtypeStruct((n, d), _F32),
        scratch_shapes=[
            pltpu.VMEM((N_HEADS, KEY_SPAN, BLOCK), _F32),
            pltpu.VMEM((TOKEN_TILE + 2 * BLOCK, kv_cols), _BF16),
            pltpu.VMEM((SUB_BLOCKS + 2, kv_cols, BLOCK), _BF16),
            pltpu.VMEM((2, KEY_SPAN, Q_PER_KV * BLOCK), _F32),
            pltpu.VMEM((2, KEY_SPAN, Q_PER_KV * BLOCK), _BF16),
            pltpu.VMEM((SUB_BLOCKS, q_cols, BLOCK), _BF16),
        ],
        compiler_params=_params(),
        name="window_attention",
    )(sink, rel_bias.reshape(-1), x2, qt, k, k, k, vt, vt, vt, buckets_t, w_out.astype(_BF16))


def kernel(x, norm_mix, norm_ffn, even_w_in, even_v_ln_g, even_v_ln_b, even_w_spatial, even_b_spatial,
           even_conv_w, even_w_out, attn_w_qkv, attn_sink, rel_bias, attn_w_out, ffn_w_gate, ffn_w_up,
           ffn_w_down, final_norm):
    bsz, seq_len, d = x.shape
    assert d == D_MODEL and seq_len % TOKEN_TILE == 0
    x2 = x.reshape(bsz * seq_len, d)
    wg, wu, wd = ffn_w_gate.astype(_BF16), ffn_w_up.astype(_BF16), ffn_w_down.astype(_BF16)
    x2 = _even_layer(x2, norm_mix[0], even_w_in[0], even_v_ln_g[0], even_v_ln_b[0], even_w_spatial[0],
                     even_b_spatial[0], even_conv_w[0], even_w_out[0], seq_len)
    x2 = _ffn_layer(x2, norm_ffn[0], wg, wu, wd, 0)
    qt, k, vt = _qkv_layer(x2, norm_mix[1], attn_w_qkv[0])
    x2 = _attn_layer(x2, qt, k, vt, attn_sink[0], rel_bias, attn_w_out[0], seq_len)
    x2 = _ffn_layer(x2, norm_ffn[1], wg, wu, wd, 1, final_g=final_norm)
    return x2.reshape(bsz, seq_len, d)
```

```python
import functools
import math

import jax
import jax.numpy as jnp
from jax import lax
from jax.experimental import pallas as pl
from jax.experimental.pallas import tpu as pltpu

D_MODEL = 1024
EPS = 1e-6
A_WIDTH = 512
B_WIDTH = 512
A_GROUPS = 4
A_GROUP_DIM = 128
CHUNK = 128
CONV_W = 3
N_HEADS = 16
N_KV_HEADS = 4
Q_PER_KV = 4
HEAD_DIM = 64
WINDOW = 128
BLOCK = 128
N_BUCKETS = 32
MAX_DISTANCE = 128
D_FF = 2816

TOKEN_TILE = 512
HALO = 16
FF_CHUNK = 256
N_FF_CHUNKS = D_FF // FF_CHUNK
SUB_BLOCKS = TOKEN_TILE // BLOCK
KEY_SPAN = 3 * BLOCK
MASK_VALUE = -1e30
ROW_SLAB = 64
VMEM_LIMIT_BYTES = 56 * 1024 * 1024

_BF16 = jnp.bfloat16
_F32 = jnp.float32


def _dot(a, b):
    return jnp.dot(a, b, preferred_element_type=_F32)


def _dot_nt(a, b):
    return lax.dot_general(a, b, (((1,), (1,)), ((), ())), preferred_element_type=_F32)


def _dot_tn(a, b):
    return lax.dot_general(a, b, (((0,), (0,)), ((), ())), preferred_element_type=_F32)


def _gelu(x):
    return 0.5 * x * (1.0 + lax.erf(x * math.sqrt(0.5)))


def _rms(x, g):
    return (x * lax.rsqrt(jnp.mean(x * x, axis=-1, keepdims=True) + EPS)) * g


def _const_spec(shape):
    zeros = (0,) * len(shape)
    return pl.BlockSpec(shape, lambda i: zeros, pipeline_mode=pl.Buffered(1))


def _params():
    return pltpu.CompilerParams(
        dimension_semantics=("arbitrary",), vmem_limit_bytes=VMEM_LIMIT_BYTES)


def _even_kernel(tiles_per_seq, x_ref, xp_ref, xn_ref, g_ref, win_ref, lng_ref, lnb_ref,
                 wsp_ref, bsp_ref, cw_ref, wo_ref, o_ref, hs_ref, y_ref):
    t = TOKEN_TILE
    i = pl.program_id(0)
    pos = i % tiles_per_seq
    g = g_ref[...]
    x = x_ref[...]
    h = _rms(x, g).astype(_BF16)
    hp = jnp.where(pos == 0, 0.0, _rms(xp_ref[...], g)).astype(_BF16)
    hn = jnp.where(pos == tiles_per_seq - 1, 0.0, _rms(xn_ref[...], g)).astype(_BF16)
    hs_ref[0:HALO, :] = hp
    hs_ref[HALO:HALO + t, :] = h
    hs_ref[HALO + t:, :] = hn

    in_width_a = 2 * A_WIDTH + B_WIDTH
    pa = _dot(h, win_ref[:, :in_width_a])
    pc = _dot(hs_ref[...], win_ref[:, in_width_a:])

    u = _gelu(pa[:, :A_WIDTH])
    v = _gelu(pa[:, A_WIDTH:2 * A_WIDTH])
    mu = jnp.mean(v, axis=-1, keepdims=True)
    vc = v - mu
    vn = vc * lax.rsqrt(jnp.mean(vc * vc, axis=-1, keepdims=True) + EPS)
    vb = (vn * lng_ref[...] + lnb_ref[...]).astype(_BF16)
    for c in range(t // CHUNK):
        rows = slice(c * CHUNK, (c + 1) * CHUNK)
        for grp in range(A_GROUPS):
            cols = slice(grp * A_GROUP_DIM, (grp + 1) * A_GROUP_DIM)
            mixed = _dot(wsp_ref[grp], vb[rows, cols]) + bsp_ref[grp]
            y_ref[rows, cols] = (u[rows, cols] * mixed).astype(_BF16)

    z = pc[:, :B_WIDTH] * pc[:, B_WIDTH:]
    n_rows = t + 2 * HALO
    z_prev = pltpu.roll(z, 1, 0)[HALO:HALO + t]
    z_next = pltpu.roll(z, n_rows - 1, 0)[HALO:HALO + t]
    cw = cw_ref[...]
    conv = z_prev * cw[0:1] + z[HALO:HALO + t] * cw[1:2] + z_next * cw[2:3]
    y_ref[:, A_WIDTH:] = (pa[:, 2 * A_WIDTH:] * conv).astype(_BF16)

    o_ref[...] = x + _dot(y_ref[...], wo_ref[...])


def _even_layer(x2, g, w_in, ln_g, ln_b, w_sp, b_sp, conv_w, w_out, seq_len):
    n, d = x2.shape
    t = TOKEN_TILE
    halo_blocks_per_tile = t // HALO
    n_halo_blocks = n // HALO
    return pl.pallas_call(
        functools.partial(_even_kernel, seq_len // t),
        grid=(n // t,),
        in_specs=[
            pl.BlockSpec((t, d), lambda i: (i, 0)),
            pl.BlockSpec((HALO, d), lambda i: (jnp.maximum(i * halo_blocks_per_tile - 1, 0), 0)),
            pl.BlockSpec((HALO, d),
                         lambda i: (jnp.minimum((i + 1) * halo_blocks_per_tile, n_halo_blocks - 1), 0)),
            _const_spec((1, d)),
            _const_spec(w_in.shape),
            _const_spec((1, A_WIDTH)),
            _const_spec((1, A_WIDTH)),
            _const_spec((A_GROUPS, CHUNK, CHUNK)),
            _const_spec((A_GROUPS, CHUNK, 1)),
            _const_spec((CONV_W, B_WIDTH)),
            _const_spec((d, d)),
        ],
        out_specs=pl.BlockSpec((t, d), lambda i: (i, 0)),
        out_shape=jax.ShapeDtypeStruct((n, d), _F32),
        scratch_shapes=[
            pltpu.VMEM((t + 2 * HALO, d), _BF16),
            pltpu.VMEM((t, d), _BF16),
        ],
        compiler_params=_params(),
        name="even_mixer",
    )(x2, x2, x2, g.reshape(1, d), w_in.astype(_BF16), ln_g.reshape(1, -1), ln_b.reshape(1, -1),
      w_sp.astype(_BF16), b_sp[..., None], conv_w, w_out.astype(_BF16))


def _ffn_kernel(has_final_norm, x_ref, xnext_ref, g_ref, wg_ref, wu_ref, wd_ref, *rest):
    if has_final_norm:
        gf_ref, o_ref, h_ref, acc_ref = rest
    else:
        o_ref, h_ref, acc_ref = rest
    i = pl.program_id(0)
    slot = i % 2

    @pl.when(i == 0)
    def _first_tile():
        h_ref[0] = _rms(x_ref[...], g_ref[...]).astype(_BF16)

    acc_ref[...] = x_ref[...]

    h = h_ref[slot]
    for c in range(N_FF_CHUNKS):
        cols = slice(c * FF_CHUNK, (c + 1) * FF_CHUNK)
        gate = _dot(h, wg_ref[:, cols])
        up = _dot(h, wu_ref[:, cols])
        act = (jax.nn.silu(gate) * up).astype(_BF16)
        acc_ref[...] += _dot(act, wd_ref[cols, :])
        if c == N_FF_CHUNKS // 2:
            h_ref[1 - slot] = _rms(xnext_ref[...], g_ref[...]).astype(_BF16)
    out = acc_ref[...]
    if has_final_norm:
        out = _rms(out, gf_ref[...])
    o_ref[...] = out


def _layer_spec(stacked_shape, layer):
    zeros = (0,) * (len(stacked_shape) - 1)
    return pl.BlockSpec((None,) + tuple(stacked_shape[1:]), lambda i: (layer,) + zeros,
                        pipeline_mode=pl.Buffered(1))


def _ffn_layer(x2, g, wg, wu, wd, layer, final_g=None):
    n, d = x2.shape
    t = TOKEN_TILE
    last = n // t - 1
    in_specs = [
        pl.BlockSpec((t, d), lambda i: (i, 0)),
        pl.BlockSpec((t, d), lambda i: (jnp.minimum(i + 1, last), 0)),
        _const_spec((1, d)),
        _layer_spec(wg.shape, layer),
        _layer_spec(wu.shape, layer),
        _layer_spec(wd.shape, layer),
    ]
    args = [x2, x2, g.reshape(1, d), wg, wu, wd]
    if final_g is not None:
        in_specs.append(_const_spec((1, d)))
        args.append(final_g.reshape(1, d))
    return pl.pallas_call(
        functools.partial(_ffn_kernel, final_g is not None),
        grid=(n // t,),
        in_specs=in_specs,
        out_specs=pl.BlockSpec((t, d), lambda i: (i, 0)),
        out_shape=jax.ShapeDtypeStruct((n, d), _F32),
        scratch_shapes=[pltpu.VMEM((2, t, d), _BF16), pltpu.VMEM((t, d), _F32)],
        compiler_params=_params(),
        name="swiglu_ffn",
    )(*args)


def _qkv_kernel(x_ref, g_ref, wqv_ref, wk_ref, qt_ref, k_ref, vt_ref):
    h = _rms(x_ref[...], g_ref[...]).astype(_BF16)
    qv = _dot_nt(wqv_ref[...], h)
    q_rows = N_HEADS * HEAD_DIM
    scale = HEAD_DIM ** -0.5
    for j in range(SUB_BLOCKS):
        lanes = slice(j * BLOCK, (j + 1) * BLOCK)
        qt_ref[j] = (qv[:q_rows, lanes] * scale).astype(_BF16)
        vt_ref[j] = qv[q_rows:, lanes].astype(_BF16)
    k_ref[...] = _dot(h, wk_ref[...]).astype(_BF16)


def _qkv_layer(x2, g, w_qkv):
    n, d = x2.shape
    t = TOKEN_TILE
    q_cols = N_HEADS * HEAD_DIM
    kv_cols = N_KV_HEADS * HEAD_DIM
    wq = w_qkv[:, :q_cols]
    wk = w_qkv[:, q_cols:q_cols + kv_cols]
    wv = w_qkv[:, q_cols + kv_cols:]
    wqv_t = jnp.concatenate([wq, wv], axis=1).T.astype(_BF16)
    n_blocks = n // BLOCK
    return pl.pallas_call(
        _qkv_kernel,
        grid=(n // t,),
        in_specs=[
            pl.BlockSpec((t, d), lambda i: (i, 0)),
            _const_spec((1, d)),
            _const_spec(wqv_t.shape),
            _const_spec((d, kv_cols)),
        ],
        out_specs=[
            pl.BlockSpec((SUB_BLOCKS, q_cols, BLOCK), lambda i: (i, 0, 0)),
            pl.BlockSpec((t, kv_cols), lambda i: (i, 0)),
            pl.BlockSpec((SUB_BLOCKS, kv_cols, BLOCK), lambda i: (i, 0, 0)),
        ],
        out_shape=[
            jax.ShapeDtypeStruct((n_blocks, q_cols, BLOCK), _BF16),
            jax.ShapeDtypeStruct((n, kv_cols), _BF16),
            jax.ShapeDtypeStruct((n_blocks, kv_cols, BLOCK), _BF16),
        ],
        compiler_params=_params(),
        name="qkv_proj",
    )(x2, g.reshape(1, d), wqv_t, wk.astype(_BF16))


def _attn_kernel(tiles_per_seq, sink_ref, rb_ref, x_ref, qt_ref, kc_ref, kp_ref, kn_ref,
                 vc_ref, vp_ref, vn_ref, bkt_ref, wo_ref, o_ref,
                 bias_ref, kall_ref, vall_ref, sc_ref, p_ref, at_ref):
    i = pl.program_id(0)
    pos = i % tiles_per_seq

    @pl.when(i == 0)
    def _init():
        bkt = bkt_ref[...]
        rel = (lax.broadcasted_iota(jnp.int32, (KEY_SPAN, BLOCK), 0) - BLOCK
               - lax.broadcasted_iota(jnp.int32, (KEY_SPAN, BLOCK), 1))
        band = jnp.abs(rel) <= WINDOW

        def fill_head(hd, carry):
            tbl = jnp.zeros((KEY_SPAN, BLOCK), _F32)
            for b in range(N_BUCKETS):
                tbl = jnp.where(bkt == b, rb_ref[b * N_HEADS + hd], tbl)
            bias_ref[hd] = jnp.where(band, tbl, MASK_VALUE)
            return carry

        lax.fori_loop(0, N_HEADS, fill_head, 0)

    kall_ref[0:BLOCK, :] = kp_ref[...]
    kall_ref[BLOCK:BLOCK + TOKEN_TILE, :] = kc_ref[...]
    kall_ref[BLOCK + TOKEN_TILE:, :] = kn_ref[...]
    vall_ref[0] = vp_ref[0]
    for j in range(SUB_BLOCKS):
        vall_ref[1 + j] = vc_ref[j]
    vall_ref[1 + SUB_BLOCKS] = vn_ref[0]

    pen_first = jnp.where(pos == 0, MASK_VALUE, 0.0)
    pen_last = jnp.where(pos == tiles_per_seq - 1, MASK_VALUE, 0.0)

    pair_lanes = 2 * HEAD_DIM
    slabs = [slice(r0, r0 + ROW_SLAB) for r0 in range(0, KEY_SPAN, ROW_SLAB)]
    heads = [(j, kvh) for j in range(SUB_BLOCKS) for kvh in range(N_KV_HEADS)]

    def scores(t):
        j, kvh = heads[t]
        pair, half = kvh // 2, kvh % 2
        q_heads = jnp.concatenate(
            [qt_ref[j, (kvh * Q_PER_KV + grp) * HEAD_DIM:(kvh * Q_PER_KV + grp + 1) * HEAD_DIM, :]
             for grp in range(Q_PER_KV)], axis=1)
        q_zero = jnp.zeros_like(q_heads)
        q2 = jnp.concatenate([q_heads, q_zero] if half == 0 else [q_zero, q_heads], axis=0)
        kwin = kall_ref[j * BLOCK:j * BLOCK + KEY_SPAN, pair * pair_lanes:(pair + 1) * pair_lanes]
        raw = _dot(kwin, q2)
        sc_buf = sc_ref.at[t % 2]
        maxes = []
        for grp in range(Q_PER_KV):
            hd = kvh * Q_PER_KV + grp
            lanes = slice(grp * BLOCK, (grp + 1) * BLOCK)
            mx = None
            for rows in slabs:
                s = raw[rows, lanes] + bias_ref[hd, rows, :]
                if j == 0 and rows.stop <= BLOCK:
                    s = s + pen_first
                if j == SUB_BLOCKS - 1 and rows.start >= 2 * BLOCK:
                    s = s + pen_last
                sc_buf[rows, lanes] = s
                mx = s if mx is None else jnp.maximum(mx, s)
            maxes.append(jnp.maximum(jnp.max(mx, axis=0, keepdims=True), sink_ref[hd]))
        return maxes

    def softmax(t, maxes):
        j, kvh = heads[t]
        sc_buf = sc_ref.at[t % 2]
        p_buf = p_ref.at[t % 2]
        inv_l = []
        for grp in range(Q_PER_KV):
            hd = kvh * Q_PER_KV + grp
            lanes = slice(grp * BLOCK, (grp + 1) * BLOCK)
            m = maxes[grp]
            acc = None
            for rows in slabs:
                p = jnp.exp(sc_buf[rows, lanes] - m)
                acc = p if acc is None else acc + p
                p_buf[rows, lanes] = p.astype(_BF16)
            l = jnp.sum(acc, axis=0, keepdims=True) + jnp.exp(sink_ref[hd] - m)
            inv_l.append(1.0 / l)
        return inv_l

    def weighted_values(t, inv_l):
        j, kvh = heads[t]
        vwin = jnp.concatenate(
            [vall_ref[j + b, kvh * HEAD_DIM:(kvh + 1) * HEAD_DIM, :] for b in range(3)], axis=1)
        ot = _dot(vwin, p_ref[t % 2])
        for grp in range(Q_PER_KV):
            hd = kvh * Q_PER_KV + grp
            at_ref[j, hd * HEAD_DIM:(hd + 1) * HEAD_DIM, :] = (
                ot[:, grp * BLOCK:(grp + 1) * BLOCK] * inv_l[grp]).astype(_BF16)

    maxes = scores(0)
    for t, (j, kvh) in enumerate(heads):
        next_maxes = scores(t + 1) if t + 1 < len(heads) else None
        inv_l = softmax(t, maxes)
        maxes = next_maxes
        weighted_values(t, inv_l)
        if kvh == N_KV_HEADS - 1:
            rows = slice(j * BLOCK, (j + 1) * BLOCK)
            o_ref[rows, :] = x_ref[rows, :] + _dot_tn(at_ref[j], wo_ref[...])


def _t5_buckets(rel):
    nb = N_BUCKETS // 2
    ret = jnp.where(rel > 0, nb, 0)
    n = jnp.abs(rel)
    max_exact = nb // 2
    nf = jnp.maximum(n, 1).astype(jnp.float32)
    large = max_exact + (jnp.log(nf / max_exact) / math.log(MAX_DISTANCE / max_exact)
                         * (nb - max_exact)).astype(jnp.int32)
    large = jnp.minimum(large, nb - 1)
    return ret + jnp.where(n < max_exact, n, large)


def _attn_layer(x2, qt, k, vt, sink, rel_bias, w_out, seq_len):
    n, d = x2.shape
    t = TOKEN_TILE
    n_blocks = n // BLOCK
    kv_cols = N_KV_HEADS * HEAD_DIM
    q_cols = N_HEADS * HEAD_DIM
    kj = jnp.arange(KEY_SPAN, dtype=jnp.int32)[:, None]
    qi = jnp.arange(BLOCK, dtype=jnp.int32)[None, :]
    buckets_t = _t5_buckets(kj - BLOCK - qi).astype(jnp.int32)

    def prev_block(i):
        return jnp.maximum(i * SUB_BLOCKS - 1, 0)

    def next_block(i):
        return jnp.minimum((i + 1) * SUB_BLOCKS, n_blocks - 1)

    smem = pl.BlockSpec(memory_space=pltpu.SMEM)
    return pl.pallas_call(
        functools.partial(_attn_kernel, seq_len // t),
        grid=(n // t,),
        in_specs=[
            smem,
            smem,
            pl.BlockSpec((t, d), lambda i: (i, 0)),
            pl.BlockSpec((SUB_BLOCKS, q_cols, BLOCK), lambda i: (i, 0, 0)),
            pl.BlockSpec((t, kv_cols), lambda i: (i, 0)),
            pl.BlockSpec((BLOCK, kv_cols), lambda i: (prev_block(i), 0)),
            pl.BlockSpec((BLOCK, kv_cols), lambda i: (next_block(i), 0)),
            pl.BlockSpec((SUB_BLOCKS, kv_cols, BLOCK), lambda i: (i, 0, 0)),
            pl.BlockSpec((1, kv_cols, BLOCK), lambda i: (prev_block(i), 0, 0)),
            pl.BlockSpec((1, kv_cols, BLOCK), lambda i: (next_block(i), 0, 0)),
            _const_spec((KEY_SPAN, BLOCK)),
            _const_spec((d, d)),
        ],
        out_specs=pl.BlockSpec((t, d), lambda i: (i, 0)),
        out_shape=jax.ShapeDtypeStruct((n, d), _F32),
        scratch_shapes=[
            pltpu.VMEM((N_HEADS, KEY_SPAN, BLOCK), _F32),
            pltpu.VMEM((TOKEN_TILE + 2 * BLOCK, kv_cols), _BF16),
            pltpu.VMEM((SUB_BLOCKS + 2, kv_cols, BLOCK), _BF16),
            pltpu.VMEM((2, KEY_SPAN, Q_PER_KV * BLOCK), _F32),
            pltpu.VMEM((2, KEY_SPAN, Q_PER_KV * BLOCK), _BF16),
            pltpu.VMEM((SUB_BLOCKS, q_cols, BLOCK), _BF16),
        ],
        compiler_params=_params(),
        name="window_attention",
    )(sink, rel_bias.reshape(-1), x2, qt, k, k, k, vt, vt, vt, buckets_t, w_out.astype(_BF16))


def kernel(x, norm_mix, norm_ffn, even_w_in, even_v_ln_g, even_v_ln_b, even_w_spatial, even_b_spatial,
           even_conv_w, even_w_out, attn_w_qkv, attn_sink, rel_bias, attn_w_out, ffn_w_gate, ffn_w_up,
           ffn_w_down, final_norm):
    bsz, seq_len, d = x.shape
    assert d == D_MODEL and seq_len % TOKEN_TILE == 0
    x2 = x.reshape(bsz * seq_len, d)
    wg, wu, wd = ffn_w_gate.astype(_BF16), ffn_w_up.astype(_BF16), ffn_w_down.astype(_BF16)
    x2 = _even_layer(x2, norm_mix[0], even_w_in[0], even_v_ln_g[0], even_v_ln_b[0], even_w_spatial[0],
                     even_b_spatial[0], even_conv_w[0], even_w_out[0], seq_len)
    x2 = _ffn_layer(x2, norm_ffn[0], wg, wu, wd, 0)
    qt, k, vt = _qkv_layer(x2, norm_mix[1], attn_w_qkv[0])
    x2 = _attn_layer(x2, qt, k, vt, attn_sink[0], rel_bias, attn_w_out[0], seq_len)
    x2 = _ffn_layer(x2, norm_ffn[1], wg, wu, wd, 1, final_g=final_norm)
    return x2.reshape(bsz, seq_len, d)
```

```python
import functools
import math

import jax
import jax.numpy as jnp
from jax import lax
from jax.experimental import pallas as pl
from jax.experimental.pallas import tpu as pltpu

D_MODEL = 1024
EPS = 1e-6
A_WIDTH = 512
B_WIDTH = 512
A_GROUPS = 4
A_GROUP_DIM = 128
CHUNK = 128
CONV_W = 3
N_HEADS = 16
N_KV_HEADS = 4
Q_PER_KV = 4
HEAD_DIM = 64
WINDOW = 128
BLOCK = 128
N_BUCKETS = 32
MAX_DISTANCE = 128
D_FF = 2816

TOKEN_TILE = 512
HALO = 16
FF_CHUNK = 256
N_FF_CHUNKS = D_FF // FF_CHUNK
SUB_BLOCKS = TOKEN_TILE // BLOCK
KEY_SPAN = 3 * BLOCK
MASK_VALUE = -1e30
ROW_SLAB = 64
OUT_PIECE = 256
VMEM_LIMIT_BYTES = 56 * 1024 * 1024

_BF16 = jnp.bfloat16
_F32 = jnp.float32


def _dot(a, b):
    return jnp.dot(a, b, preferred_element_type=_F32)


def _dot_nt(a, b):
    return lax.dot_general(a, b, (((1,), (1,)), ((), ())), preferred_element_type=_F32)


def _dot_tn(a, b):
    return lax.dot_general(a, b, (((0,), (0,)), ((), ())), preferred_element_type=_F32)


def _gelu(x):
    return 0.5 * x * (1.0 + lax.erf(x * math.sqrt(0.5)))


def _rms(x, g):
    return (x * lax.rsqrt(jnp.mean(x * x, axis=-1, keepdims=True) + EPS)) * g


def _const_spec(shape):
    zeros = (0,) * len(shape)
    return pl.BlockSpec(shape, lambda i: zeros, pipeline_mode=pl.Buffered(1))


def _params():
    return pltpu.CompilerParams(
        dimension_semantics=("arbitrary",), vmem_limit_bytes=VMEM_LIMIT_BYTES)


def _even_kernel(tiles_per_seq, x_ref, xp_ref, xn_ref, g_ref, win_ref, lng_ref, lnb_ref,
                 wsp_ref, bsp_ref, cw_ref, wo_ref, o_ref, hs_ref, y_ref):
    t = TOKEN_TILE
    i = pl.program_id(0)
    pos = i % tiles_per_seq
    g = g_ref[...]
    x = x_ref[...]
    h = _rms(x, g).astype(_BF16)
    hp = jnp.where(pos == 0, 0.0, _rms(xp_ref[...], g)).astype(_BF16)
    hn = jnp.where(pos == tiles_per_seq - 1, 0.0, _rms(xn_ref[...], g)).astype(_BF16)
    hs_ref[0:HALO, :] = hp
    hs_ref[HALO:HALO + t, :] = h
    hs_ref[HALO + t:, :] = hn

    in_width_a = 2 * A_WIDTH + B_WIDTH
    pa = _dot(h, win_ref[:, :in_width_a])
    pc = _dot(hs_ref[...], win_ref[:, in_width_a:])

    u = _gelu(pa[:, :A_WIDTH])
    v = _gelu(pa[:, A_WIDTH:2 * A_WIDTH])
    mu = jnp.mean(v, axis=-1, keepdims=True)
    vc = v - mu
    vn = vc * lax.rsqrt(jnp.mean(vc * vc, axis=-1, keepdims=True) + EPS)
    vb = (vn * lng_ref[...] + lnb_ref[...]).astype(_BF16)
    for c in range(t // CHUNK):
        rows = slice(c * CHUNK, (c + 1) * CHUNK)
        for grp in range(A_GROUPS):
            cols = slice(grp * A_GROUP_DIM, (grp + 1) * A_GROUP_DIM)
            mixed = _dot(wsp_ref[grp], vb[rows, cols]) + bsp_ref[grp]
            y_ref[rows, cols] = (u[rows, cols] * mixed).astype(_BF16)

    z = pc[:, :B_WIDTH] * pc[:, B_WIDTH:]
    n_rows = t + 2 * HALO
    z_prev = pltpu.roll(z, 1, 0)[HALO:HALO + t]
    z_next = pltpu.roll(z, n_rows - 1, 0)[HALO:HALO + t]
    cw = cw_ref[...]
    conv = z_prev * cw[0:1] + z[HALO:HALO + t] * cw[1:2] + z_next * cw[2:3]
    y_ref[:, A_WIDTH:] = (pa[:, 2 * A_WIDTH:] * conv).astype(_BF16)

    o_ref[...] = x + _dot(y_ref[...], wo_ref[...])


def _even_layer(x2, g, w_in, ln_g, ln_b, w_sp, b_sp, conv_w, w_out, seq_len):
    n, d = x2.shape
    t = TOKEN_TILE
    halo_blocks_per_tile = t // HALO
    n_halo_blocks = n // HALO
    return pl.pallas_call(
        functools.partial(_even_kernel, seq_len // t),
        grid=(n // t,),
        in_specs=[
            pl.BlockSpec((t, d), lambda i: (i, 0)),
            pl.BlockSpec((HALO, d), lambda i: (jnp.maximum(i * halo_blocks_per_tile - 1, 0), 0)),
            pl.BlockSpec((HALO, d),
                         lambda i: (jnp.minimum((i + 1) * halo_blocks_per_tile, n_halo_blocks - 1), 0)),
            _const_spec((1, d)),
            _const_spec(w_in.shape),
            _const_spec((1, A_WIDTH)),
            _const_spec((1, A_WIDTH)),
            _const_spec((A_GROUPS, CHUNK, CHUNK)),
            _const_spec((A_GROUPS, CHUNK, 1)),
            _const_spec((CONV_W, B_WIDTH)),
            _const_spec((d, d)),
        ],
        out_specs=pl.BlockSpec((t, d), lambda i: (i, 0)),
        out_shape=jax.ShapeDtypeStruct((n, d), _F32),
        scratch_shapes=[
            pltpu.VMEM((t + 2 * HALO, d), _BF16),
            pltpu.VMEM((t, d), _BF16),
        ],
        compiler_params=_params(),
        name="even_mixer",
    )(x2, x2, x2, g.reshape(1, d), w_in.astype(_BF16), ln_g.reshape(1, -1), ln_b.reshape(1, -1),
      w_sp.astype(_BF16), b_sp[..., None], conv_w, w_out.astype(_BF16))


def _ffn_kernel(has_final_norm, x_ref, g_ref, wg_ref, wu_ref, wd_ref, *rest):
    if has_final_norm:
        gf_ref, o_ref, h_ref, acc_ref = rest
    else:
        o_ref, h_ref, acc_ref = rest
    x = x_ref[...]
    h_ref[...] = _rms(x, g_ref[...]).astype(_BF16)
    acc_ref[...] = x

    h = h_ref[...]
    for c in range(N_FF_CHUNKS):
        cols = slice(c * FF_CHUNK, (c + 1) * FF_CHUNK)
        gate = _dot(h, wg_ref[:, cols])
        up = _dot(h, wu_ref[:, cols])
        act = (jax.nn.silu(gate) * up).astype(_BF16)
        acc_ref[...] += _dot(act, wd_ref[cols, :])
    out = acc_ref[...]
    if has_final_norm:
        out = _rms(out, gf_ref[...])
    o_ref[...] = out


def _layer_spec(stacked_shape, layer):
    zeros = (0,) * (len(stacked_shape) - 1)
    return pl.BlockSpec((None,) + tuple(stacked_shape[1:]), lambda i: (layer,) + zeros,
                        pipeline_mode=pl.Buffered(1))


def _ffn_layer(x2, g, wg, wu, wd, layer, final_g=None):
    n, d = x2.shape
    t = TOKEN_TILE
    in_specs = [
        pl.BlockSpec((t, d), lambda i: (i, 0)),
        _const_spec((1, d)),
        _layer_spec(wg.shape, layer),
        _layer_spec(wu.shape, layer),
        _layer_spec(wd.shape, layer),
    ]
    args = [x2, g.reshape(1, d), wg, wu, wd]
    if final_g is not None:
        in_specs.append(_const_spec((1, d)))
        args.append(final_g.reshape(1, d))
    return pl.pallas_call(
        functools.partial(_ffn_kernel, final_g is not None),
        grid=(n // t,),
        in_specs=in_specs,
        out_specs=pl.BlockSpec((t, d), lambda i: (i, 0)),
        out_shape=jax.ShapeDtypeStruct((n, d), _F32),
        scratch_shapes=[pltpu.VMEM((t, d), _BF16), pltpu.VMEM((t, d), _F32)],
        compiler_params=_params(),
        name="swiglu_ffn",
    )(*args)


def _qkv_kernel(x_ref, g_ref, wqv_ref, wk_ref, qt_ref, k_ref, vt_ref):
    h = _rms(x_ref[...], g_ref[...]).astype(_BF16)
    qv = _dot_nt(wqv_ref[...], h)
    q_rows = N_HEADS * HEAD_DIM
    scale = HEAD_DIM ** -0.5
    for j in range(SUB_BLOCKS):
        lanes = slice(j * BLOCK, (j + 1) * BLOCK)
        qt_ref[j] = (qv[:q_rows, lanes] * scale).astype(_BF16)
        vt_ref[j] = qv[q_rows:, lanes].astype(_BF16)
    k_ref[...] = _dot(h, wk_ref[...]).astype(_BF16)


def _qkv_layer(x2, g, w_qkv):
    n, d = x2.shape
    t = TOKEN_TILE
    q_cols = N_HEADS * HEAD_DIM
    kv_cols = N_KV_HEADS * HEAD_DIM
    wq = w_qkv[:, :q_cols]
    wk = w_qkv[:, q_cols:q_cols + kv_cols]
    wv = w_qkv[:, q_cols + kv_cols:]
    wqv_t = jnp.concatenate([wq, wv], axis=1).T.astype(_BF16)
    n_blocks = n // BLOCK
    return pl.pallas_call(
        _qkv_kernel,
        grid=(n // t,),
        in_specs=[
            pl.BlockSpec((t, d), lambda i: (i, 0)),
            _const_spec((1, d)),
            _const_spec(wqv_t.shape),
            _const_spec((d, kv_cols)),
        ],
        out_specs=[
            pl.BlockSpec((SUB_BLOCKS, q_cols, BLOCK), lambda i: (i, 0, 0)),
            pl.BlockSpec((t, kv_cols), lambda i: (i, 0)),
            pl.BlockSpec((SUB_BLOCKS, kv_cols, BLOCK), lambda i: (i, 0, 0)),
        ],
        out_shape=[
            jax.ShapeDtypeStruct((n_blocks, q_cols, BLOCK), _BF16),
            jax.ShapeDtypeStruct((n, kv_cols), _BF16),
            jax.ShapeDtypeStruct((n_blocks, kv_cols, BLOCK), _BF16),
        ],
        compiler_params=_params(),
        name="qkv_proj",
    )(x2, g.reshape(1, d), wqv_t, wk.astype(_BF16))


def _attn_kernel(tiles_per_seq, sink_ref, rb_ref, x_ref, qt_ref, kc_ref, kp_ref, kn_ref,
                 vc_ref, vp_ref, vn_ref, bkt_ref, wo_ref, o_ref,
                 bias_ref, kall_ref, vall_ref, sc_ref, p_ref, at_ref):
    i = pl.program_id(0)
    pos = i % tiles_per_seq

    @pl.when(i == 0)
    def _init():
        bkt = bkt_ref[...]
        rel = (lax.broadcasted_iota(jnp.int32, (KEY_SPAN, BLOCK), 0) - BLOCK
               - lax.broadcasted_iota(jnp.int32, (KEY_SPAN, BLOCK), 1))
        band = jnp.abs(rel) <= WINDOW

        def fill_head(hd, carry):
            tbl = jnp.zeros((KEY_SPAN, BLOCK), _F32)
            for b in range(N_BUCKETS):
                tbl = jnp.where(bkt == b, rb_ref[b * N_HEADS + hd], tbl)
            bias_ref[hd] = jnp.where(band, tbl, MASK_VALUE)
            return carry

        lax.fori_loop(0, N_HEADS, fill_head, 0)

    kall_ref[0:BLOCK, :] = kp_ref[...]
    kall_ref[BLOCK:BLOCK + TOKEN_TILE, :] = kc_ref[...]
    kall_ref[BLOCK + TOKEN_TILE:, :] = kn_ref[...]
    vall_ref[0] = vp_ref[0]
    for j in range(SUB_BLOCKS):
        vall_ref[1 + j] = vc_ref[j]
    vall_ref[1 + SUB_BLOCKS] = vn_ref[0]

    pen_first = jnp.where(pos == 0, MASK_VALUE, 0.0)
    pen_last = jnp.where(pos == tiles_per_seq - 1, MASK_VALUE, 0.0)

    pair_lanes = 2 * HEAD_DIM
    slabs = [slice(r0, r0 + ROW_SLAB) for r0 in range(0, KEY_SPAN, ROW_SLAB)]
    heads = [(j, kvh) for j in range(SUB_BLOCKS) for kvh in range(N_KV_HEADS)]

    def scores(t):
        j, kvh = heads[t]
        pair, half = kvh // 2, kvh % 2
        q_heads = jnp.concatenate(
            [qt_ref[j, (kvh * Q_PER_KV + grp) * HEAD_DIM:(kvh * Q_PER_KV + grp + 1) * HEAD_DIM, :]
             for grp in range(Q_PER_KV)], axis=1)
        q_zero = jnp.zeros_like(q_heads)
        q2 = jnp.concatenate([q_heads, q_zero] if half == 0 else [q_zero, q_heads], axis=0)
        kwin = kall_ref[j * BLOCK:j * BLOCK + KEY_SPAN, pair * pair_lanes:(pair + 1) * pair_lanes]
        raw = _dot(kwin, q2)
        sc_buf = sc_ref.at[t % 2]
        maxes = []
        for grp in range(Q_PER_KV):
            hd = kvh * Q_PER_KV + grp
            lanes = slice(grp * BLOCK, (grp + 1) * BLOCK)
            mx = None
            for rows in slabs:
                s = raw[rows, lanes] + bias_ref[hd, rows, :]
                if j == 0 and rows.stop <= BLOCK:
                    s = s + pen_first
                if j == SUB_BLOCKS - 1 and rows.start >= 2 * BLOCK:
                    s = s + pen_last
                sc_buf[rows, lanes] = s
                mx = s if mx is None else jnp.maximum(mx, s)
            maxes.append(jnp.maximum(jnp.max(mx, axis=0, keepdims=True), sink_ref[hd]))
        return maxes

    def softmax(t, maxes):
        j, kvh = heads[t]
        sc_buf = sc_ref.at[t % 2]
        p_buf = p_ref.at[t % 2]
        inv_l = []
        for grp in range(Q_PER_KV):
            hd = kvh * Q_PER_KV + grp
            lanes = slice(grp * BLOCK, (grp + 1) * BLOCK)
            m = maxes[grp]
            acc = None
            for rows in slabs:
                p = jnp.exp(sc_buf[rows, lanes] - m)
                acc = p if acc is None else acc + p
                p_buf[rows, lanes] = p.astype(_BF16)
            l = jnp.sum(acc, axis=0, keepdims=True) + jnp.exp(sink_ref[hd] - m)
            inv_l.append(1.0 / l)
        return inv_l

    def weighted_values(t, inv_l):
        j, kvh = heads[t]
        vwin = jnp.concatenate(
            [vall_ref[j + b, kvh * HEAD_DIM:(kvh + 1) * HEAD_DIM, :] for b in range(3)], axis=1)
        ot = _dot(vwin, p_ref[t % 2])
        for grp in range(Q_PER_KV):
            hd = kvh * Q_PER_KV + grp
            at_ref[j, hd * HEAD_DIM:(hd + 1) * HEAD_DIM, :] = (
                ot[:, grp * BLOCK:(grp + 1) * BLOCK] * inv_l[grp]).astype(_BF16)

    def out_proj_piece(j, piece):
        rows = slice(j * BLOCK, (j + 1) * BLOCK)
        cols = slice(piece * OUT_PIECE, (piece + 1) * OUT_PIECE)
        o_ref[rows, cols] = x_ref[rows, cols] + _dot_tn(at_ref[j], wo_ref[:, cols])

    pieces_per_head = (D_MODEL // OUT_PIECE) // N_KV_HEADS
    maxes = {0: scores(0), 1: scores(1)}
    for t, (j, kvh) in enumerate(heads):
        inv_l = softmax(t, maxes.pop(t))
        weighted_values(t, inv_l)
        if j > 0:
            for piece in range(kvh * pieces_per_head, (kvh + 1) * pieces_per_head):
                out_proj_piece(j - 1, piece)
        if t + 2 < len(heads):
            maxes[t + 2] = scores(t + 2)
    for piece in range(D_MODEL // OUT_PIECE):
        out_proj_piece(SUB_BLOCKS - 1, piece)


def _t5_buckets(rel):
    nb = N_BUCKETS // 2
    ret = jnp.where(rel > 0, nb, 0)
    n = jnp.abs(rel)
    max_exact = nb // 2
    nf = jnp.maximum(n, 1).astype(jnp.float32)
    large = max_exact + (jnp.log(nf / max_exact) / math.log(MAX_DISTANCE / max_exact)
                         * (nb - max_exact)).astype(jnp.int32)
    large = jnp.minimum(large, nb - 1)
    return ret + jnp.where(n < max_exact, n, large)


def _attn_layer(x2, qt, k, vt, sink, rel_bias, w_out, seq_len):
    n, d = x2.shape
    t = TOKEN_TILE
    n_blocks = n // BLOCK
    kv_cols = N_KV_HEADS * HEAD_DIM
    q_cols = N_HEADS * HEAD_DIM
    kj = jnp.arange(KEY_SPAN, dtype=jnp.int32)[:, None]
    qi = jnp.arange(BLOCK, dtype=jnp.int32)[None, :]
    buckets_t = _t5_buckets(kj - BLOCK - qi).astype(jnp.int32)

    def prev_block(i):
        return jnp.maximum(i * SUB_BLOCKS - 1, 0)

    def next_block(i):
        return jnp.minimum((i + 1) * SUB_BLOCKS, n_blocks - 1)

    smem = pl.BlockSpec(memory_space=pltpu.SMEM)
    return pl.pallas_call(
        functools.partial(_attn_kernel, seq_len // t),
        grid=(n // t,),
        in_specs=[
            smem,
            smem,
            pl.BlockSpec((t, d), lambda i: (i, 0)),
            pl.BlockSpec((SUB_BLOCKS, q_cols, BLOCK), lambda i: (i, 0, 0)),
            pl.BlockSpec((t, kv_cols), lambda i: (i, 0)),
            pl.BlockSpec((BLOCK, kv_cols), lambda i: (prev_block(i), 0)),
            pl.BlockSpec((BLOCK, kv_cols), lambda i: (next_block(i), 0)),
            pl.BlockSpec((SUB_BLOCKS, kv_cols, BLOCK), lambda i: (i, 0, 0)),
            pl.BlockSpec((1, kv_cols, BLOCK), lambda i: (prev_block(i), 0, 0)),
            pl.BlockSpec((1, kv_cols, BLOCK), lambda i: (next_block(i), 0, 0)),
            _const_spec((KEY_SPAN, BLOCK)),
            _const_spec((d, d)),
        ],
        out_specs=pl.BlockSpec((t, d), lambda i: (i, 0)),
        out_shape=jax.ShapeDtypeStruct((n, d), _F32),
        scratch_shapes=[
            pltpu.VMEM((N_HEADS, KEY_SPAN, BLOCK), _F32),
            pltpu.VMEM((TOKEN_TILE + 2 * BLOCK, kv_cols), _BF16),
            pltpu.VMEM((SUB_BLOCKS + 2, kv_cols, BLOCK), _BF16),
            pltpu.VMEM((2, KEY_SPAN, Q_PER_KV * BLOCK), _F32),
            pltpu.VMEM((2, KEY_SPAN, Q_PER_KV * BLOCK), _BF16),
            pltpu.VMEM((SUB_BLOCKS, q_cols, BLOCK), _BF16),
        ],
        compiler_params=_params(),
        name="window_attention",
    )(sink, rel_bias.reshape(-1), x2, qt, k, k, k, vt, vt, vt, buckets_t, w_out.astype(_BF16))


def kernel(x, norm_mix, norm_ffn, even_w_in, even_v_ln_g, even_v_ln_b, even_w_spatial, even_b_spatial,
           even_conv_w, even_w_out, attn_w_qkv, attn_sink, rel_bias, attn_w_out, ffn_w_gate, ffn_w_up,
           ffn_w_down, final_norm):
    bsz, seq_len, d = x.shape
    assert d == D_MODEL and seq_len % TOKEN_TILE == 0
    x2 = x.reshape(bsz * seq_len, d)
    wg, wu, wd = ffn_w_gate.astype(_BF16), ffn_w_up.astype(_BF16), ffn_w_down.astype(_BF16)
    x2 = _even_layer(x2, norm_mix[0], even_w_in[0], even_v_ln_g[0], even_v_ln_b[0], even_w_spatial[0],
                     even_b_spatial[0], even_conv_w[0], even_w_out[0], seq_len)
    x2 = _ffn_layer(x2, norm_ffn[0], wg, wu, wd, 0)
    qt, k, vt = _qkv_layer(x2, norm_mix[1], attn_w_qkv[0])
    x2 = _attn_layer(x2, qt, k, vt, attn_sink[0], rel_bias, attn_w_out[0], seq_len)
    x2 = _ffn_layer(x2, norm_ffn[1], wg, wu, wd, 1, final_g=final_norm)
    return x2.reshape(bsz, seq_len, d)
```

```python
import functools
import math

import jax
import jax.numpy as jnp
from jax import lax
from jax.experimental import pallas as pl
from jax.experimental.pallas import tpu as pltpu

D_MODEL = 1024
EPS = 1e-6
A_WIDTH = 512
B_WIDTH = 512
A_GROUPS = 4
A_GROUP_DIM = 128
CHUNK = 128
CONV_W = 3
N_HEADS = 16
N_KV_HEADS = 4
Q_PER_KV = 4
HEAD_DIM = 64
WINDOW = 128
BLOCK = 128
N_BUCKETS = 32
MAX_DISTANCE = 128
D_FF = 2816

TOKEN_TILE = 512
HALO = 16
FF_CHUNK = 256
N_FF_CHUNKS = D_FF // FF_CHUNK
SUB_BLOCKS = TOKEN_TILE // BLOCK
KEY_SPAN = 3 * BLOCK
MASK_VALUE = -1e30
ROW_SLAB = 64
OUT_PIECE = 512
Q_COLS = N_HEADS * HEAD_DIM
KV_COLS = N_KV_HEADS * HEAD_DIM
PROJ_SLAB = 512
N_PROJ_SLABS = (Q_COLS + 2 * KV_COLS) // PROJ_SLAB
VMEM_LIMIT_BYTES = 56 * 1024 * 1024

_BF16 = jnp.bfloat16
_F32 = jnp.float32


def _dot(a, b):
    return jnp.dot(a, b, preferred_element_type=_F32)


def _dot_nt(a, b):
    return lax.dot_general(a, b, (((1,), (1,)), ((), ())), preferred_element_type=_F32)


def _dot_tn(a, b):
    return lax.dot_general(a, b, (((0,), (0,)), ((), ())), preferred_element_type=_F32)


def _gelu(x):
    return 0.5 * x * (1.0 + lax.erf(x * math.sqrt(0.5)))


def _rms(x, g):
    return (x * lax.rsqrt(jnp.mean(x * x, axis=-1, keepdims=True) + EPS)) * g


def _const_spec(shape):
    zeros = (0,) * len(shape)
    return pl.BlockSpec(shape, lambda i: zeros, pipeline_mode=pl.Buffered(1))


def _params():
    return pltpu.CompilerParams(
        dimension_semantics=("arbitrary",), vmem_limit_bytes=VMEM_LIMIT_BYTES)


def _even_kernel(tiles_per_seq, x_ref, xp_ref, xn_ref, g_ref, win_ref, lng_ref, lnb_ref,
                 wsp_ref, bsp_ref, cw_ref, wo_ref, o_ref, hs_ref, y_ref):
    t = TOKEN_TILE
    i = pl.program_id(0)
    pos = i % tiles_per_seq
    g = g_ref[...]
    x = x_ref[...]
    h = _rms(x, g).astype(_BF16)
    hp = jnp.where(pos == 0, 0.0, _rms(xp_ref[...], g)).astype(_BF16)
    hn = jnp.where(pos == tiles_per_seq - 1, 0.0, _rms(xn_ref[...], g)).astype(_BF16)
    hs_ref[0:HALO, :] = hp
    hs_ref[HALO:HALO + t, :] = h
    hs_ref[HALO + t:, :] = hn

    in_width_a = 2 * A_WIDTH + B_WIDTH
    pa = _dot(h, win_ref[:, :in_width_a])
    pc = _dot(hs_ref[...], win_ref[:, in_width_a:])

    u = _gelu(pa[:, :A_WIDTH])
    v = _gelu(pa[:, A_WIDTH:2 * A_WIDTH])
    mu = jnp.mean(v, axis=-1, keepdims=True)
    vc = v - mu
    vn = vc * lax.rsqrt(jnp.mean(vc * vc, axis=-1, keepdims=True) + EPS)
    vb = (vn * lng_ref[...] + lnb_ref[...]).astype(_BF16)
    for c in range(t // CHUNK):
        rows = slice(c * CHUNK, (c + 1) * CHUNK)
        for grp in range(A_GROUPS):
            cols = slice(grp * A_GROUP_DIM, (grp + 1) * A_GROUP_DIM)
            mixed = _dot(wsp_ref[grp], vb[rows, cols]) + bsp_ref[grp]
            y_ref[rows, cols] = (u[rows, cols] * mixed).astype(_BF16)

    z = pc[:, :B_WIDTH] * pc[:, B_WIDTH:]
    n_rows = t + 2 * HALO
    z_prev = pltpu.roll(z, 1, 0)[HALO:HALO + t]
    z_next = pltpu.roll(z, n_rows - 1, 0)[HALO:HALO + t]
    cw = cw_ref[...]
    conv = z_prev * cw[0:1] + z[HALO:HALO + t] * cw[1:2] + z_next * cw[2:3]
    y_ref[:, A_WIDTH:] = (pa[:, 2 * A_WIDTH:] * conv).astype(_BF16)

    o_ref[...] = x + _dot(y_ref[...], wo_ref[...])


def _even_layer(x2, g, w_in, ln_g, ln_b, w_sp, b_sp, conv_w, w_out, seq_len):
    n, d = x2.shape
    t = TOKEN_TILE
    halo_blocks_per_tile = t // HALO
    n_halo_blocks = n // HALO
    return pl.pallas_call(
        functools.partial(_even_kernel, seq_len // t),
        grid=(n // t,),
        in_specs=[
            pl.BlockSpec((t, d), lambda i: (i, 0)),
            pl.BlockSpec((HALO, d), lambda i: (jnp.maximum(i * halo_blocks_per_tile - 1, 0), 0)),
            pl.BlockSpec((HALO, d),
                         lambda i: (jnp.minimum((i + 1) * halo_blocks_per_tile, n_halo_blocks - 1), 0)),
            _const_spec((1, d)),
            _const_spec(w_in.shape),
            _const_spec((1, A_WIDTH)),
            _const_spec((1, A_WIDTH)),
            _const_spec((A_GROUPS, CHUNK, CHUNK)),
            _const_spec((A_GROUPS, CHUNK, 1)),
            _const_spec((CONV_W, B_WIDTH)),
            _const_spec((d, d)),
        ],
        out_specs=pl.BlockSpec((t, d), lambda i: (i, 0)),
        out_shape=jax.ShapeDtypeStruct((n, d), _F32),
        scratch_shapes=[
            pltpu.VMEM((t + 2 * HALO, d), _BF16),
            pltpu.VMEM((t, d), _BF16),
        ],
        compiler_params=_params(),
        name="even_mixer",
    )(x2, x2, x2, g.reshape(1, d), w_in.astype(_BF16), ln_g.reshape(1, -1), ln_b.reshape(1, -1),
      w_sp.astype(_BF16), b_sp[..., None], conv_w, w_out.astype(_BF16))


def _ffn_kernel(has_final_norm, x_ref, g_ref, wg_ref, wu_ref, wd_ref, *rest):
    if has_final_norm:
        gf_ref, o_ref, h_ref, acc_ref = rest
    else:
        o_ref, h_ref, acc_ref = rest
    x = x_ref[...]
    h_ref[...] = _rms(x, g_ref[...]).astype(_BF16)
    acc_ref[...] = x

    h = h_ref[...]
    for c in range(N_FF_CHUNKS):
        cols = slice(c * FF_CHUNK, (c + 1) * FF_CHUNK)
        gate = _dot(h, wg_ref[:, cols])
        up = _dot(h, wu_ref[:, cols])
        act = (jax.nn.silu(gate) * up).astype(_BF16)
        acc_ref[...] += _dot(act, wd_ref[cols, :])
    out = acc_ref[...]
    if has_final_norm:
        out = _rms(out, gf_ref[...])
    o_ref[...] = out


def _layer_spec(stacked_shape, layer):
    zeros = (0,) * (len(stacked_shape) - 1)
    return pl.BlockSpec((None,) + tuple(stacked_shape[1:]), lambda i: (layer,) + zeros,
                        pipeline_mode=pl.Buffered(1))


def _ffn_layer(x2, g, wg, wu, wd, layer, final_g=None):
    n, d = x2.shape
    t = TOKEN_TILE
    in_specs = [
        pl.BlockSpec((t, d), lambda i: (i, 0)),
        _const_spec((1, d)),
        _layer_spec(wg.shape, layer),
        _layer_spec(wu.shape, layer),
        _layer_spec(wd.shape, layer),
    ]
    args = [x2, g.reshape(1, d), wg, wu, wd]
    if final_g is not None:
        in_specs.append(_const_spec((1, d)))
        args.append(final_g.reshape(1, d))
    return pl.pallas_call(
        functools.partial(_ffn_kernel, final_g is not None),
        grid=(n // t,),
        in_specs=in_specs,
        out_specs=pl.BlockSpec((t, d), lambda i: (i, 0)),
        out_shape=jax.ShapeDtypeStruct((n, d), _F32),
        scratch_shapes=[pltpu.VMEM((t, d), _BF16), pltpu.VMEM((t, d), _F32)],
        compiler_params=_params(),
        name="swiglu_ffn",
    )(*args)


def _attn_kernel(tiles_per_seq, sink_ref, rb_ref, x_ref, xnext_ref, g_ref, wqkv_ref, bkt_ref, wo_ref,
                 o_ref, bias_ref, h_ref, q_ring, kz_ring, vt_ring, sc_ref, p_ref, at_ref):
    t = TOKEN_TILE
    i = pl.program_id(0)
    pos = i % tiles_per_seq
    kv_prev, kv_cur, kv_next = (i + 2) % 3, i % 3, (i + 1) % 3
    q_cur, q_next = i % 2, (i + 1) % 2
    pair_lanes = 2 * HEAD_DIM
    n_q_slabs = Q_COLS // PROJ_SLAB
    lane_half = lax.broadcasted_iota(jnp.int32, (t, pair_lanes), 1) // HEAD_DIM

    def project_slab(s, q_slot, kv_slot):
        cols = slice(s * PROJ_SLAB, (s + 1) * PROJ_SLAB)
        y = _dot(h_ref[...], wqkv_ref[:, cols])
        if s < n_q_slabs:
            q_ring[q_slot, :, cols] = (y * HEAD_DIM ** -0.5).astype(_BF16)
            return
        for kvh in range(N_KV_HEADS):
            pair = y[:, (kvh // 2) * pair_lanes:(kvh // 2 + 1) * pair_lanes]
            kz_ring[kv_slot, :, kvh * pair_lanes:(kvh + 1) * pair_lanes] = jnp.where(
                lane_half == kvh % 2, pair, 0.0).astype(_BF16)
        vt_ring[kv_slot] = y[:, KV_COLS:].T.astype(_BF16)

    @pl.when(i == 0)
    def _init():
        bkt = bkt_ref[...]
        rel = (lax.broadcasted_iota(jnp.int32, (KEY_SPAN, BLOCK), 0) - BLOCK
               - lax.broadcasted_iota(jnp.int32, (KEY_SPAN, BLOCK), 1))
        band = jnp.abs(rel) <= WINDOW

        def fill_head(hd, carry):
            tbl = jnp.zeros((KEY_SPAN, BLOCK), _F32)
            for b in range(N_BUCKETS):
                tbl = jnp.where(bkt == b, rb_ref[b * N_HEADS + hd], tbl)
            bias_ref[hd] = jnp.where(band, tbl, MASK_VALUE)
            return carry

        lax.fori_loop(0, N_HEADS, fill_head, 0)
        kz_ring[2] = jnp.zeros(kz_ring.shape[1:], _BF16)
        vt_ring[2] = jnp.zeros(vt_ring.shape[1:], _BF16)
        h_ref[...] = _rms(x_ref[...], g_ref[...]).astype(_BF16)
        for s in range(N_PROJ_SLABS):
            project_slab(s, 0, 0)

    pen_first = jnp.where(pos == 0, MASK_VALUE, 0.0)
    pen_last = jnp.where(pos == tiles_per_seq - 1, MASK_VALUE, 0.0)

    slabs = [slice(r0, r0 + ROW_SLAB) for r0 in range(0, KEY_SPAN, ROW_SLAB)]
    heads = [(j, kvh) for j in range(SUB_BLOCKS) for kvh in range(N_KV_HEADS)]

    def key_blocks(j):
        out = []
        for jb in (j - 1, j, j + 1):
            if jb < 0:
                out.append((kv_prev, slice(t - BLOCK, t)))
            elif jb >= SUB_BLOCKS:
                out.append((kv_next, slice(0, BLOCK)))
            else:
                out.append((kv_cur, slice(jb * BLOCK, (jb + 1) * BLOCK)))
        return out

    def scores(ti):
        j, kvh = heads[ti]
        kz_win = jnp.concatenate(
            [kz_ring[slot, rows, kvh * pair_lanes:(kvh + 1) * pair_lanes] for slot, rows in key_blocks(j)],
            axis=0)
        q_col0 = (kvh // 2) * Q_PER_KV
        q4 = jnp.concatenate(
            [q_ring[q_cur, j * BLOCK:(j + 1) * BLOCK, (q_col0 + grp) * pair_lanes:(q_col0 + grp + 1) * pair_lanes]
             for grp in range(Q_PER_KV)], axis=0)
        raw = _dot_nt(kz_win, q4)
        sc_buf = sc_ref.at[ti % 2]
        maxes = []
        for grp in range(Q_PER_KV):
            hd = kvh * Q_PER_KV + grp
            lanes = slice(grp * BLOCK, (grp + 1) * BLOCK)
            mx = None
            for rows in slabs:
                s = raw[rows, lanes] + bias_ref[hd, rows, :]
                if j == 0 and rows.stop <= BLOCK:
                    s = s + pen_first
                if j == SUB_BLOCKS - 1 and rows.start >= 2 * BLOCK:
                    s = s + pen_last
                sc_buf[rows, lanes] = s
                mx = s if mx is None else jnp.maximum(mx, s)
            maxes.append(jnp.maximum(jnp.max(mx, axis=0, keepdims=True), sink_ref[hd]))
        return maxes

    def softmax(ti, maxes):
        j, kvh = heads[ti]
        sc_buf = sc_ref.at[ti % 2]
        p_buf = p_ref.at[ti % 2]
        inv_l = []
        for grp in range(Q_PER_KV):
            hd = kvh * Q_PER_KV + grp
            lanes = slice(grp * BLOCK, (grp + 1) * BLOCK)
            m = maxes[grp]
            acc = None
            for rows in slabs:
                p = jnp.exp(sc_buf[rows, lanes] - m)
                acc = p if acc is None else acc + p
                p_buf[rows, lanes] = p.astype(_BF16)
            l = jnp.sum(acc, axis=0, keepdims=True) + jnp.exp(sink_ref[hd] - m)
            inv_l.append(1.0 / l)
        return inv_l

    def weighted_values(ti, inv_l):
        j, kvh = heads[ti]
        vwin = jnp.concatenate(
            [vt_ring[slot, kvh * HEAD_DIM:(kvh + 1) * HEAD_DIM, rows] for slot, rows in key_blocks(j)],
            axis=1)
        ot = _dot(vwin, p_ref[ti % 2])
        for grp in range(Q_PER_KV):
            hd = kvh * Q_PER_KV + grp
            at_ref[j, hd * HEAD_DIM:(hd + 1) * HEAD_DIM, :] = (
                ot[:, grp * BLOCK:(grp + 1) * BLOCK] * inv_l[grp]).astype(_BF16)

    def out_proj_piece(j, piece):
        rows = slice(j * BLOCK, (j + 1) * BLOCK)
        cols = slice(piece * OUT_PIECE, (piece + 1) * OUT_PIECE)
        o_ref[rows, cols] = x_ref[rows, cols] + _dot_tn(at_ref[j], wo_ref[:, cols])

    n_pieces = D_MODEL // OUT_PIECE
    piece_at = {(p * N_KV_HEADS) // n_pieces + 1: p for p in range(n_pieces)}
    slab_order = [n_q_slabs] + list(range(n_q_slabs))
    slab_at = {(k * len(heads)) // (len(slab_order) + 1): s for k, s in enumerate(slab_order)}
    maxes = {0: scores(0), 1: scores(1)}
    h_ref[...] = _rms(xnext_ref[...], g_ref[...]).astype(_BF16)
    for ti, (j, kvh) in enumerate(heads):
        inv_l = softmax(ti, maxes.pop(ti))
        weighted_values(ti, inv_l)
        if ti + 2 < len(heads):
            maxes[ti + 2] = scores(ti + 2)
        if j > 0 and kvh in piece_at:
            out_proj_piece(j - 1, piece_at[kvh])
        if ti in slab_at:
            project_slab(slab_at[ti], q_next, kv_next)
    for piece in range(D_MODEL // OUT_PIECE):
        out_proj_piece(SUB_BLOCKS - 1, piece)


def _t5_buckets(rel):
    nb = N_BUCKETS // 2
    ret = jnp.where(rel > 0, nb, 0)
    n = jnp.abs(rel)
    max_exact = nb // 2
    nf = jnp.maximum(n, 1).astype(jnp.float32)
    large = max_exact + (jnp.log(nf / max_exact) / math.log(MAX_DISTANCE / max_exact)
                         * (nb - max_exact)).astype(jnp.int32)
    large = jnp.minimum(large, nb - 1)
    return ret + jnp.where(n < max_exact, n, large)


def _attn_layer(x2, g, w_qkv, sink, rel_bias, w_out, seq_len):
    n, d = x2.shape
    t = TOKEN_TILE
    last = n // t - 1
    kj = jnp.arange(KEY_SPAN, dtype=jnp.int32)[:, None]
    qi = jnp.arange(BLOCK, dtype=jnp.int32)[None, :]
    buckets_t = _t5_buckets(kj - BLOCK - qi).astype(jnp.int32)
    wq = w_qkv[:, :Q_COLS].reshape(d, N_KV_HEADS // 2, 2, Q_PER_KV, HEAD_DIM)
    wq = wq.transpose(0, 1, 3, 2, 4).reshape(d, Q_COLS)
    wqkv = jnp.concatenate([wq, w_qkv[:, Q_COLS:]], axis=1).astype(_BF16)

    smem = pl.BlockSpec(memory_space=pltpu.SMEM)
    return pl.pallas_call(
        functools.partial(_attn_kernel, seq_len // t),
        grid=(n // t,),
        in_specs=[
            smem,
            smem,
            pl.BlockSpec((t, d), lambda i: (i, 0)),
            pl.BlockSpec((t, d), lambda i: (jnp.minimum(i + 1, last), 0)),
            _const_spec((1, d)),
            _const_spec(wqkv.shape),
            _const_spec((KEY_SPAN, BLOCK)),
            _const_spec((d, d)),
        ],
        out_specs=pl.BlockSpec((t, d), lambda i: (i, 0)),
        out_shape=jax.ShapeDtypeStruct((n, d), _F32),
        scratch_shapes=[
            pltpu.VMEM((N_HEADS, KEY_SPAN, BLOCK), _F32),
            pltpu.VMEM((t, d), _BF16),
            pltpu.VMEM((2, t, Q_COLS), _BF16),
            pltpu.VMEM((3, t, N_KV_HEADS * 2 * HEAD_DIM), _BF16),
            pltpu.VMEM((3, KV_COLS, t), _BF16),
            pltpu.VMEM((2, KEY_SPAN, Q_PER_KV * BLOCK), _F32),
            pltpu.VMEM((2, KEY_SPAN, Q_PER_KV * BLOCK), _BF16),
            pltpu.VMEM((SUB_BLOCKS, Q_COLS, BLOCK), _BF16),
        ],
        compiler_params=_params(),
        name="window_attention",
    )(sink, rel_bias.reshape(-1), x2, x2, g.reshape(1, d), wqkv, buckets_t, w_out.astype(_BF16))


def kernel(x, norm_mix, norm_ffn, even_w_in, even_v_ln_g, even_v_ln_b, even_w_spatial, even_b_spatial,
           even_conv_w, even_w_out, attn_w_qkv, attn_sink, rel_bias, attn_w_out, ffn_w_gate, ffn_w_up,
           ffn_w_down, final_norm):
    bsz, seq_len, d = x.shape
    assert d == D_MODEL and seq_len % TOKEN_TILE == 0
    x2 = x.reshape(bsz * seq_len, d)
    wg, wu, wd = ffn_w_gate.astype(_BF16), ffn_w_up.astype(_BF16), ffn_w_down.astype(_BF16)
    x2 = _even_layer(x2, norm_mix[0], even_w_in[0], even_v_ln_g[0], even_v_ln_b[0], even_w_spatial[0],
                     even_b_spatial[0], even_conv_w[0], even_w_out[0], seq_len)
    x2 = _ffn_layer(x2, norm_ffn[0], wg, wu, wd, 0)
    x2 = _attn_layer(x2, norm_mix[1], attn_w_qkv[0], attn_sink[0], rel_bias, attn_w_out[0], seq_len)
    x2 = _ffn_layer(x2, norm_ffn[1], wg, wu, wd, 1, final_g=final_norm)
    return x2.reshape(bsz, seq_len, d)
```

```python
import functools
import math

import jax
import jax.numpy as jnp
from jax import lax
from jax.experimental import pallas as pl
from jax.experimental.pallas import tpu as pltpu

D_MODEL = 1024
EPS = 1e-6
A_WIDTH = 512
B_WIDTH = 512
A_GROUPS = 4
A_GROUP_DIM = 128
CHUNK = 128
CONV_W = 3
N_HEADS = 16
N_KV_HEADS = 4
Q_PER_KV = 4
HEAD_DIM = 64
WINDOW = 128
BLOCK = 128
N_BUCKETS = 32
MAX_DISTANCE = 128
D_FF = 2816

TOKEN_TILE = 512
FFN_TILE = 1024
HALO = 16
FF_CHUNK = 256
N_FF_CHUNKS = D_FF // FF_CHUNK
SUB_BLOCKS = TOKEN_TILE // BLOCK
KEY_SPAN = 3 * BLOCK
MASK_VALUE = -1e30
ROW_SLAB = 64
OUT_PIECE = 256
VMEM_LIMIT_BYTES = 56 * 1024 * 1024

_BF16 = jnp.bfloat16
_F32 = jnp.float32


def _dot(a, b):
    return jnp.dot(a, b, preferred_element_type=_F32)


def _dot_nt(a, b):
    return lax.dot_general(a, b, (((1,), (1,)), ((), ())), preferred_element_type=_F32)


def _dot_tn(a, b):
    return lax.dot_general(a, b, (((0,), (0,)), ((), ())), preferred_element_type=_F32)


def _gelu(x):
    return 0.5 * x * (1.0 + lax.erf(x * math.sqrt(0.5)))


def _rms(x, g):
    return (x * lax.rsqrt(jnp.mean(x * x, axis=-1, keepdims=True) + EPS)) * g


def _const_spec(shape):
    zeros = (0,) * len(shape)
    return pl.BlockSpec(shape, lambda i: zeros, pipeline_mode=pl.Buffered(1))


def _params():
    return pltpu.CompilerParams(
        dimension_semantics=("arbitrary",), vmem_limit_bytes=VMEM_LIMIT_BYTES)


def _even_kernel(tiles_per_seq, x_ref, xp_ref, xn_ref, g_ref, win_ref, lng_ref, lnb_ref,
                 wsp_ref, bsp_ref, cw_ref, wo_ref, o_ref, hs_ref, y_ref):
    t = TOKEN_TILE
    i = pl.program_id(0)
    pos = i % tiles_per_seq
    g = g_ref[...]
    x = x_ref[...]
    h = _rms(x, g).astype(_BF16)
    hp = jnp.where(pos == 0, 0.0, _rms(xp_ref[...], g)).astype(_BF16)
    hn = jnp.where(pos == tiles_per_seq - 1, 0.0, _rms(xn_ref[...], g)).astype(_BF16)
    hs_ref[0:HALO, :] = hp
    hs_ref[HALO:HALO + t, :] = h
    hs_ref[HALO + t:, :] = hn

    in_width_a = 2 * A_WIDTH + B_WIDTH
    pa = _dot(h, win_ref[:, :in_width_a])
    pc = _dot(hs_ref[...], win_ref[:, in_width_a:])

    u = _gelu(pa[:, :A_WIDTH])
    v = _gelu(pa[:, A_WIDTH:2 * A_WIDTH])
    mu = jnp.mean(v, axis=-1, keepdims=True)
    vc = v - mu
    vn = vc * lax.rsqrt(jnp.mean(vc * vc, axis=-1, keepdims=True) + EPS)
    vb = (vn * lng_ref[...] + lnb_ref[...]).astype(_BF16)
    for c in range(t // CHUNK):
        rows = slice(c * CHUNK, (c + 1) * CHUNK)
        for grp in range(A_GROUPS):
            cols = slice(grp * A_GROUP_DIM, (grp + 1) * A_GROUP_DIM)
            mixed = _dot(wsp_ref[grp], vb[rows, cols]) + bsp_ref[grp]
            y_ref[rows, cols] = (u[rows, cols] * mixed).astype(_BF16)

    z = pc[:, :B_WIDTH] * pc[:, B_WIDTH:]
    n_rows = t + 2 * HALO
    z_prev = pltpu.roll(z, 1, 0)[HALO:HALO + t]
    z_next = pltpu.roll(z, n_rows - 1, 0)[HALO:HALO + t]
    cw = cw_ref[...]
    conv = z_prev * cw[0:1] + z[HALO:HALO + t] * cw[1:2] + z_next * cw[2:3]
    y_ref[:, A_WIDTH:] = (pa[:, 2 * A_WIDTH:] * conv).astype(_BF16)

    o_ref[...] = x + _dot(y_ref[...], wo_ref[...])


def _even_layer(x2, g, w_in, ln_g, ln_b, w_sp, b_sp, conv_w, w_out, seq_len):
    n, d = x2.shape
    t = TOKEN_TILE
    halo_blocks_per_tile = t // HALO
    n_halo_blocks = n // HALO
    return pl.pallas_call(
        functools.partial(_even_kernel, seq_len // t),
        grid=(n // t,),
        in_specs=[
            pl.BlockSpec((t, d), lambda i: (i, 0)),
            pl.BlockSpec((HALO, d), lambda i: (jnp.maximum(i * halo_blocks_per_tile - 1, 0), 0)),
            pl.BlockSpec((HALO, d),
                         lambda i: (jnp.minimum((i + 1) * halo_blocks_per_tile, n_halo_blocks - 1), 0)),
            _const_spec((1, d)),
            _const_spec(w_in.shape),
            _const_spec((1, A_WIDTH)),
            _const_spec((1, A_WIDTH)),
            _const_spec((A_GROUPS, CHUNK, CHUNK)),
            _const_spec((A_GROUPS, CHUNK, 1)),
            _const_spec((CONV_W, B_WIDTH)),
            _const_spec((d, d)),
        ],
        out_specs=pl.BlockSpec((t, d), lambda i: (i, 0)),
        out_shape=jax.ShapeDtypeStruct((n, d), _F32),
        scratch_shapes=[
            pltpu.VMEM((t + 2 * HALO, d), _BF16),
            pltpu.VMEM((t, d), _BF16),
        ],
        compiler_params=_params(),
        name="even_mixer",
    )(x2, x2, x2, g.reshape(1, d), w_in.astype(_BF16), ln_g.reshape(1, -1), ln_b.reshape(1, -1),
      w_sp.astype(_BF16), b_sp[..., None], conv_w, w_out.astype(_BF16))


def _ffn_kernel(has_final_norm, x_ref, g_ref, wg_ref, wu_ref, wd_ref, *rest):
    if has_final_norm:
        gf_ref, o_ref, h_ref, acc_ref = rest
    else:
        o_ref, h_ref, acc_ref = rest
    x = x_ref[...]
    h_ref[...] = _rms(x, g_ref[...]).astype(_BF16)
    acc_ref[...] = x

    h = h_ref[...]
    for c in range(N_FF_CHUNKS):
        cols = slice(c * FF_CHUNK, (c + 1) * FF_CHUNK)
        gate = _dot(h, wg_ref[:, cols])
        up = _dot(h, wu_ref[:, cols])
        act = (jax.nn.silu(gate) * up).astype(_BF16)
        acc_ref[...] += _dot(act, wd_ref[cols, :])
    out = acc_ref[...]
    if has_final_norm:
        out = _rms(out, gf_ref[...])
    o_ref[...] = out


def _layer_spec(stacked_shape, layer):
    zeros = (0,) * (len(stacked_shape) - 1)
    return pl.BlockSpec((None,) + tuple(stacked_shape[1:]), lambda i: (layer,) + zeros,
                        pipeline_mode=pl.Buffered(1))


def _ffn_layer(x2, g, wg, wu, wd, layer, final_g=None):
    n, d = x2.shape
    t = FFN_TILE
    in_specs = [
        pl.BlockSpec((t, d), lambda i: (i, 0)),
        _const_spec((1, d)),
        _layer_spec(wg.shape, layer),
        _layer_spec(wu.shape, layer),
        _layer_spec(wd.shape, layer),
    ]
    args = [x2, g.reshape(1, d), wg, wu, wd]
    if final_g is not None:
        in_specs.append(_const_spec((1, d)))
        args.append(final_g.reshape(1, d))
    return pl.pallas_call(
        functools.partial(_ffn_kernel, final_g is not None),
        grid=(n // t,),
        in_specs=in_specs,
        out_specs=pl.BlockSpec((t, d), lambda i: (i, 0)),
        out_shape=jax.ShapeDtypeStruct((n, d), _F32),
        scratch_shapes=[pltpu.VMEM((t, d), _BF16), pltpu.VMEM((t, d), _F32)],
        compiler_params=_params(),
        name="swiglu_ffn",
    )(*args)


def _qkv_kernel(x_ref, g_ref, wqv_ref, wk_ref, qt_ref, k_ref, vt_ref):
    h = _rms(x_ref[...], g_ref[...]).astype(_BF16)
    qv = _dot_nt(wqv_ref[...], h)
    q_rows = N_HEADS * HEAD_DIM
    scale = HEAD_DIM ** -0.5
    for j in range(SUB_BLOCKS):
        lanes = slice(j * BLOCK, (j + 1) * BLOCK)
        qt_ref[j] = (qv[:q_rows, lanes] * scale).astype(_BF16)
        vt_ref[j] = qv[q_rows:, lanes].astype(_BF16)
    k_ref[...] = _dot(h, wk_ref[...]).astype(_BF16)


def _qkv_layer(x2, g, w_qkv):
    n, d = x2.shape
    t = TOKEN_TILE
    q_cols = N_HEADS * HEAD_DIM
    kv_cols = N_KV_HEADS * HEAD_DIM
    wq = w_qkv[:, :q_cols]
    wk = w_qkv[:, q_cols:q_cols + kv_cols]
    wv = w_qkv[:, q_cols + kv_cols:]
    wqv_t = jnp.concatenate([wq, wv], axis=1).T.astype(_BF16)
    n_blocks = n // BLOCK
    return pl.pallas_call(
        _qkv_kernel,
        grid=(n // t,),
        in_specs=[
            pl.BlockSpec((t, d), lambda i: (i, 0)),
            _const_spec((1, d)),
            _const_spec(wqv_t.shape),
            _const_spec((d, kv_cols)),
        ],
        out_specs=[
            pl.BlockSpec((SUB_BLOCKS, q_cols, BLOCK), lambda i: (i, 0, 0)),
            pl.BlockSpec((t, kv_cols), lambda i: (i, 0)),
            pl.BlockSpec((SUB_BLOCKS, kv_cols, BLOCK), lambda i: (i, 0, 0)),
        ],
        out_shape=[
            jax.ShapeDtypeStruct((n_blocks, q_cols, BLOCK), _BF16),
            jax.ShapeDtypeStruct((n, kv_cols), _BF16),
            jax.ShapeDtypeStruct((n_blocks, kv_cols, BLOCK), _BF16),
        ],
        compiler_params=_params(),
        name="qkv_proj",
    )(x2, g.reshape(1, d), wqv_t, wk.astype(_BF16))


def _attn_kernel(tiles_per_seq, sink_ref, rb_ref, x_ref, qt_ref, kc_ref, kp_ref, kn_ref,
                 vc_ref, vp_ref, vn_ref, bkt_ref, wo_ref, o_ref,
                 bias_ref, kall_ref, vall_ref, sc_ref, p_ref, at_ref):
    i = pl.program_id(0)
    pos = i % tiles_per_seq

    @pl.when(i == 0)
    def _init():
        bkt = bkt_ref[...]
        rel = (lax.broadcasted_iota(jnp.int32, (KEY_SPAN, BLOCK), 0) - BLOCK
               - lax.broadcasted_iota(jnp.int32, (KEY_SPAN, BLOCK), 1))
        band = jnp.abs(rel) <= WINDOW

        def fill_head(hd, carry):
            tbl = jnp.zeros((KEY_SPAN, BLOCK), _F32)
            for b in range(N_BUCKETS):
                tbl = jnp.where(bkt == b, rb_ref[b * N_HEADS + hd], tbl)
            bias_ref[hd] = jnp.where(band, tbl, MASK_VALUE)
            return carry

        lax.fori_loop(0, N_HEADS, fill_head, 0)

    kall_ref[0:BLOCK, :] = kp_ref[...]
    kall_ref[BLOCK:BLOCK + TOKEN_TILE, :] = kc_ref[...]
    kall_ref[BLOCK + TOKEN_TILE:, :] = kn_ref[...]
    vall_ref[0] = vp_ref[0]
    for j in range(SUB_BLOCKS):
        vall_ref[1 + j] = vc_ref[j]
    vall_ref[1 + SUB_BLOCKS] = vn_ref[0]

    pen_first = jnp.where(pos == 0, MASK_VALUE, 0.0)
    pen_last = jnp.where(pos == tiles_per_seq - 1, MASK_VALUE, 0.0)

    pair_lanes = 2 * HEAD_DIM
    slabs = [slice(r0, r0 + ROW_SLAB) for r0 in range(0, KEY_SPAN, ROW_SLAB)]
    heads = [(j, kvh) for j in range(SUB_BLOCKS) for kvh in range(N_KV_HEADS)]

    def scores(t):
        j, kvh = heads[t]
        pair, half = kvh // 2, kvh % 2
        q_heads = jnp.concatenate(
            [qt_ref[j, (kvh * Q_PER_KV + grp) * HEAD_DIM:(kvh * Q_PER_KV + grp + 1) * HEAD_DIM, :]
             for grp in range(Q_PER_KV)], axis=1)
        q_zero = jnp.zeros_like(q_heads)
        q2 = jnp.concatenate([q_heads, q_zero] if half == 0 else [q_zero, q_heads], axis=0)
        kwin = kall_ref[j * BLOCK:j * BLOCK + KEY_SPAN, pair * pair_lanes:(pair + 1) * pair_lanes]
        raw = _dot(kwin, q2)
        sc_buf = sc_ref.at[t % 2]
        maxes = []
        for grp in range(Q_PER_KV):
            hd = kvh * Q_PER_KV + grp
            lanes = slice(grp * BLOCK, (grp + 1) * BLOCK)
            mx = None
            for rows in slabs:
                s = raw[rows, lanes] + bias_ref[hd, rows, :]
                if j == 0 and rows.stop <= BLOCK:
                    s = s + pen_first
                if j == SUB_BLOCKS - 1 and rows.start >= 2 * BLOCK:
                    s = s + pen_last
                sc_buf[rows, lanes] = s
                mx = s if mx is None else jnp.maximum(mx, s)
            maxes.append(jnp.maximum(jnp.max(mx, axis=0, keepdims=True), sink_ref[hd]))
        return maxes

    def softmax(t, maxes):
        j, kvh = heads[t]
        sc_buf = sc_ref.at[t % 2]
        p_buf = p_ref.at[t % 2]
        inv_l = []
        for grp in range(Q_PER_KV):
            hd = kvh * Q_PER_KV + grp
            lanes = slice(grp * BLOCK, (grp + 1) * BLOCK)
            m = maxes[grp]
            acc = None
            for rows in slabs:
                p = jnp.exp(sc_buf[rows, lanes] - m)
                acc = p if acc is None else acc + p
                p_buf[rows, lanes] = p.astype(_BF16)
            l = jnp.sum(acc, axis=0, keepdims=True) + jnp.exp(sink_ref[hd] - m)
            inv_l.append(1.0 / l)
        return inv_l

    def weighted_values(t, inv_l):
        j, kvh = heads[t]
        vwin = jnp.concatenate(
            [vall_ref[j + b, kvh * HEAD_DIM:(kvh + 1) * HEAD_DIM, :] for b in range(3)], axis=1)
        ot = _dot(vwin, p_ref[t % 2])
        for grp in range(Q_PER_KV):
            hd = kvh * Q_PER_KV + grp
            at_ref[j, hd * HEAD_DIM:(hd + 1) * HEAD_DIM, :] = (
                ot[:, grp * BLOCK:(grp + 1) * BLOCK] * inv_l[grp]).astype(_BF16)

    def out_proj_piece(j, piece):
        rows = slice(j * BLOCK, (j + 1) * BLOCK)
        cols = slice(piece * OUT_PIECE, (piece + 1) * OUT_PIECE)
        o_ref[rows, cols] = x_ref[rows, cols] + _dot_tn(at_ref[j], wo_ref[:, cols])

    pieces_per_head = (D_MODEL // OUT_PIECE) // N_KV_HEADS
    maxes = {0: scores(0), 1: scores(1)}
    for t, (j, kvh) in enumerate(heads):
        inv_l = softmax(t, maxes.pop(t))
        weighted_values(t, inv_l)
        if j > 0:
            for piece in range(kvh * pieces_per_head, (kvh + 1) * pieces_per_head):
                out_proj_piece(j - 1, piece)
        if t + 2 < len(heads):
            maxes[t + 2] = scores(t + 2)
    for piece in range(D_MODEL // OUT_PIECE):
        out_proj_piece(SUB_BLOCKS - 1, piece)


def _t5_buckets(rel):
    nb = N_BUCKETS // 2
    ret = jnp.where(rel > 0, nb, 0)
    n = jnp.abs(rel)
    max_exact = nb // 2
    nf = jnp.maximum(n, 1).astype(jnp.float32)
    large = max_exact + (jnp.log(nf / max_exact) / math.log(MAX_DISTANCE / max_exact)
                         * (nb - max_exact)).astype(jnp.int32)
    large = jnp.minimum(large, nb - 1)
    return ret + jnp.where(n < max_exact, n, large)


def _attn_layer(x2, qt, k, vt, sink, rel_bias, w_out, seq_len):
    n, d = x2.shape
    t = TOKEN_TILE
    n_blocks = n // BLOCK
    kv_cols = N_KV_HEADS * HEAD_DIM
    q_cols = N_HEADS * HEAD_DIM
    kj = jnp.arange(KEY_SPAN, dtype=jnp.int32)[:, None]
    qi = jnp.arange(BLOCK, dtype=jnp.int32)[None, :]
    buckets_t = _t5_buckets(kj - BLOCK - qi).astype(jnp.int32)

    def prev_block(i):
        return jnp.maximum(i * SUB_BLOCKS - 1, 0)

    def next_block(i):
        return jnp.minimum((i + 1) * SUB_BLOCKS, n_blocks - 1)

    smem = pl.BlockSpec(memory_space=pltpu.SMEM)
    return pl.pallas_call(
        functools.partial(_attn_kernel, seq_len // t),
        grid=(n // t,),
        in_specs=[
            smem,
            smem,
            pl.BlockSpec((t, d), lambda i: (i, 0)),
            pl.BlockSpec((SUB_BLOCKS, q_cols, BLOCK), lambda i: (i, 0, 0)),
            pl.BlockSpec((t, kv_cols), lambda i: (i, 0)),
            pl.BlockSpec((BLOCK, kv_cols), lambda i: (prev_block(i), 0)),
            pl.BlockSpec((BLOCK, kv_cols), lambda i: (next_block(i), 0)),
            pl.BlockSpec((SUB_BLOCKS, kv_cols, BLOCK), lambda i: (i, 0, 0)),
            pl.BlockSpec((1, kv_cols, BLOCK), lambda i: (prev_block(i), 0, 0)),
            pl.BlockSpec((1, kv_cols, BLOCK), lambda i: (next_block(i), 0, 0)),
            _const_spec((KEY_SPAN, BLOCK)),
            _const_spec((d, d)),
        ],
        out_specs=pl.BlockSpec((t, d), lambda i: (i, 0)),
        out_shape=jax.ShapeDtypeStruct((n, d), _F32),
        scratch_shapes=[
            pltpu.VMEM((N_HEADS, KEY_SPAN, BLOCK), _F32),
            pltpu.VMEM((TOKEN_TILE + 2 * BLOCK, kv_cols), _BF16),
            pltpu.VMEM((SUB_BLOCKS + 2, kv_cols, BLOCK), _BF16),
            pltpu.VMEM((2, KEY_SPAN, Q_PER_KV * BLOCK), _F32),
            pltpu.VMEM((2, KEY_SPAN, Q_PER_KV * BLOCK), _BF16),
            pltpu.VMEM((SUB_BLOCKS, q_cols, BLOCK), _BF16),
        ],
        compiler_params=_params(),
        name="window_attention",
    )(sink, rel_bias.reshape(-1), x2, qt, k, k, k, vt, vt, vt, buckets_t, w_out.astype(_BF16))


def kernel(x, norm_mix, norm_ffn, even_w_in, even_v_ln_g, even_v_ln_b, even_w_spatial, even_b_spatial,
           even_conv_w, even_w_out, attn_w_qkv, attn_sink, rel_bias, attn_w_out, ffn_w_gate, ffn_w_up,
           ffn_w_down, final_norm):
    bsz, seq_len, d = x.shape
    assert d == D_MODEL and seq_len % TOKEN_TILE == 0 and (bsz * seq_len) % FFN_TILE == 0
    x2 = x.reshape(bsz * seq_len, d)
    wg, wu, wd = ffn_w_gate.astype(_BF16), ffn_w_up.astype(_BF16), ffn_w_down.astype(_BF16)
    x2 = _even_layer(x2, norm_mix[0], even_w_in[0], even_v_ln_g[0], even_v_ln_b[0], even_w_spatial[0],
                     even_b_spatial[0], even_conv_w[0], even_w_out[0], seq_len)
    x2 = _ffn_layer(x2, norm_ffn[0], wg, wu, wd, 0)
    qt, k, vt = _qkv_layer(x2, norm_mix[1], attn_w_qkv[0])
    x2 = _attn_layer(x2, qt, k, vt, attn_sink[0], rel_bias, attn_w_out[0], seq_len)
    x2 = _ffn_layer(x2, norm_ffn[1], wg, wu, wd, 1, final_g=final_norm)
    return x2.reshape(bsz, seq_len, d)
```

```python
import functools
import math

import jax
import jax.numpy as jnp
from jax import lax
from jax.experimental import pallas as pl
from jax.experimental.pallas import tpu as pltpu

D_MODEL = 1024
EPS = 1e-6
A_WIDTH = 512
B_WIDTH = 512
A_GROUPS = 4
A_GROUP_DIM = 128
CHUNK = 128
CONV_W = 3
N_HEADS = 16
N_KV_HEADS = 4
Q_PER_KV = 4
HEAD_DIM = 64
WINDOW = 128
BLOCK = 128
N_BUCKETS = 32
MAX_DISTANCE = 128
D_FF = 2816

TOKEN_TILE = 512
WIDE_TILE = 1024
HALO = 16
FF_CHUNK = 256
N_FF_CHUNKS = D_FF // FF_CHUNK
SUB_BLOCKS = TOKEN_TILE // BLOCK
KEY_SPAN = 3 * BLOCK
MASK_VALUE = -1e30
ROW_SLAB = 64
OUT_PIECE = 256
VMEM_LIMIT_BYTES = 56 * 1024 * 1024

_BF16 = jnp.bfloat16
_F32 = jnp.float32


def _dot(a, b):
    return jnp.dot(a, b, preferred_element_type=_F32)


def _dot_nt(a, b):
    return lax.dot_general(a, b, (((1,), (1,)), ((), ())), preferred_element_type=_F32)


def _dot_tn(a, b):
    return lax.dot_general(a, b, (((0,), (0,)), ((), ())), preferred_element_type=_F32)


def _gelu(x):
    return 0.5 * x * (1.0 + lax.erf(x * math.sqrt(0.5)))


def _rms(x, g):
    return (x * lax.rsqrt(jnp.mean(x * x, axis=-1, keepdims=True) + EPS)) * g


def _const_spec(shape):
    zeros = (0,) * len(shape)
    return pl.BlockSpec(shape, lambda i: zeros, pipeline_mode=pl.Buffered(1))


def _params():
    return pltpu.CompilerParams(
        dimension_semantics=("arbitrary",), vmem_limit_bytes=VMEM_LIMIT_BYTES)


def _even_kernel(tiles_per_seq, x_ref, xp_ref, xn_ref, g_ref, win_ref, lng_ref, lnb_ref,
                 wsp_ref, bsp_ref, cw_ref, wo_ref, o_ref, hs_ref, y_ref):
    t = WIDE_TILE
    i = pl.program_id(0)
    pos = i % tiles_per_seq
    g = g_ref[...]
    x = x_ref[...]
    h = _rms(x, g).astype(_BF16)
    hp = jnp.where(pos == 0, 0.0, _rms(xp_ref[...], g)).astype(_BF16)
    hn = jnp.where(pos == tiles_per_seq - 1, 0.0, _rms(xn_ref[...], g)).astype(_BF16)
    hs_ref[0:HALO, :] = hp
    hs_ref[HALO:HALO + t, :] = h
    hs_ref[HALO + t:, :] = hn

    in_width_a = 2 * A_WIDTH + B_WIDTH
    pa = _dot(h, win_ref[:, :in_width_a])
    pc = _dot(hs_ref[...], win_ref[:, in_width_a:])

    u = _gelu(pa[:, :A_WIDTH])
    v = _gelu(pa[:, A_WIDTH:2 * A_WIDTH])
    mu = jnp.mean(v, axis=-1, keepdims=True)
    vc = v - mu
    vn = vc * lax.rsqrt(jnp.mean(vc * vc, axis=-1, keepdims=True) + EPS)
    vb = (vn * lng_ref[...] + lnb_ref[...]).astype(_BF16)
    for c in range(t // CHUNK):
        rows = slice(c * CHUNK, (c + 1) * CHUNK)
        for grp in range(A_GROUPS):
            cols = slice(grp * A_GROUP_DIM, (grp + 1) * A_GROUP_DIM)
            mixed = _dot(wsp_ref[grp], vb[rows, cols]) + bsp_ref[grp]
            y_ref[rows, cols] = (u[rows, cols] * mixed).astype(_BF16)

    z = pc[:, :B_WIDTH] * pc[:, B_WIDTH:]
    n_rows = t + 2 * HALO
    z_prev = pltpu.roll(z, 1, 0)[HALO:HALO + t]
    z_next = pltpu.roll(z, n_rows - 1, 0)[HALO:HALO + t]
    cw = cw_ref[...]
    conv = z_prev * cw[0:1] + z[HALO:HALO + t] * cw[1:2] + z_next * cw[2:3]
    y_ref[:, A_WIDTH:] = (pa[:, 2 * A_WIDTH:] * conv).astype(_BF16)

    o_ref[...] = x + _dot(y_ref[...], wo_ref[...])


def _even_layer(x2, g, w_in, ln_g, ln_b, w_sp, b_sp, conv_w, w_out, seq_len):
    n, d = x2.shape
    t = WIDE_TILE
    halo_blocks_per_tile = t // HALO
    n_halo_blocks = n // HALO
    return pl.pallas_call(
        functools.partial(_even_kernel, seq_len // t),
        grid=(n // t,),
        in_specs=[
            pl.BlockSpec((t, d), lambda i: (i, 0)),
            pl.BlockSpec((HALO, d), lambda i: (jnp.maximum(i * halo_blocks_per_tile - 1, 0), 0)),
            pl.BlockSpec((HALO, d),
                         lambda i: (jnp.minimum((i + 1) * halo_blocks_per_tile, n_halo_blocks - 1), 0)),
            _const_spec((1, d)),
            _const_spec(w_in.shape),
            _const_spec((1, A_WIDTH)),
            _const_spec((1, A_WIDTH)),
            _const_spec((A_GROUPS, CHUNK, CHUNK)),
            _const_spec((A_GROUPS, CHUNK, 1)),
            _const_spec((CONV_W, B_WIDTH)),
            _const_spec((d, d)),
        ],
        out_specs=pl.BlockSpec((t, d), lambda i: (i, 0)),
        out_shape=jax.ShapeDtypeStruct((n, d), _F32),
        scratch_shapes=[
            pltpu.VMEM((t + 2 * HALO, d), _BF16),
            pltpu.VMEM((t, d), _BF16),
        ],
        compiler_params=_params(),
        name="even_mixer",
    )(x2, x2, x2, g.reshape(1, d), w_in.astype(_BF16), ln_g.reshape(1, -1), ln_b.reshape(1, -1),
      w_sp.astype(_BF16), b_sp[..., None], conv_w, w_out.astype(_BF16))


def _ffn_kernel(has_final_norm, x_ref, g_ref, wg_ref, wu_ref, wd_ref, *rest):
    if has_final_norm:
        gf_ref, o_ref, h_ref, acc_ref = rest
    else:
        o_ref, h_ref, acc_ref = rest
    x = x_ref[...]
    h_ref[...] = _rms(x, g_ref[...]).astype(_BF16)
    acc_ref[...] = x

    h = h_ref[...]
    for c in range(N_FF_CHUNKS):
        cols = slice(c * FF_CHUNK, (c + 1) * FF_CHUNK)
        gate = _dot(h, wg_ref[:, cols])
        up = _dot(h, wu_ref[:, cols])
        act = (jax.nn.silu(gate) * up).astype(_BF16)
        acc_ref[...] += _dot(act, wd_ref[cols, :])
    out = acc_ref[...]
    if has_final_norm:
        out = _rms(out, gf_ref[...])
    o_ref[...] = out


def _layer_spec(stacked_shape, layer):
    zeros = (0,) * (len(stacked_shape) - 1)
    return pl.BlockSpec((None,) + tuple(stacked_shape[1:]), lambda i: (layer,) + zeros,
                        pipeline_mode=pl.Buffered(1))


def _ffn_layer(x2, g, wg, wu, wd, layer, final_g=None):
    n, d = x2.shape
    t = WIDE_TILE
    in_specs = [
        pl.BlockSpec((t, d), lambda i: (i, 0)),
        _const_spec((1, d)),
        _layer_spec(wg.shape, layer),
        _layer_spec(wu.shape, layer),
        _layer_spec(wd.shape, layer),
    ]
    args = [x2, g.reshape(1, d), wg, wu, wd]
    if final_g is not None:
        in_specs.append(_const_spec((1, d)))
        args.append(final_g.reshape(1, d))
    return pl.pallas_call(
        functools.partial(_ffn_kernel, final_g is not None),
        grid=(n // t,),
        in_specs=in_specs,
        out_specs=pl.BlockSpec((t, d), lambda i: (i, 0)),
        out_shape=jax.ShapeDtypeStruct((n, d), _F32),
        scratch_shapes=[pltpu.VMEM((t, d), _BF16), pltpu.VMEM((t, d), _F32)],
        compiler_params=_params(),
        name="swiglu_ffn",
    )(*args)


def _qkv_kernel(x_ref, g_ref, wqv_ref, wk_ref, qt_ref, k_ref, vt_ref):
    h = _rms(x_ref[...], g_ref[...]).astype(_BF16)
    qv = _dot_nt(wqv_ref[...], h)
    q_rows = N_HEADS * HEAD_DIM
    scale = HEAD_DIM ** -0.5
    for j in range(WIDE_TILE // BLOCK):
        lanes = slice(j * BLOCK, (j + 1) * BLOCK)
        qt_ref[j] = (qv[:q_rows, lanes] * scale).astype(_BF16)
        vt_ref[j] = qv[q_rows:, lanes].astype(_BF16)
    k_ref[...] = _dot(h, wk_ref[...]).astype(_BF16)


def _qkv_layer(x2, g, w_qkv):
    n, d = x2.shape
    t = WIDE_TILE
    q_cols = N_HEADS * HEAD_DIM
    kv_cols = N_KV_HEADS * HEAD_DIM
    wq = w_qkv[:, :q_cols]
    wk = w_qkv[:, q_cols:q_cols + kv_cols]
    wv = w_qkv[:, q_cols + kv_cols:]
    wqv_t = jnp.concatenate([wq, wv], axis=1).T.astype(_BF16)
    n_blocks = n // BLOCK
    return pl.pallas_call(
        _qkv_kernel,
        grid=(n // t,),
        in_specs=[
            pl.BlockSpec((t, d), lambda i: (i, 0)),
            _const_spec((1, d)),
            _const_spec(wqv_t.shape),
            _const_spec((d, kv_cols)),
        ],
        out_specs=[
            pl.BlockSpec((t // BLOCK, q_cols, BLOCK), lambda i: (i, 0, 0)),
            pl.BlockSpec((t, kv_cols), lambda i: (i, 0)),
            pl.BlockSpec((t // BLOCK, kv_cols, BLOCK), lambda i: (i, 0, 0)),
        ],
        out_shape=[
            jax.ShapeDtypeStruct((n_blocks, q_cols, BLOCK), _BF16),
            jax.ShapeDtypeStruct((n, kv_cols), _BF16),
            jax.ShapeDtypeStruct((n_blocks, kv_cols, BLOCK), _BF16),
        ],
        compiler_params=_params(),
        name="qkv_proj",
    )(x2, g.reshape(1, d), wqv_t, wk.astype(_BF16))


def _attn_kernel(tiles_per_seq, sink_ref, rb_ref, x_ref, qt_ref, kc_ref, kp_ref, kn_ref,
                 vc_ref, vp_ref, vn_ref, bkt_ref, wo_ref, o_ref,
                 bias_ref, kall_ref, vall_ref, sc_ref, p_ref, at_ref):
    i = pl.program_id(0)
    pos = i % tiles_per_seq

    @pl.when(i == 0)
    def _init():
        bkt = bkt_ref[...]
        rel = (lax.broadcasted_iota(jnp.int32, (KEY_SPAN, BLOCK), 0) - BLOCK
               - lax.broadcasted_iota(jnp.int32, (KEY_SPAN, BLOCK), 1))
        band = jnp.abs(rel) <= WINDOW

        def fill_head(hd, carry):
            tbl = jnp.zeros((KEY_SPAN, BLOCK), _F32)
            for b in range(N_BUCKETS):
                tbl = jnp.where(bkt == b, rb_ref[b * N_HEADS + hd], tbl)
            bias_ref[hd] = jnp.where(band, tbl, MASK_VALUE)
            return carry

        lax.fori_loop(0, N_HEADS, fill_head, 0)

    kall_ref[0:BLOCK, :] = kp_ref[...]
    kall_ref[BLOCK:BLOCK + TOKEN_TILE, :] = kc_ref[...]
    kall_ref[BLOCK + TOKEN_TILE:, :] = kn_ref[...]
    vall_ref[0] = vp_ref[0]
    for j in range(SUB_BLOCKS):
        vall_ref[1 + j] = vc_ref[j]
    vall_ref[1 + SUB_BLOCKS] = vn_ref[0]

    pen_first = jnp.where(pos == 0, MASK_VALUE, 0.0)
    pen_last = jnp.where(pos == tiles_per_seq - 1, MASK_VALUE, 0.0)

    pair_lanes = 2 * HEAD_DIM
    slabs = [slice(r0, r0 + ROW_SLAB) for r0 in range(0, KEY_SPAN, ROW_SLAB)]
    heads = [(j, kvh) for j in range(SUB_BLOCKS) for kvh in range(N_KV_HEADS)]

    def scores(t):
        j, kvh = heads[t]
        pair, half = kvh // 2, kvh % 2
        q_heads = jnp.concatenate(
            [qt_ref[j, (kvh * Q_PER_KV + grp) * HEAD_DIM:(kvh * Q_PER_KV + grp + 1) * HEAD_DIM, :]
             for grp in range(Q_PER_KV)], axis=1)
        q_zero = jnp.zeros_like(q_heads)
        q2 = jnp.concatenate([q_heads, q_zero] if half == 0 else [q_zero, q_heads], axis=0)
        kwin = kall_ref[j * BLOCK:j * BLOCK + KEY_SPAN, pair * pair_lanes:(pair + 1) * pair_lanes]
        raw = _dot(kwin, q2)
        sc_buf = sc_ref.at[t % 2]
        maxes = []
        for grp in range(Q_PER_KV):
            hd = kvh * Q_PER_KV + grp
            lanes = slice(grp * BLOCK, (grp + 1) * BLOCK)
            mx = None
            for rows in slabs:
                s = raw[rows, lanes] + bias_ref[hd, rows, :]
                if j == 0 and rows.stop <= BLOCK:
                    s = s + pen_first
                if j == SUB_BLOCKS - 1 and rows.start >= 2 * BLOCK:
                    s = s + pen_last
                sc_buf[rows, lanes] = s
                mx = s if mx is None else jnp.maximum(mx, s)
            maxes.append(jnp.maximum(jnp.max(mx, axis=0, keepdims=True), sink_ref[hd]))
        return maxes

    def softmax(t, maxes):
        j, kvh = heads[t]
        sc_buf = sc_ref.at[t % 2]
        p_buf = p_ref.at[t % 2]
        inv_l = []
        for grp in range(Q_PER_KV):
            hd = kvh * Q_PER_KV + grp
            lanes = slice(grp * BLOCK, (grp + 1) * BLOCK)
            m = maxes[grp]
            acc = None
            for rows in slabs:
                p = jnp.exp(sc_buf[rows, lanes] - m)
                acc = p if acc is None else acc + p
                p_buf[rows, lanes] = p.astype(_BF16)
            l = jnp.sum(acc, axis=0, keepdims=True) + jnp.exp(sink_ref[hd] - m)
            inv_l.append(1.0 / l)
        return inv_l

    def weighted_values(t, inv_l):
        j, kvh = heads[t]
        vwin = jnp.concatenate(
            [vall_ref[j + b, kvh * HEAD_DIM:(kvh + 1) * HEAD_DIM, :] for b in range(3)], axis=1)
        ot = _dot(vwin, p_ref[t % 2])
        for grp in range(Q_PER_KV):
            hd = kvh * Q_PER_KV + grp
            at_ref[j, hd * HEAD_DIM:(hd + 1) * HEAD_DIM, :] = (
                ot[:, grp * BLOCK:(grp + 1) * BLOCK] * inv_l[grp]).astype(_BF16)

    def out_proj_piece(j, piece):
        rows = slice(j * BLOCK, (j + 1) * BLOCK)
        cols = slice(piece * OUT_PIECE, (piece + 1) * OUT_PIECE)
        o_ref[rows, cols] = x_ref[rows, cols] + _dot_tn(at_ref[j], wo_ref[:, cols])

    pieces_per_head = (D_MODEL // OUT_PIECE) // N_KV_HEADS
    maxes = {0: scores(0), 1: scores(1)}
    for t, (j, kvh) in enumerate(heads):
        inv_l = softmax(t, maxes.pop(t))
        weighted_values(t, inv_l)
        if j > 0:
            for piece in range(kvh * pieces_per_head, (kvh + 1) * pieces_per_head):
                out_proj_piece(j - 1, piece)
        if t + 2 < len(heads):
            maxes[t + 2] = scores(t + 2)
    for piece in range(D_MODEL // OUT_PIECE):
        out_proj_piece(SUB_BLOCKS - 1, piece)


def _t5_buckets(rel):
    nb = N_BUCKETS // 2
    ret = jnp.where(rel > 0, nb, 0)
    n = jnp.abs(rel)
    max_exact = nb // 2
    nf = jnp.maximum(n, 1).astype(jnp.float32)
    large = max_exact + (jnp.log(nf / max_exact) / math.log(MAX_DISTANCE / max_exact)
                         * (nb - max_exact)).astype(jnp.int32)
    large = jnp.minimum(large, nb - 1)
    return ret + jnp.where(n < max_exact, n, large)


def _attn_layer(x2, qt, k, vt, sink, rel_bias, w_out, seq_len):
    n, d = x2.shape
    t = TOKEN_TILE
    n_blocks = n // BLOCK
    kv_cols = N_KV_HEADS * HEAD_DIM
    q_cols = N_HEADS * HEAD_DIM
    kj = jnp.arange(KEY_SPAN, dtype=jnp.int32)[:, None]
    qi = jnp.arange(BLOCK, dtype=jnp.int32)[None, :]
    buckets_t = _t5_buckets(kj - BLOCK - qi).astype(jnp.int32)

    def prev_block(i):
        return jnp.maximum(i * SUB_BLOCKS - 1, 0)

    def next_block(i):
        return jnp.minimum((i + 1) * SUB_BLOCKS, n_blocks - 1)

    smem = pl.BlockSpec(memory_space=pltpu.SMEM)
    return pl.pallas_call(
        functools.partial(_attn_kernel, seq_len // t),
        grid=(n // t,),
        in_specs=[
            smem,
            smem,
            pl.BlockSpec((t, d), lambda i: (i, 0)),
            pl.BlockSpec((SUB_BLOCKS, q_cols, BLOCK), lambda i: (i, 0, 0)),
            pl.BlockSpec((t, kv_cols), lambda i: (i, 0)),
            pl.BlockSpec((BLOCK, kv_cols), lambda i: (prev_block(i), 0)),
            pl.BlockSpec((BLOCK, kv_cols), lambda i: (next_block(i), 0)),
            pl.BlockSpec((SUB_BLOCKS, kv_cols, BLOCK), lambda i: (i, 0, 0)),
            pl.BlockSpec((1, kv_cols, BLOCK), lambda i: (prev_block(i), 0, 0)),
            pl.BlockSpec((1, kv_cols, BLOCK), lambda i: (next_block(i), 0, 0)),
            _const_spec((KEY_SPAN, BLOCK)),
            _const_spec((d, d)),
        ],
        out_specs=pl.BlockSpec((t, d), lambda i: (i, 0)),
        out_shape=jax.ShapeDtypeStruct((n, d), _F32),
        scratch_shapes=[
            pltpu.VMEM((N_HEADS, KEY_SPAN, BLOCK), _F32),
            pltpu.VMEM((TOKEN_TILE + 2 * BLOCK, kv_cols), _BF16),
            pltpu.VMEM((SUB_BLOCKS + 2, kv_cols, BLOCK), _BF16),
            pltpu.VMEM((2, KEY_SPAN, Q_PER_KV * BLOCK), _F32),
            pltpu.VMEM((2, KEY_SPAN, Q_PER_KV * BLOCK), _BF16),
            pltpu.VMEM((SUB_BLOCKS, q_cols, BLOCK), _BF16),
        ],
        compiler_params=_params(),
        name="window_attention",
    )(sink, rel_bias.reshape(-1), x2, qt, k, k, k, vt, vt, vt, buckets_t, w_out.astype(_BF16))


def kernel(x, norm_mix, norm_ffn, even_w_in, even_v_ln_g, even_v_ln_b, even_w_spatial, even_b_spatial,
           even_conv_w, even_w_out, attn_w_qkv, attn_sink, rel_bias, attn_w_out, ffn_w_gate, ffn_w_up,
           ffn_w_down, final_norm):
    bsz, seq_len, d = x.shape
    assert d == D_MODEL and seq_len % WIDE_TILE == 0
    x2 = x.reshape(bsz * seq_len, d)
    wg, wu, wd = ffn_w_gate.astype(_BF16), ffn_w_up.astype(_BF16), ffn_w_down.astype(_BF16)
    x2 = _even_layer(x2, norm_mix[0], even_w_in[0], even_v_ln_g[0], even_v_ln_b[0], even_w_spatial[0],
                     even_b_spatial[0], even_conv_w[0], even_w_out[0], seq_len)
    x2 = _ffn_layer(x2, norm_ffn[0], wg, wu, wd, 0)
    qt, k, vt = _qkv_layer(x2, norm_mix[1], attn_w_qkv[0])
    x2 = _attn_layer(x2, qt, k, vt, attn_sink[0], rel_bias, attn_w_out[0], seq_len)
    x2 = _ffn_layer(x2, norm_ffn[1], wg, wu, wd, 1, final_g=final_norm)
    return x2.reshape(bsz, seq_len, d)
```

```python
import functools
import math

import jax
import jax.numpy as jnp
from jax import lax
from jax.experimental import pallas as pl
from jax.experimental.pallas import tpu as pltpu

D_MODEL = 1024
EPS = 1e-6
A_WIDTH = 512
B_WIDTH = 512
A_GROUPS = 4
A_GROUP_DIM = 128
CHUNK = 128
CONV_W = 3
N_HEADS = 16
N_KV_HEADS = 4
Q_PER_KV = 4
HEAD_DIM = 64
WINDOW = 128
BLOCK = 128
N_BUCKETS = 32
MAX_DISTANCE = 128
D_FF = 2816

TOKEN_TILE = 1024
WIDE_TILE = 1024
HALO = 16
FF_CHUNK = 256
N_FF_CHUNKS = D_FF // FF_CHUNK
SUB_BLOCKS = TOKEN_TILE // BLOCK
KEY_SPAN = 3 * BLOCK
MASK_VALUE = -1e30
ROW_SLAB = 64
OUT_PIECE = 256
VMEM_LIMIT_BYTES = 56 * 1024 * 1024

_BF16 = jnp.bfloat16
_F32 = jnp.float32


def _dot(a, b):
    return jnp.dot(a, b, preferred_element_type=_F32)


def _dot_nt(a, b):
    return lax.dot_general(a, b, (((1,), (1,)), ((), ())), preferred_element_type=_F32)


def _dot_tn(a, b):
    return lax.dot_general(a, b, (((0,), (0,)), ((), ())), preferred_element_type=_F32)


def _gelu(x):
    return 0.5 * x * (1.0 + lax.erf(x * math.sqrt(0.5)))


def _rms(x, g):
    return (x * lax.rsqrt(jnp.mean(x * x, axis=-1, keepdims=True) + EPS)) * g


def _const_spec(shape):
    zeros = (0,) * len(shape)
    return pl.BlockSpec(shape, lambda i: zeros, pipeline_mode=pl.Buffered(1))


def _params():
    return pltpu.CompilerParams(
        dimension_semantics=("arbitrary",), vmem_limit_bytes=VMEM_LIMIT_BYTES)


def _even_kernel(tiles_per_seq, x_ref, xp_ref, xn_ref, g_ref, win_ref, lng_ref, lnb_ref,
                 wsp_ref, bsp_ref, cw_ref, wo_ref, o_ref, hs_ref, y_ref):
    t = WIDE_TILE
    i = pl.program_id(0)
    pos = i % tiles_per_seq
    g = g_ref[...]
    x = x_ref[...]
    h = _rms(x, g).astype(_BF16)
    hp = jnp.where(pos == 0, 0.0, _rms(xp_ref[...], g)).astype(_BF16)
    hn = jnp.where(pos == tiles_per_seq - 1, 0.0, _rms(xn_ref[...], g)).astype(_BF16)
    hs_ref[0:HALO, :] = hp
    hs_ref[HALO:HALO + t, :] = h
    hs_ref[HALO + t:, :] = hn

    in_width_a = 2 * A_WIDTH + B_WIDTH
    pa = _dot(h, win_ref[:, :in_width_a])
    pc = _dot(hs_ref[...], win_ref[:, in_width_a:])

    u = _gelu(pa[:, :A_WIDTH])
    v = _gelu(pa[:, A_WIDTH:2 * A_WIDTH])
    mu = jnp.mean(v, axis=-1, keepdims=True)
    vc = v - mu
    vn = vc * lax.rsqrt(jnp.mean(vc * vc, axis=-1, keepdims=True) + EPS)
    vb = (vn * lng_ref[...] + lnb_ref[...]).astype(_BF16)
    for c in range(t // CHUNK):
        rows = slice(c * CHUNK, (c + 1) * CHUNK)
        for grp in range(A_GROUPS):
            cols = slice(grp * A_GROUP_DIM, (grp + 1) * A_GROUP_DIM)
            mixed = _dot(wsp_ref[grp], vb[rows, cols]) + bsp_ref[grp]
            y_ref[rows, cols] = (u[rows, cols] * mixed).astype(_BF16)

    z = pc[:, :B_WIDTH] * pc[:, B_WIDTH:]
    n_rows = t + 2 * HALO
    z_prev = pltpu.roll(z, 1, 0)[HALO:HALO + t]
    z_next = pltpu.roll(z, n_rows - 1, 0)[HALO:HALO + t]
    cw = cw_ref[...]
    conv = z_prev * cw[0:1] + z[HALO:HALO + t] * cw[1:2] + z_next * cw[2:3]
    y_ref[:, A_WIDTH:] = (pa[:, 2 * A_WIDTH:] * conv).astype(_BF16)

    o_ref[...] = x + _dot(y_ref[...], wo_ref[...])


def _even_layer(x2, g, w_in, ln_g, ln_b, w_sp, b_sp, conv_w, w_out, seq_len):
    n, d = x2.shape
    t = WIDE_TILE
    halo_blocks_per_tile = t // HALO
    n_halo_blocks = n // HALO
    return pl.pallas_call(
        functools.partial(_even_kernel, seq_len // t),
        grid=(n // t,),
        in_specs=[
            pl.BlockSpec((t, d), lambda i: (i, 0)),
            pl.BlockSpec((HALO, d), lambda i: (jnp.maximum(i * halo_blocks_per_tile - 1, 0), 0)),
            pl.BlockSpec((HALO, d),
                         lambda i: (jnp.minimum((i + 1) * halo_blocks_per_tile, n_halo_blocks - 1), 0)),
            _const_spec((1, d)),
            _const_spec(w_in.shape),
            _const_spec((1, A_WIDTH)),
            _const_spec((1, A_WIDTH)),
            _const_spec((A_GROUPS, CHUNK, CHUNK)),
            _const_spec((A_GROUPS, CHUNK, 1)),
            _const_spec((CONV_W, B_WIDTH)),
            _const_spec((d, d)),
        ],
        out_specs=pl.BlockSpec((t, d), lambda i: (i, 0)),
        out_shape=jax.ShapeDtypeStruct((n, d), _F32),
        scratch_shapes=[
            pltpu.VMEM((t + 2 * HALO, d), _BF16),
            pltpu.VMEM((t, d), _BF16),
        ],
        compiler_params=_params(),
        name="even_mixer",
    )(x2, x2, x2, g.reshape(1, d), w_in.astype(_BF16), ln_g.reshape(1, -1), ln_b.reshape(1, -1),
      w_sp.astype(_BF16), b_sp[..., None], conv_w, w_out.astype(_BF16))


def _ffn_kernel(has_final_norm, x_ref, g_ref, wg_ref, wu_ref, wd_ref, *rest):
    if has_final_norm:
        gf_ref, o_ref, h_ref, acc_ref = rest
    else:
        o_ref, h_ref, acc_ref = rest
    x = x_ref[...]
    h_ref[...] = _rms(x, g_ref[...]).astype(_BF16)
    acc_ref[...] = x

    h = h_ref[...]
    for c in range(N_FF_CHUNKS):
        cols = slice(c * FF_CHUNK, (c + 1) * FF_CHUNK)
        gate = _dot(h, wg_ref[:, cols])
        up = _dot(h, wu_ref[:, cols])
        act = (jax.nn.silu(gate) * up).astype(_BF16)
        acc_ref[...] += _dot(act, wd_ref[cols, :])
    out = acc_ref[...]
    if has_final_norm:
        out = _rms(out, gf_ref[...])
    o_ref[...] = out


def _layer_spec(stacked_shape, layer):
    zeros = (0,) * (len(stacked_shape) - 1)
    return pl.BlockSpec((None,) + tuple(stacked_shape[1:]), lambda i: (layer,) + zeros,
                        pipeline_mode=pl.Buffered(1))


def _ffn_layer(x2, g, wg, wu, wd, layer, final_g=None):
    n, d = x2.shape
    t = WIDE_TILE
    in_specs = [
        pl.BlockSpec((t, d), lambda i: (i, 0)),
        _const_spec((1, d)),
        _layer_spec(wg.shape, layer),
        _layer_spec(wu.shape, layer),
        _layer_spec(wd.shape, layer),
    ]
    args = [x2, g.reshape(1, d), wg, wu, wd]
    if final_g is not None:
        in_specs.append(_const_spec((1, d)))
        args.append(final_g.reshape(1, d))
    return pl.pallas_call(
        functools.partial(_ffn_kernel, final_g is not None),
        grid=(n // t,),
        in_specs=in_specs,
        out_specs=pl.BlockSpec((t, d), lambda i: (i, 0)),
        out_shape=jax.ShapeDtypeStruct((n, d), _F32),
        scratch_shapes=[pltpu.VMEM((t, d), _BF16), pltpu.VMEM((t, d), _F32)],
        compiler_params=_params(),
        name="swiglu_ffn",
    )(*args)


def _qkv_kernel(x_ref, g_ref, wqv_ref, wk_ref, qt_ref, k_ref, vt_ref):
    h = _rms(x_ref[...], g_ref[...]).astype(_BF16)
    qv = _dot_nt(wqv_ref[...], h)
    q_rows = N_HEADS * HEAD_DIM
    scale = HEAD_DIM ** -0.5
    for j in range(WIDE_TILE // BLOCK):
        lanes = slice(j * BLOCK, (j + 1) * BLOCK)
        qt_ref[j] = (qv[:q_rows, lanes] * scale).astype(_BF16)
        vt_ref[j] = qv[q_rows:, lanes].astype(_BF16)
    k_ref[...] = _dot(h, wk_ref[...]).astype(_BF16)


def _qkv_layer(x2, g, w_qkv):
    n, d = x2.shape
    t = WIDE_TILE
    q_cols = N_HEADS * HEAD_DIM
    kv_cols = N_KV_HEADS * HEAD_DIM
    wq = w_qkv[:, :q_cols]
    wk = w_qkv[:, q_cols:q_cols + kv_cols]
    wv = w_qkv[:, q_cols + kv_cols:]
    wqv_t = jnp.concatenate([wq, wv], axis=1).T.astype(_BF16)
    n_blocks = n // BLOCK
    return pl.pallas_call(
        _qkv_kernel,
        grid=(n // t,),
        in_specs=[
            pl.BlockSpec((t, d), lambda i: (i, 0)),
            _const_spec((1, d)),
            _const_spec(wqv_t.shape),
            _const_spec((d, kv_cols)),
        ],
        out_specs=[
            pl.BlockSpec((t // BLOCK, q_cols, BLOCK), lambda i: (i, 0, 0)),
            pl.BlockSpec((t, kv_cols), lambda i: (i, 0)),
            pl.BlockSpec((t // BLOCK, kv_cols, BLOCK), lambda i: (i, 0, 0)),
        ],
        out_shape=[
            jax.ShapeDtypeStruct((n_blocks, q_cols, BLOCK), _BF16),
            jax.ShapeDtypeStruct((n, kv_cols), _BF16),
            jax.ShapeDtypeStruct((n_blocks, kv_cols, BLOCK), _BF16),
        ],
        compiler_params=_params(),
        name="qkv_proj",
    )(x2, g.reshape(1, d), wqv_t, wk.astype(_BF16))


def _attn_kernel(tiles_per_seq, sink_ref, rb_ref, x_ref, qt_ref, kc_ref, kp_ref, kn_ref,
                 vc_ref, vp_ref, vn_ref, bkt_ref, wo_ref, o_ref,
                 bias_ref, kall_ref, vall_ref, sc_ref, p_ref, at_ref):
    i = pl.program_id(0)
    pos = i % tiles_per_seq

    @pl.when(i == 0)
    def _init():
        bkt = bkt_ref[...]
        rel = (lax.broadcasted_iota(jnp.int32, (KEY_SPAN, BLOCK), 0) - BLOCK
               - lax.broadcasted_iota(jnp.int32, (KEY_SPAN, BLOCK), 1))
        band = jnp.abs(rel) <= WINDOW

        def fill_head(hd, carry):
            tbl = jnp.zeros((KEY_SPAN, BLOCK), _F32)
            for b in range(N_BUCKETS):
                tbl = jnp.where(bkt == b, rb_ref[b * N_HEADS + hd], tbl)
            bias_ref[hd] = jnp.where(band, tbl, MASK_VALUE)
            return carry

        lax.fori_loop(0, N_HEADS, fill_head, 0)

    kall_ref[0:BLOCK, :] = kp_ref[...]
    kall_ref[BLOCK:BLOCK + TOKEN_TILE, :] = kc_ref[...]
    kall_ref[BLOCK + TOKEN_TILE:, :] = kn_ref[...]
    vall_ref[0] = vp_ref[0]
    for j in range(SUB_BLOCKS):
        vall_ref[1 + j] = vc_ref[j]
    vall_ref[1 + SUB_BLOCKS] = vn_ref[0]

    pen_first = jnp.where(pos == 0, MASK_VALUE, 0.0)
    pen_last = jnp.where(pos == tiles_per_seq - 1, MASK_VALUE, 0.0)

    pair_lanes = 2 * HEAD_DIM
    slabs = [slice(r0, r0 + ROW_SLAB) for r0 in range(0, KEY_SPAN, ROW_SLAB)]
    heads = [(j, kvh) for j in range(SUB_BLOCKS) for kvh in range(N_KV_HEADS)]

    def scores(t):
        j, kvh = heads[t]
        pair, half = kvh // 2, kvh % 2
        q_heads = jnp.concatenate(
            [qt_ref[j, (kvh * Q_PER_KV + grp) * HEAD_DIM:(kvh * Q_PER_KV + grp + 1) * HEAD_DIM, :]
             for grp in range(Q_PER_KV)], axis=1)
        q_zero = jnp.zeros_like(q_heads)
        q2 = jnp.concatenate([q_heads, q_zero] if half == 0 else [q_zero, q_heads], axis=0)
        kwin = kall_ref[j * BLOCK:j * BLOCK + KEY_SPAN, pair * pair_lanes:(pair + 1) * pair_lanes]
        raw = _dot(kwin, q2)
        sc_buf = sc_ref.at[t % 2]
        maxes = []
        for grp in range(Q_PER_KV):
            hd = kvh * Q_PER_KV + grp
            lanes = slice(grp * BLOCK, (grp + 1) * BLOCK)
            mx = None
            for rows in slabs:
                s = raw[rows, lanes] + bias_ref[hd, rows, :]
                if j == 0 and rows.stop <= BLOCK:
                    s = s + pen_first
                if j == SUB_BLOCKS - 1 and rows.start >= 2 * BLOCK:
                    s = s + pen_last
                sc_buf[rows, lanes] = s
                mx = s if mx is None else jnp.maximum(mx, s)
            maxes.append(jnp.maximum(jnp.max(mx, axis=0, keepdims=True), sink_ref[hd]))
        return maxes

    def softmax(t, maxes):
        j, kvh = heads[t]
        sc_buf = sc_ref.at[t % 2]
        p_buf = p_ref.at[t % 2]
        inv_l = []
        for grp in range(Q_PER_KV):
            hd = kvh * Q_PER_KV + grp
            lanes = slice(grp * BLOCK, (grp + 1) * BLOCK)
            m = maxes[grp]
            acc = None
            for rows in slabs:
                p = jnp.exp(sc_buf[rows, lanes] - m)
                acc = p if acc is None else acc + p
                p_buf[rows, lanes] = p.astype(_BF16)
            l = jnp.sum(acc, axis=0, keepdims=True) + jnp.exp(sink_ref[hd] - m)
            inv_l.append(1.0 / l)
        return inv_l

    def weighted_values(t, inv_l):
        j, kvh = heads[t]
        vwin = jnp.concatenate(
            [vall_ref[j + b, kvh * HEAD_DIM:(kvh + 1) * HEAD_DIM, :] for b in range(3)], axis=1)
        ot = _dot(vwin, p_ref[t % 2])
        for grp in range(Q_PER_KV):
            hd = kvh * Q_PER_KV + grp
            at_ref[j, hd * HEAD_DIM:(hd + 1) * HEAD_DIM, :] = (
                ot[:, grp * BLOCK:(grp + 1) * BLOCK] * inv_l[grp]).astype(_BF16)

    def out_proj_piece(j, piece):
        rows = slice(j * BLOCK, (j + 1) * BLOCK)
        cols = slice(piece * OUT_PIECE, (piece + 1) * OUT_PIECE)
        o_ref[rows, cols] = x_ref[rows, cols] + _dot_tn(at_ref[j], wo_ref[:, cols])

    pieces_per_head = (D_MODEL // OUT_PIECE) // N_KV_HEADS
    maxes = {0: scores(0), 1: scores(1)}
    for t, (j, kvh) in enumerate(heads):
        inv_l = softmax(t, maxes.pop(t))
        weighted_values(t, inv_l)
        if j > 0:
            for piece in range(kvh * pieces_per_head, (kvh + 1) * pieces_per_head):
                out_proj_piece(j - 1, piece)
        if t + 2 < len(heads):
            maxes[t + 2] = scores(t + 2)
    for piece in range(D_MODEL // OUT_PIECE):
        out_proj_piece(SUB_BLOCKS - 1, piece)


def _t5_buckets(rel):
    nb = N_BUCKETS // 2
    ret = jnp.where(rel > 0, nb, 0)
    n = jnp.abs(rel)
    max_exact = nb // 2
    nf = jnp.maximum(n, 1).astype(jnp.float32)
    large = max_exact + (jnp.log(nf / max_exact) / math.log(MAX_DISTANCE / max_exact)
                         * (nb - max_exact)).astype(jnp.int32)
    large = jnp.minimum(large, nb - 1)
    return ret + jnp.where(n < max_exact, n, large)


def _attn_layer(x2, qt, k, vt, sink, rel_bias, w_out, seq_len):
    n, d = x2.shape
    t = TOKEN_TILE
    n_blocks = n // BLOCK
    kv_cols = N_KV_HEADS * HEAD_DIM
    q_cols = N_HEADS * HEAD_DIM
    kj = jnp.arange(KEY_SPAN, dtype=jnp.int32)[:, None]
    qi = jnp.arange(BLOCK, dtype=jnp.int32)[None, :]
    buckets_t = _t5_buckets(kj - BLOCK - qi).astype(jnp.int32)

    def prev_block(i):
        return jnp.maximum(i * SUB_BLOCKS - 1, 0)

    def next_block(i):
        return jnp.minimum((i + 1) * SUB_BLOCKS, n_blocks - 1)

    smem = pl.BlockSpec(memory_space=pltpu.SMEM)
    return pl.pallas_call(
        functools.partial(_attn_kernel, seq_len // t),
        grid=(n // t,),
        in_specs=[
            smem,
            smem,
            pl.BlockSpec((t, d), lambda i: (i, 0)),
            pl.BlockSpec((SUB_BLOCKS, q_cols, BLOCK), lambda i: (i, 0, 0)),
            pl.BlockSpec((t, kv_cols), lambda i: (i, 0)),
            pl.BlockSpec((BLOCK, kv_cols), lambda i: (prev_block(i), 0)),
            pl.BlockSpec((BLOCK, kv_cols), lambda i: (next_block(i), 0)),
            pl.BlockSpec((SUB_BLOCKS, kv_cols, BLOCK), lambda i: (i, 0, 0)),
            pl.BlockSpec((1, kv_cols, BLOCK), lambda i: (prev_block(i), 0, 0)),
            pl.BlockSpec((1, kv_cols, BLOCK), lambda i: (next_block(i), 0, 0)),
            _const_spec((KEY_SPAN, BLOCK)),
            _const_spec((d, d)),
        ],
        out_specs=pl.BlockSpec((t, d), lambda i: (i, 0)),
        out_shape=jax.ShapeDtypeStruct((n, d), _F32),
        scratch_shapes=[
            pltpu.VMEM((N_HEADS, KEY_SPAN, BLOCK), _F32),
            pltpu.VMEM((TOKEN_TILE + 2 * BLOCK, kv_cols), _BF16),
            pltpu.VMEM((SUB_BLOCKS + 2, kv_cols, BLOCK), _BF16),
            pltpu.VMEM((2, KEY_SPAN, Q_PER_KV * BLOCK), _F32),
            pltpu.VMEM((2, KEY_SPAN, Q_PER_KV * BLOCK), _BF16),
            pltpu.VMEM((SUB_BLOCKS, q_cols, BLOCK), _BF16),
        ],
        compiler_params=_params(),
        name="window_attention",
    )(sink, rel_bias.reshape(-1), x2, qt, k, k, k, vt, vt, vt, buckets_t, w_out.astype(_BF16))


def kernel(x, norm_mix, norm_ffn, even_w_in, even_v_ln_g, even_v_ln_b, even_w_spatial, even_b_spatial,
           even_conv_w, even_w_out, attn_w_qkv, attn_sink, rel_bias, attn_w_out, ffn_w_gate, ffn_w_up,
           ffn_w_down, final_norm):
    bsz, seq_len, d = x.shape
    assert d == D_MODEL and seq_len % WIDE_TILE == 0
    x2 = x.reshape(bsz * seq_len, d)
    wg, wu, wd = ffn_w_gate.astype(_BF16), ffn_w_up.astype(_BF16), ffn_w_down.astype(_BF16)
    x2 = _even_layer(x2, norm_mix[0], even_w_in[0], even_v_ln_g[0], even_v_ln_b[0], even_w_spatial[0],
                     even_b_spatial[0], even_conv_w[0], even_w_out[0], seq_len)
    x2 = _ffn_layer(x2, norm_ffn[0], wg, wu, wd, 0)
    qt, k, vt = _qkv_layer(x2, norm_mix[1], attn_w_qkv[0])
    x2 = _attn_layer(x2, qt, k, vt, attn_sink[0], rel_bias, attn_w_out[0], seq_len)
    x2 = _ffn_layer(x2, norm_ffn[1], wg, wu, wd, 1, final_g=final_norm)
    return x2.reshape(bsz, seq_len, d)
```

```python
import functools
import math

import jax
import jax.numpy as jnp
from jax import lax
from jax.experimental import pallas as pl
from jax.experimental.pallas import tpu as pltpu

D_MODEL = 1024
EPS = 1e-6
A_WIDTH = 512
B_WIDTH = 512
A_GROUPS = 4
A_GROUP_DIM = 128
CHUNK = 128
CONV_W = 3
N_HEADS = 16
N_KV_HEADS = 4
Q_PER_KV = 4
HEAD_DIM = 64
WINDOW = 128
BLOCK = 128
N_BUCKETS = 32
MAX_DISTANCE = 128
D_FF = 2816

TOKEN_TILE = 1024
WIDE_TILE = 1024
HALO = 16
EVEN_PARTS = 4
FF_CHUNK = 256
N_FF_CHUNKS = D_FF // FF_CHUNK
SUB_BLOCKS = TOKEN_TILE // BLOCK
KEY_SPAN = 3 * BLOCK
MASK_VALUE = -1e30
ROW_SLAB = 64
OUT_PIECE = 256
VMEM_LIMIT_BYTES = 56 * 1024 * 1024

_BF16 = jnp.bfloat16
_F32 = jnp.float32


def _dot(a, b):
    return jnp.dot(a, b, preferred_element_type=_F32)


def _dot_nt(a, b):
    return lax.dot_general(a, b, (((1,), (1,)), ((), ())), preferred_element_type=_F32)


def _dot_tn(a, b):
    return lax.dot_general(a, b, (((0,), (0,)), ((), ())), preferred_element_type=_F32)


def _gelu(x):
    return 0.5 * x * (1.0 + lax.erf(x * math.sqrt(0.5)))


def _rms(x, g):
    return (x * lax.rsqrt(jnp.mean(x * x, axis=-1, keepdims=True) + EPS)) * g


def _const_spec(shape):
    zeros = (0,) * len(shape)
    return pl.BlockSpec(shape, lambda i: zeros, pipeline_mode=pl.Buffered(1))


def _params():
    return pltpu.CompilerParams(
        dimension_semantics=("arbitrary",), vmem_limit_bytes=VMEM_LIMIT_BYTES)


def _even_kernel(tiles_per_seq, x_ref, xp_ref, xn_ref, g_ref, win_ref, lng_ref, lnb_ref,
                 wsp_ref, bsp_ref, cw_ref, wo_ref, o_ref, hs_ref, conv_ref, y_ref):
    t = WIDE_TILE
    i = pl.program_id(0)
    pos = i % tiles_per_seq
    g = g_ref[...]
    h = _rms(x_ref[...], g).astype(_BF16)
    hp = jnp.where(pos == 0, 0.0, _rms(xp_ref[...], g)).astype(_BF16)
    hn = jnp.where(pos == tiles_per_seq - 1, 0.0, _rms(xn_ref[...], g)).astype(_BF16)
    hs_ref[0:HALO, :] = hp
    hs_ref[HALO:HALO + t, :] = h
    hs_ref[HALO + t:, :] = hn

    in_width_a = 2 * A_WIDTH + B_WIDTH
    pc = _dot(hs_ref[...], win_ref[:, in_width_a:])

    z = pc[:, :B_WIDTH] * pc[:, B_WIDTH:]
    n_rows = t + 2 * HALO
    z_prev = pltpu.roll(z, 1, 0)[HALO:HALO + t]
    z_next = pltpu.roll(z, n_rows - 1, 0)[HALO:HALO + t]
    cw = cw_ref[...]
    conv_ref[...] = z_prev * cw[0:1] + z[HALO:HALO + t] * cw[1:2] + z_next * cw[2:3]

    part = t // EVEN_PARTS
    parts = [slice(q * part, (q + 1) * part) for q in range(EVEN_PARTS)]
    pas = [_dot(hs_ref[HALO + rows.start:HALO + rows.stop, :], win_ref[:, :in_width_a])
           for rows in parts]
    for rows, pa in zip(parts, pas):
        u = _gelu(pa[:, :A_WIDTH])
        v = _gelu(pa[:, A_WIDTH:2 * A_WIDTH])
        mu = jnp.mean(v, axis=-1, keepdims=True)
        vc = v - mu
        vn = vc * lax.rsqrt(jnp.mean(vc * vc, axis=-1, keepdims=True) + EPS)
        vb = (vn * lng_ref[...] + lnb_ref[...]).astype(_BF16)
        for c in range(part // CHUNK):
            crows = slice(c * CHUNK, (c + 1) * CHUNK)
            orows = slice(rows.start + c * CHUNK, rows.start + (c + 1) * CHUNK)
            for grp in range(A_GROUPS):
                cols = slice(grp * A_GROUP_DIM, (grp + 1) * A_GROUP_DIM)
                mixed = _dot(wsp_ref[grp], vb[crows, cols]) + bsp_ref[grp]
                y_ref[orows, cols] = (u[crows, cols] * mixed).astype(_BF16)
        y_ref[rows, A_WIDTH:] = (pa[:, 2 * A_WIDTH:] * conv_ref[rows, :]).astype(_BF16)
        o_ref[rows, :] = x_ref[rows, :] + _dot(y_ref[rows, :], wo_ref[...])


def _even_layer(x2, g, w_in, ln_g, ln_b, w_sp, b_sp, conv_w, w_out, seq_len):
    n, d = x2.shape
    t = WIDE_TILE
    halo_blocks_per_tile = t // HALO
    n_halo_blocks = n // HALO
    return pl.pallas_call(
        functools.partial(_even_kernel, seq_len // t),
        grid=(n // t,),
        in_specs=[
            pl.BlockSpec((t, d), lambda i: (i, 0)),
            pl.BlockSpec((HALO, d), lambda i: (jnp.maximum(i * halo_blocks_per_tile - 1, 0), 0)),
            pl.BlockSpec((HALO, d),
                         lambda i: (jnp.minimum((i + 1) * halo_blocks_per_tile, n_halo_blocks - 1), 0)),
            _const_spec((1, d)),
            _const_spec(w_in.shape),
            _const_spec((1, A_WIDTH)),
            _const_spec((1, A_WIDTH)),
            _const_spec((A_GROUPS, CHUNK, CHUNK)),
            _const_spec((A_GROUPS, CHUNK, 1)),
            _const_spec((CONV_W, B_WIDTH)),
            _const_spec((d, d)),
        ],
        out_specs=pl.BlockSpec((t, d), lambda i: (i, 0)),
        out_shape=jax.ShapeDtypeStruct((n, d), _F32),
        scratch_shapes=[
            pltpu.VMEM((t + 2 * HALO, d), _BF16),
            pltpu.VMEM((t, B_WIDTH), _F32),
            pltpu.VMEM((t, d), _BF16),
        ],
        compiler_params=_params(),
        name="even_mixer",
    )(x2, x2, x2, g.reshape(1, d), w_in.astype(_BF16), ln_g.reshape(1, -1), ln_b.reshape(1, -1),
      w_sp.astype(_BF16), b_sp[..., None], conv_w, w_out.astype(_BF16))


def _ffn_kernel(has_final_norm, x_ref, g_ref, wg_ref, wu_ref, wd_ref, *rest):
    if has_final_norm:
        gf_ref, o_ref, h_ref, acc_ref = rest
    else:
        o_ref, h_ref, acc_ref = rest
    x = x_ref[...]
    h_ref[...] = _rms(x, g_ref[...]).astype(_BF16)
    acc_ref[...] = x

    h = h_ref[...]
    for c in range(N_FF_CHUNKS):
        cols = slice(c * FF_CHUNK, (c + 1) * FF_CHUNK)
        gate = _dot(h, wg_ref[:, cols])
        up = _dot(h, wu_ref[:, cols])
        act = (jax.nn.silu(gate) * up).astype(_BF16)
        acc_ref[...] += _dot(act, wd_ref[cols, :])
    out = acc_ref[...]
    if has_final_norm:
        out = _rms(out, gf_ref[...])
    o_ref[...] = out


def _layer_spec(stacked_shape, layer):
    zeros = (0,) * (len(stacked_shape) - 1)
    return pl.BlockSpec((None,) + tuple(stacked_shape[1:]), lambda i: (layer,) + zeros,
                        pipeline_mode=pl.Buffered(1))


def _ffn_layer(x2, g, wg, wu, wd, layer, final_g=None):
    n, d = x2.shape
    t = WIDE_TILE
    in_specs = [
        pl.BlockSpec((t, d), lambda i: (i, 0)),
        _const_spec((1, d)),
        _layer_spec(wg.shape, layer),
        _layer_spec(wu.shape, layer),
        _layer_spec(wd.shape, layer),
    ]
    args = [x2, g.reshape(1, d), wg, wu, wd]
    if final_g is not None:
        in_specs.append(_const_spec((1, d)))
        args.append(final_g.reshape(1, d))
    return pl.pallas_call(
        functools.partial(_ffn_kernel, final_g is not None),
        grid=(n // t,),
        in_specs=in_specs,
        out_specs=pl.BlockSpec((t, d), lambda i: (i, 0)),
        out_shape=jax.ShapeDtypeStruct((n, d), _F32),
        scratch_shapes=[pltpu.VMEM((t, d), _BF16), pltpu.VMEM((t, d), _F32)],
        compiler_params=_params(),
        name="swiglu_ffn",
    )(*args)


def _qkv_kernel(x_ref, g_ref, wqv_ref, wk_ref, qt_ref, k_ref, vt_ref):
    h = _rms(x_ref[...], g_ref[...]).astype(_BF16)
    qv = _dot_nt(wqv_ref[...], h)
    q_rows = N_HEADS * HEAD_DIM
    scale = HEAD_DIM ** -0.5
    for j in range(WIDE_TILE // BLOCK):
        lanes = slice(j * BLOCK, (j + 1) * BLOCK)
        qt_ref[j] = (qv[:q_rows, lanes] * scale).astype(_BF16)
        vt_ref[j] = qv[q_rows:, lanes].astype(_BF16)
    k_ref[...] = _dot(h, wk_ref[...]).astype(_BF16)


def _qkv_layer(x2, g, w_qkv):
    n, d = x2.shape
    t = WIDE_TILE
    q_cols = N_HEADS * HEAD_DIM
    kv_cols = N_KV_HEADS * HEAD_DIM
    wq = w_qkv[:, :q_cols]
    wk = w_qkv[:, q_cols:q_cols + kv_cols]
    wv = w_qkv[:, q_cols + kv_cols:]
    wqv_t = jnp.concatenate([wq, wv], axis=1).T.astype(_BF16)
    n_blocks = n // BLOCK
    return pl.pallas_call(
        _qkv_kernel,
        grid=(n // t,),
        in_specs=[
            pl.BlockSpec((t, d), lambda i: (i, 0)),
            _const_spec((1, d)),
            _const_spec(wqv_t.shape),
            _const_spec((d, kv_cols)),
        ],
        out_specs=[
            pl.BlockSpec((t // BLOCK, q_cols, BLOCK), lambda i: (i, 0, 0)),
            pl.BlockSpec((t, kv_cols), lambda i: (i, 0)),
            pl.BlockSpec((t // BLOCK, kv_cols, BLOCK), lambda i: (i, 0, 0)),
        ],
        out_shape=[
            jax.ShapeDtypeStruct((n_blocks, q_cols, BLOCK), _BF16),
            jax.ShapeDtypeStruct((n, kv_cols), _BF16),
            jax.ShapeDtypeStruct((n_blocks, kv_cols, BLOCK), _BF16),
        ],
        compiler_params=_params(),
        name="qkv_proj",
    )(x2, g.reshape(1, d), wqv_t, wk.astype(_BF16))


def _attn_kernel(tiles_per_seq, sink_ref, rb_ref, x_ref, qt_ref, kc_ref, kp_ref, kn_ref,
                 vc_ref, vp_ref, vn_ref, bkt_ref, wo_ref, o_ref,
                 bias_ref, kall_ref, vall_ref, sc_ref, p_ref, at_ref):
    i = pl.program_id(0)
    pos = i % tiles_per_seq

    @pl.when(i == 0)
    def _init():
        bkt = bkt_ref[...]
        rel = (lax.broadcasted_iota(jnp.int32, (KEY_SPAN, BLOCK), 0) - BLOCK
               - lax.broadcasted_iota(jnp.int32, (KEY_SPAN, BLOCK), 1))
        band = jnp.abs(rel) <= WINDOW

        def fill_head(hd, carry):
            tbl = jnp.zeros((KEY_SPAN, BLOCK), _F32)
            for b in range(N_BUCKETS):
                tbl = jnp.where(bkt == b, rb_ref[b * N_HEADS + hd], tbl)
            bias_ref[hd] = jnp.where(band, tbl, MASK_VALUE)
            return carry

        lax.fori_loop(0, N_HEADS, fill_head, 0)

    kall_ref[0:BLOCK, :] = kp_ref[...]
    kall_ref[BLOCK:BLOCK + TOKEN_TILE, :] = kc_ref[...]
    kall_ref[BLOCK + TOKEN_TILE:, :] = kn_ref[...]
    vall_ref[0] = vp_ref[0]
    for j in range(SUB_BLOCKS):
        vall_ref[1 + j] = vc_ref[j]
    vall_ref[1 + SUB_BLOCKS] = vn_ref[0]

    pen_first = jnp.where(pos == 0, MASK_VALUE, 0.0)
    pen_last = jnp.where(pos == tiles_per_seq - 1, MASK_VALUE, 0.0)

    pair_lanes = 2 * HEAD_DIM
    slabs = [slice(r0, r0 + ROW_SLAB) for r0 in range(0, KEY_SPAN, ROW_SLAB)]
    heads = [(j, kvh) for j in range(SUB_BLOCKS) for kvh in range(N_KV_HEADS)]

    def scores(t):
        j, kvh = heads[t]
        pair, half = kvh // 2, kvh % 2
        q_heads = jnp.concatenate(
            [qt_ref[j, (kvh * Q_PER_KV + grp) * HEAD_DIM:(kvh * Q_PER_KV + grp + 1) * HEAD_DIM, :]
             for grp in range(Q_PER_KV)], axis=1)
        q_zero = jnp.zeros_like(q_heads)
        q2 = jnp.concatenate([q_heads, q_zero] if half == 0 else [q_zero, q_heads], axis=0)
        kwin = kall_ref[j * BLOCK:j * BLOCK + KEY_SPAN, pair * pair_lanes:(pair + 1) * pair_lanes]
        raw = _dot(kwin, q2)
        sc_buf = sc_ref.at[t % 2]
        maxes = []
        for grp in range(Q_PER_KV):
            hd = kvh * Q_PER_KV + grp
            lanes = slice(grp * BLOCK, (grp + 1) * BLOCK)
            mx = None
            for rows in slabs:
                s = raw[rows, lanes] + bias_ref[hd, rows, :]
                if j == 0 and rows.stop <= BLOCK:
                    s = s + pen_first
                if j == SUB_BLOCKS - 1 and rows.start >= 2 * BLOCK:
                    s = s + pen_last
                sc_buf[rows, lanes] = s
                mx = s if mx is None else jnp.maximum(mx, s)
            maxes.append(jnp.maximum(jnp.max(mx, axis=0, keepdims=True), sink_ref[hd]))
        return maxes

    def softmax(t, maxes):
        j, kvh = heads[t]
        sc_buf = sc_ref.at[t % 2]
        p_buf = p_ref.at[t % 2]
        inv_l = []
        for grp in range(Q_PER_KV):
            hd = kvh * Q_PER_KV + grp
            lanes = slice(grp * BLOCK, (grp + 1) * BLOCK)
            m = maxes[grp]
            acc = None
            for rows in slabs:
                p = jnp.exp(sc_buf[rows, lanes] - m)
                acc = p if acc is None else acc + p
                p_buf[rows, lanes] = p.astype(_BF16)
            l = jnp.sum(acc, axis=0, keepdims=True) + jnp.exp(sink_ref[hd] - m)
            inv_l.append(1.0 / l)
        return inv_l

    def weighted_values(t, inv_l):
        j, kvh = heads[t]
        vwin = jnp.concatenate(
            [vall_ref[j + b, kvh * HEAD_DIM:(kvh + 1) * HEAD_DIM, :] for b in range(3)], axis=1)
        ot = _dot(vwin, p_ref[t % 2])
        for grp in range(Q_PER_KV):
            hd = kvh * Q_PER_KV + grp
            at_ref[j, hd * HEAD_DIM:(hd + 1) * HEAD_DIM, :] = (
                ot[:, grp * BLOCK:(grp + 1) * BLOCK] * inv_l[grp]).astype(_BF16)

    def out_proj_piece(j, piece):
        rows = slice(j * BLOCK, (j + 1) * BLOCK)
        cols = slice(piece * OUT_PIECE, (piece + 1) * OUT_PIECE)
        o_ref[rows, cols] = x_ref[rows, cols] + _dot_tn(at_ref[j], wo_ref[:, cols])

    pieces_per_head = (D_MODEL // OUT_PIECE) // N_KV_HEADS
    maxes = {0: scores(0), 1: scores(1)}
    for t, (j, kvh) in enumerate(heads):
        inv_l = softmax(t, maxes.pop(t))
        weighted_values(t, inv_l)
        if j > 0:
            for piece in range(kvh * pieces_per_head, (kvh + 1) * pieces_per_head):
                out_proj_piece(j - 1, piece)
        if t + 2 < len(heads):
            maxes[t + 2] = scores(t + 2)
    for piece in range(D_MODEL // OUT_PIECE):
        out_proj_piece(SUB_BLOCKS - 1, piece)


def _t5_buckets(rel):
    nb = N_BUCKETS // 2
    ret = jnp.where(rel > 0, nb, 0)
    n = jnp.abs(rel)
    max_exact = nb // 2
    nf = jnp.maximum(n, 1).astype(jnp.float32)
    large = max_exact + (jnp.log(nf / max_exact) / math.log(MAX_DISTANCE / max_exact)
                         * (nb - max_exact)).astype(jnp.int32)
    large = jnp.minimum(large, nb - 1)
    return ret + jnp.where(n < max_exact, n, large)


def _attn_layer(x2, qt, k, vt, sink, rel_bias, w_out, seq_len):
    n, d = x2.shape
    t = TOKEN_TILE
    n_blocks = n // BLOCK
    kv_cols = N_KV_HEADS * HEAD_DIM
    q_cols = N_HEADS * HEAD_DIM
    kj = jnp.arange(KEY_SPAN, dtype=jnp.int32)[:, None]
    qi = jnp.arange(BLOCK, dtype=jnp.int32)[None, :]
    buckets_t = _t5_buckets(kj - BLOCK - qi).astype(jnp.int32)

    def prev_block(i):
        return jnp.maximum(i * SUB_BLOCKS - 1, 0)

    def next_block(i):
        return jnp.minimum((i + 1) * SUB_BLOCKS, n_blocks - 1)

    smem = pl.BlockSpec(memory_space=pltpu.SMEM)
    return pl.pallas_call(
        functools.partial(_attn_kernel, seq_len // t),
        grid=(n // t,),
        in_specs=[
            smem,
            smem,
            pl.BlockSpec((t, d), lambda i: (i, 0)),
            pl.BlockSpec((SUB_BLOCKS, q_cols, BLOCK), lambda i: (i, 0, 0)),
            pl.BlockSpec((t, kv_cols), lambda i: (i, 0)),
            pl.BlockSpec((BLOCK, kv_cols), lambda i: (prev_block(i), 0)),
            pl.BlockSpec((BLOCK, kv_cols), lambda i: (next_block(i), 0)),
            pl.BlockSpec((SUB_BLOCKS, kv_cols, BLOCK), lambda i: (i, 0, 0)),
            pl.BlockSpec((1, kv_cols, BLOCK), lambda i: (prev_block(i), 0, 0)),
            pl.BlockSpec((1, kv_cols, BLOCK), lambda i: (next_block(i), 0, 0)),
            _const_spec((KEY_SPAN, BLOCK)),
            _const_spec((d, d)),
        ],
        out_specs=pl.BlockSpec((t, d), lambda i: (i, 0)),
        out_shape=jax.ShapeDtypeStruct((n, d), _F32),
        scratch_shapes=[
            pltpu.VMEM((N_HEADS, KEY_SPAN, BLOCK), _F32),
            pltpu.VMEM((TOKEN_TILE + 2 * BLOCK, kv_cols), _BF16),
            pltpu.VMEM((SUB_BLOCKS + 2, kv_cols, BLOCK), _BF16),
            pltpu.VMEM((2, KEY_SPAN, Q_PER_KV * BLOCK), _F32),
            pltpu.VMEM((2, KEY_SPAN, Q_PER_KV * BLOCK), _BF16),
            pltpu.VMEM((SUB_BLOCKS, q_cols, BLOCK), _BF16),
        ],
        compiler_params=_params(),
        name="window_attention",
    )(sink, rel_bias.reshape(-1), x2, qt, k, k, k, vt, vt, vt, buckets_t, w_out.astype(_BF16))


def kernel(x, norm_mix, norm_ffn, even_w_in, even_v_ln_g, even_v_ln_b, even_w_spatial, even_b_spatial,
           even_conv_w, even_w_out, attn_w_qkv, attn_sink, rel_bias, attn_w_out, ffn_w_gate, ffn_w_up,
           ffn_w_down, final_norm):
    bsz, seq_len, d = x.shape
    assert d == D_MODEL and seq_len % WIDE_TILE == 0
    x2 = x.reshape(bsz * seq_len, d)
    wg, wu, wd = ffn_w_gate.astype(_BF16), ffn_w_up.astype(_BF16), ffn_w_down.astype(_BF16)
    x2 = _even_layer(x2, norm_mix[0], even_w_in[0], even_v_ln_g[0], even_v_ln_b[0], even_w_spatial[0],
                     even_b_spatial[0], even_conv_w[0], even_w_out[0], seq_len)
    x2 = _ffn_layer(x2, norm_ffn[0], wg, wu, wd, 0)
    qt, k, vt = _qkv_layer(x2, norm_mix[1], attn_w_qkv[0])
    x2 = _attn_layer(x2, qt, k, vt, attn_sink[0], rel_bias, attn_w_out[0], seq_len)
    x2 = _ffn_layer(x2, norm_ffn[1], wg, wu, wd, 1, final_g=final_norm)
    return x2.reshape(bsz, seq_len, d)
```

```python
import functools
import math

import jax
import jax.numpy as jnp
from jax import lax
from jax.experimental import pallas as pl
from jax.experimental.pallas import tpu as pltpu

D_MODEL = 1024
EPS = 1e-6
A_WIDTH = 512
B_WIDTH = 512
A_GROUPS = 4
A_GROUP_DIM = 128
CHUNK = 128
CONV_W = 3
N_HEADS = 16
N_KV_HEADS = 4
Q_PER_KV = 4
HEAD_DIM = 64
WINDOW = 128
BLOCK = 128
N_BUCKETS = 32
MAX_DISTANCE = 128
D_FF = 2816

TOKEN_TILE = 1024
WIDE_TILE = 1024
HALO = 16
EVEN_PARTS = 4
FF_CHUNK = 256
N_FF_CHUNKS = D_FF // FF_CHUNK
SUB_BLOCKS = TOKEN_TILE // BLOCK
KEY_SPAN = 3 * BLOCK
MASK_VALUE = -1e30
LOG2_E = math.log2(math.e)
ROW_SLAB = 64
OUT_PIECE = 256
VMEM_LIMIT_BYTES = 56 * 1024 * 1024

_BF16 = jnp.bfloat16
_F32 = jnp.float32


def _dot(a, b):
    return jnp.dot(a, b, preferred_element_type=_F32)


def _dot_nt(a, b):
    return lax.dot_general(a, b, (((1,), (1,)), ((), ())), preferred_element_type=_F32)


def _dot_tn(a, b):
    return lax.dot_general(a, b, (((0,), (0,)), ((), ())), preferred_element_type=_F32)


def _gelu(x):
    return 0.5 * x * (1.0 + lax.erf(x * math.sqrt(0.5)))


def _rms(x, g):
    return (x * lax.rsqrt(jnp.mean(x * x, axis=-1, keepdims=True) + EPS)) * g


def _const_spec(shape):
    zeros = (0,) * len(shape)
    return pl.BlockSpec(shape, lambda i: zeros, pipeline_mode=pl.Buffered(1))


def _params():
    return pltpu.CompilerParams(
        dimension_semantics=("arbitrary",), vmem_limit_bytes=VMEM_LIMIT_BYTES)


def _even_kernel(tiles_per_seq, x_ref, xp_ref, xn_ref, g_ref, win_ref, lng_ref, lnb_ref,
                 wsp_ref, bsp_ref, cw_ref, wo_ref, o_ref, hs_ref, conv_ref, y_ref):
    t = WIDE_TILE
    i = pl.program_id(0)
    pos = i % tiles_per_seq
    g = g_ref[...]
    h = _rms(x_ref[...], g).astype(_BF16)
    hp = jnp.where(pos == 0, 0.0, _rms(xp_ref[...], g)).astype(_BF16)
    hn = jnp.where(pos == tiles_per_seq - 1, 0.0, _rms(xn_ref[...], g)).astype(_BF16)
    hs_ref[0:HALO, :] = hp
    hs_ref[HALO:HALO + t, :] = h
    hs_ref[HALO + t:, :] = hn

    in_width_a = 2 * A_WIDTH + B_WIDTH
    pc = _dot(hs_ref[...], win_ref[:, in_width_a:])

    z = pc[:, :B_WIDTH] * pc[:, B_WIDTH:]
    n_rows = t + 2 * HALO
    z_prev = pltpu.roll(z, 1, 0)[HALO:HALO + t]
    z_next = pltpu.roll(z, n_rows - 1, 0)[HALO:HALO + t]
    cw = cw_ref[...]
    conv_ref[...] = z_prev * cw[0:1] + z[HALO:HALO + t] * cw[1:2] + z_next * cw[2:3]

    part = t // EVEN_PARTS
    parts = [slice(q * part, (q + 1) * part) for q in range(EVEN_PARTS)]
    pas = [_dot(hs_ref[HALO + rows.start:HALO + rows.stop, :], win_ref[:, :in_width_a])
           for rows in parts]
    for rows, pa in zip(parts, pas):
        u = _gelu(pa[:, :A_WIDTH])
        v = _gelu(pa[:, A_WIDTH:2 * A_WIDTH])
        mu = jnp.mean(v, axis=-1, keepdims=True)
        vc = v - mu
        vn = vc * lax.rsqrt(jnp.mean(vc * vc, axis=-1, keepdims=True) + EPS)
        vb = (vn * lng_ref[...] + lnb_ref[...]).astype(_BF16)
        for c in range(part // CHUNK):
            crows = slice(c * CHUNK, (c + 1) * CHUNK)
            orows = slice(rows.start + c * CHUNK, rows.start + (c + 1) * CHUNK)
            for grp in range(A_GROUPS):
                cols = slice(grp * A_GROUP_DIM, (grp + 1) * A_GROUP_DIM)
                mixed = _dot(wsp_ref[grp], vb[crows, cols]) + bsp_ref[grp]
                y_ref[orows, cols] = (u[crows, cols] * mixed).astype(_BF16)
        y_ref[rows, A_WIDTH:] = (pa[:, 2 * A_WIDTH:] * conv_ref[rows, :]).astype(_BF16)
        o_ref[rows, :] = x_ref[rows, :] + _dot(y_ref[rows, :], wo_ref[...])


def _even_layer(x2, g, w_in, ln_g, ln_b, w_sp, b_sp, conv_w, w_out, seq_len):
    n, d = x2.shape
    t = WIDE_TILE
    halo_blocks_per_tile = t // HALO
    n_halo_blocks = n // HALO
    return pl.pallas_call(
        functools.partial(_even_kernel, seq_len // t),
        grid=(n // t,),
        in_specs=[
            pl.BlockSpec((t, d), lambda i: (i, 0)),
            pl.BlockSpec((HALO, d), lambda i: (jnp.maximum(i * halo_blocks_per_tile - 1, 0), 0)),
            pl.BlockSpec((HALO, d),
                         lambda i: (jnp.minimum((i + 1) * halo_blocks_per_tile, n_halo_blocks - 1), 0)),
            _const_spec((1, d)),
            _const_spec(w_in.shape),
            _const_spec((1, A_WIDTH)),
            _const_spec((1, A_WIDTH)),
            _const_spec((A_GROUPS, CHUNK, CHUNK)),
            _const_spec((A_GROUPS, CHUNK, 1)),
            _const_spec((CONV_W, B_WIDTH)),
            _const_spec((d, d)),
        ],
        out_specs=pl.BlockSpec((t, d), lambda i: (i, 0)),
        out_shape=jax.ShapeDtypeStruct((n, d), _F32),
        scratch_shapes=[
            pltpu.VMEM((t + 2 * HALO, d), _BF16),
            pltpu.VMEM((t, B_WIDTH), _F32),
            pltpu.VMEM((t, d), _BF16),
        ],
        compiler_params=_params(),
        name="even_mixer",
    )(x2, x2, x2, g.reshape(1, d), w_in.astype(_BF16), ln_g.reshape(1, -1), ln_b.reshape(1, -1),
      w_sp.astype(_BF16), b_sp[..., None], conv_w, w_out.astype(_BF16))


def _ffn_kernel(has_final_norm, x_ref, g_ref, wg_ref, wu_ref, wd_ref, *rest):
    if has_final_norm:
        gf_ref, o_ref, h_ref, acc_ref = rest
    else:
        o_ref, h_ref, acc_ref = rest
    x = x_ref[...]
    h_ref[...] = _rms(x, g_ref[...]).astype(_BF16)
    acc_ref[...] = x

    h = h_ref[...]
    for c in range(N_FF_CHUNKS):
        cols = slice(c * FF_CHUNK, (c + 1) * FF_CHUNK)
        gate = _dot(h, wg_ref[:, cols])
        up = _dot(h, wu_ref[:, cols])
        act = (jax.nn.silu(gate) * up).astype(_BF16)
        acc_ref[...] += _dot(act, wd_ref[cols, :])
    out = acc_ref[...]
    if has_final_norm:
        out = _rms(out, gf_ref[...])
    o_ref[...] = out


def _layer_spec(stacked_shape, layer):
    zeros = (0,) * (len(stacked_shape) - 1)
    return pl.BlockSpec((None,) + tuple(stacked_shape[1:]), lambda i: (layer,) + zeros,
                        pipeline_mode=pl.Buffered(1))


def _ffn_layer(x2, g, wg, wu, wd, layer, final_g=None):
    n, d = x2.shape
    t = WIDE_TILE
    in_specs = [
        pl.BlockSpec((t, d), lambda i: (i, 0)),
        _const_spec((1, d)),
        _layer_spec(wg.shape, layer),
        _layer_spec(wu.shape, layer),
        _layer_spec(wd.shape, layer),
    ]
    args = [x2, g.reshape(1, d), wg, wu, wd]
    if final_g is not None:
        in_specs.append(_const_spec((1, d)))
        args.append(final_g.reshape(1, d))
    return pl.pallas_call(
        functools.partial(_ffn_kernel, final_g is not None),
        grid=(n // t,),
        in_specs=in_specs,
        out_specs=pl.BlockSpec((t, d), lambda i: (i, 0)),
        out_shape=jax.ShapeDtypeStruct((n, d), _F32),
        scratch_shapes=[pltpu.VMEM((t, d), _BF16), pltpu.VMEM((t, d), _F32)],
        compiler_params=_params(),
        name="swiglu_ffn",
    )(*args)


def _qkv_kernel(x_ref, g_ref, wqv_ref, wk_ref, qt_ref, k_ref, vt_ref):
    h = _rms(x_ref[...], g_ref[...]).astype(_BF16)
    qv = _dot_nt(wqv_ref[...], h)
    q_rows = N_HEADS * HEAD_DIM
    scale = HEAD_DIM ** -0.5 * LOG2_E
    for j in range(WIDE_TILE // BLOCK):
        lanes = slice(j * BLOCK, (j + 1) * BLOCK)
        qt_ref[j] = (qv[:q_rows, lanes] * scale).astype(_BF16)
        vt_ref[j] = qv[q_rows:, lanes].astype(_BF16)
    k_ref[...] = _dot(h, wk_ref[...]).astype(_BF16)


def _qkv_layer(x2, g, w_qkv):
    n, d = x2.shape
    t = WIDE_TILE
    q_cols = N_HEADS * HEAD_DIM
    kv_cols = N_KV_HEADS * HEAD_DIM
    wq = w_qkv[:, :q_cols]
    wk = w_qkv[:, q_cols:q_cols + kv_cols]
    wv = w_qkv[:, q_cols + kv_cols:]
    wqv_t = jnp.concatenate([wq, wv], axis=1).T.astype(_BF16)
    n_blocks = n // BLOCK
    return pl.pallas_call(
        _qkv_kernel,
        grid=(n // t,),
        in_specs=[
            pl.BlockSpec((t, d), lambda i: (i, 0)),
            _const_spec((1, d)),
            _const_spec(wqv_t.shape),
            _const_spec((d, kv_cols)),
        ],
        out_specs=[
            pl.BlockSpec((t // BLOCK, q_cols, BLOCK), lambda i: (i, 0, 0)),
            pl.BlockSpec((t, kv_cols), lambda i: (i, 0)),
            pl.BlockSpec((t // BLOCK, kv_cols, BLOCK), lambda i: (i, 0, 0)),
        ],
        out_shape=[
            jax.ShapeDtypeStruct((n_blocks, q_cols, BLOCK), _BF16),
            jax.ShapeDtypeStruct((n, kv_cols), _BF16),
            jax.ShapeDtypeStruct((n_blocks, kv_cols, BLOCK), _BF16),
        ],
        compiler_params=_params(),
        name="qkv_proj",
    )(x2, g.reshape(1, d), wqv_t, wk.astype(_BF16))


def _attn_kernel(tiles_per_seq, sink_ref, rb_ref, x_ref, qt_ref, kc_ref, kp_ref, kn_ref,
                 vc_ref, vp_ref, vn_ref, bkt_ref, wo_ref, o_ref,
                 bias_ref, kall_ref, vall_ref, sc_ref, p_ref, at_ref):
    i = pl.program_id(0)
    pos = i % tiles_per_seq

    @pl.when(i == 0)
    def _init():
        bkt = bkt_ref[...]
        rel = (lax.broadcasted_iota(jnp.int32, (KEY_SPAN, BLOCK), 0) - BLOCK
               - lax.broadcasted_iota(jnp.int32, (KEY_SPAN, BLOCK), 1))
        band = jnp.abs(rel) <= WINDOW

        def fill_head(hd, carry):
            tbl = jnp.zeros((KEY_SPAN, BLOCK), _F32)
            for b in range(N_BUCKETS):
                tbl = jnp.where(bkt == b, rb_ref[b * N_HEADS + hd], tbl)
            bias_ref[hd] = jnp.where(band, tbl * LOG2_E, MASK_VALUE)
            return carry

        lax.fori_loop(0, N_HEADS, fill_head, 0)

    kall_ref[0:BLOCK, :] = kp_ref[...]
    kall_ref[BLOCK:BLOCK + TOKEN_TILE, :] = kc_ref[...]
    kall_ref[BLOCK + TOKEN_TILE:, :] = kn_ref[...]
    vall_ref[0] = vp_ref[0]
    for j in range(SUB_BLOCKS):
        vall_ref[1 + j] = vc_ref[j]
    vall_ref[1 + SUB_BLOCKS] = vn_ref[0]

    pen_first = jnp.where(pos == 0, MASK_VALUE, 0.0)
    pen_last = jnp.where(pos == tiles_per_seq - 1, MASK_VALUE, 0.0)

    pair_lanes = 2 * HEAD_DIM
    slabs = [slice(r0, r0 + ROW_SLAB) for r0 in range(0, KEY_SPAN, ROW_SLAB)]
    heads = [(j, kvh) for j in range(SUB_BLOCKS) for kvh in range(N_KV_HEADS)]

    def scores(t):
        j, kvh = heads[t]
        pair, half = kvh // 2, kvh % 2
        q_heads = jnp.concatenate(
            [qt_ref[j, (kvh * Q_PER_KV + grp) * HEAD_DIM:(kvh * Q_PER_KV + grp + 1) * HEAD_DIM, :]
             for grp in range(Q_PER_KV)], axis=1)
        q_zero = jnp.zeros_like(q_heads)
        q2 = jnp.concatenate([q_heads, q_zero] if half == 0 else [q_zero, q_heads], axis=0)
        kwin = kall_ref[j * BLOCK:j * BLOCK + KEY_SPAN, pair * pair_lanes:(pair + 1) * pair_lanes]
        raw = _dot(kwin, q2)
        sc_buf = sc_ref.at[t % 2]
        maxes = []
        for grp in range(Q_PER_KV):
            hd = kvh * Q_PER_KV + grp
            lanes = slice(grp * BLOCK, (grp + 1) * BLOCK)
            mx = None
            for rows in slabs:
                s = raw[rows, lanes] + bias_ref[hd, rows, :]
                if j == 0 and rows.stop <= BLOCK:
                    s = s + pen_first
                if j == SUB_BLOCKS - 1 and rows.start >= 2 * BLOCK:
                    s = s + pen_last
                sc_buf[rows, lanes] = s
                mx = s if mx is None else jnp.maximum(mx, s)
            maxes.append(jnp.maximum(jnp.max(mx, axis=0, keepdims=True), sink_ref[hd] * LOG2_E))
        return maxes

    def softmax(t, maxes):
        j, kvh = heads[t]
        sc_buf = sc_ref.at[t % 2]
        p_buf = p_ref.at[t % 2]
        inv_l = []
        for grp in range(Q_PER_KV):
            hd = kvh * Q_PER_KV + grp
            lanes = slice(grp * BLOCK, (grp + 1) * BLOCK)
            m = maxes[grp]
            acc = None
            for rows in slabs:
                p = jnp.exp2(sc_buf[rows, lanes] - m)
                acc = p if acc is None else acc + p
                p_buf[rows, lanes] = p.astype(_BF16)
            l = jnp.sum(acc, axis=0, keepdims=True) + jnp.exp2(sink_ref[hd] * LOG2_E - m)
            inv_l.append(1.0 / l)
        return inv_l

    def weighted_values(t, inv_l):
        j, kvh = heads[t]
        vwin = jnp.concatenate(
            [vall_ref[j + b, kvh * HEAD_DIM:(kvh + 1) * HEAD_DIM, :] for b in range(3)], axis=1)
        ot = _dot(vwin, p_ref[t % 2])
        for grp in range(Q_PER_KV):
            hd = kvh * Q_PER_KV + grp
            at_ref[j, hd * HEAD_DIM:(hd + 1) * HEAD_DIM, :] = (
                ot[:, grp * BLOCK:(grp + 1) * BLOCK] * inv_l[grp]).astype(_BF16)

    def out_proj_piece(j, piece):
        rows = slice(j * BLOCK, (j + 1) * BLOCK)
        cols = slice(piece * OUT_PIECE, (piece + 1) * OUT_PIECE)
        o_ref[rows, cols] = x_ref[rows, cols] + _dot_tn(at_ref[j], wo_ref[:, cols])

    pieces_per_head = (D_MODEL // OUT_PIECE) // N_KV_HEADS
    maxes = {0: scores(0), 1: scores(1)}
    for t, (j, kvh) in enumerate(heads):
        inv_l = softmax(t, maxes.pop(t))
        weighted_values(t, inv_l)
        if j > 0:
            for piece in range(kvh * pieces_per_head, (kvh + 1) * pieces_per_head):
                out_proj_piece(j - 1, piece)
        if t + 2 < len(heads):
            maxes[t + 2] = scores(t + 2)
    for piece in range(D_MODEL // OUT_PIECE):
        out_proj_piece(SUB_BLOCKS - 1, piece)


def _t5_buckets(rel):
    nb = N_BUCKETS // 2
    ret = jnp.where(rel > 0, nb, 0)
    n = jnp.abs(rel)
    max_exact = nb // 2
    nf = jnp.maximum(n, 1).astype(jnp.float32)
    large = max_exact + (jnp.log(nf / max_exact) / math.log(MAX_DISTANCE / max_exact)
                         * (nb - max_exact)).astype(jnp.int32)
    large = jnp.minimum(large, nb - 1)
    return ret + jnp.where(n < max_exact, n, large)


def _attn_layer(x2, qt, k, vt, sink, rel_bias, w_out, seq_len):
    n, d = x2.shape
    t = TOKEN_TILE
    n_blocks = n // BLOCK
    kv_cols = N_KV_HEADS * HEAD_DIM
    q_cols = N_HEADS * HEAD_DIM
    kj = jnp.arange(KEY_SPAN, dtype=jnp.int32)[:, None]
    qi = jnp.arange(BLOCK, dtype=jnp.int32)[None, :]
    buckets_t = _t5_buckets(kj - BLOCK - qi).astype(jnp.int32)

    def prev_block(i):
        return jnp.maximum(i * SUB_BLOCKS - 1, 0)

    def next_block(i):
        return jnp.minimum((i + 1) * SUB_BLOCKS, n_blocks - 1)

    smem = pl.BlockSpec(memory_space=pltpu.SMEM)
    return pl.pallas_call(
        functools.partial(_attn_kernel, seq_len // t),
        grid=(n // t,),
        in_specs=[
            smem,
            smem,
            pl.BlockSpec((t, d), lambda i: (i, 0)),
            pl.BlockSpec((SUB_BLOCKS, q_cols, BLOCK), lambda i: (i, 0, 0)),
            pl.BlockSpec((t, kv_cols), lambda i: (i, 0)),
            pl.BlockSpec((BLOCK, kv_cols), lambda i: (prev_block(i), 0)),
            pl.BlockSpec((BLOCK, kv_cols), lambda i: (next_block(i), 0)),
            pl.BlockSpec((SUB_BLOCKS, kv_cols, BLOCK), lambda i: (i, 0, 0)),
            pl.BlockSpec((1, kv_cols, BLOCK), lambda i: (prev_block(i), 0, 0)),
            pl.BlockSpec((1, kv_cols, BLOCK), lambda i: (next_block(i), 0, 0)),
            _const_spec((KEY_SPAN, BLOCK)),
            _const_spec((d, d)),
        ],
        out_specs=pl.BlockSpec((t, d), lambda i: (i, 0)),
        out_shape=jax.ShapeDtypeStruct((n, d), _F32),
        scratch_shapes=[
            pltpu.VMEM((N_HEADS, KEY_SPAN, BLOCK), _F32),
            pltpu.VMEM((TOKEN_TILE + 2 * BLOCK, kv_cols), _BF16),
            pltpu.VMEM((SUB_BLOCKS + 2, kv_cols, BLOCK), _BF16),
            pltpu.VMEM((2, KEY_SPAN, Q_PER_KV * BLOCK), _F32),
            pltpu.VMEM((2, KEY_SPAN, Q_PER_KV * BLOCK), _BF16),
            pltpu.VMEM((SUB_BLOCKS, q_cols, BLOCK), _BF16),
        ],
        compiler_params=_params(),
        name="window_attention",
    )(sink, rel_bias.reshape(-1), x2, qt, k, k, k, vt, vt, vt, buckets_t, w_out.astype(_BF16))


def kernel(x, norm_mix, norm_ffn, even_w_in, even_v_ln_g, even_v_ln_b, even_w_spatial, even_b_spatial,
           even_conv_w, even_w_out, attn_w_qkv, attn_sink, rel_bias, attn_w_out, ffn_w_gate, ffn_w_up,
           ffn_w_down, final_norm):
    bsz, seq_len, d = x.shape
    assert d == D_MODEL and seq_len % WIDE_TILE == 0
    x2 = x.reshape(bsz * seq_len, d)
    wg, wu, wd = ffn_w_gate.astype(_BF16), ffn_w_up.astype(_BF16), ffn_w_down.astype(_BF16)
    x2 = _even_layer(x2, norm_mix[0], even_w_in[0], even_v_ln_g[0], even_v_ln_b[0], even_w_spatial[0],
                     even_b_spatial[0], even_conv_w[0], even_w_out[0], seq_len)
    x2 = _ffn_layer(x2, norm_ffn[0], wg, wu, wd, 0)
    qt, k, vt = _qkv_layer(x2, norm_mix[1], attn_w_qkv[0])
    x2 = _attn_layer(x2, qt, k, vt, attn_sink[0], rel_bias, attn_w_out[0], seq_len)
    x2 = _ffn_layer(x2, norm_ffn[1], wg, wu, wd, 1, final_g=final_norm)
    return x2.reshape(bsz, seq_len, d)
```

```python
import functools
import math

import jax
import jax.numpy as jnp
from jax import lax
from jax.experimental import pallas as pl
from jax.experimental.pallas import tpu as pltpu

D_MODEL = 1024
EPS = 1e-6
A_WIDTH = 512
B_WIDTH = 512
A_GROUPS = 4
A_GROUP_DIM = 128
CHUNK = 128
CONV_W = 3
N_HEADS = 16
N_KV_HEADS = 4
Q_PER_KV = 4
HEAD_DIM = 64
WINDOW = 128
BLOCK = 128
N_BUCKETS = 32
MAX_DISTANCE = 128
D_FF = 2816

TOKEN_TILE = 1024
WIDE_TILE = 1024
HALO = 16
EVEN_PARTS = 2
FF_CHUNK = 256
N_FF_CHUNKS = D_FF // FF_CHUNK
SUB_BLOCKS = TOKEN_TILE // BLOCK
KEY_SPAN = 3 * BLOCK
MASK_VALUE = float("-inf")
LOG2_E = math.log2(math.e)
ROW_SLAB = 64
OUT_PIECE = 256
SCORE_AHEAD = 2
V7X_VMEM_BYTES = 64 * 1024 * 1024
VMEM_LIMIT_BYTES = V7X_VMEM_BYTES * 7 // 8

_BF16 = jnp.bfloat16
_F32 = jnp.float32


def _dot(a, b):
    return jnp.dot(a, b, preferred_element_type=_F32)


def _dot_nt(a, b):
    return lax.dot_general(a, b, (((1,), (1,)), ((), ())), preferred_element_type=_F32)


def _dot_tn(a, b):
    return lax.dot_general(a, b, (((0,), (0,)), ((), ())), preferred_element_type=_F32)


def _gelu(x):
    return 0.5 * x * (1.0 + lax.erf(x * math.sqrt(0.5)))


def _rms(x, g):
    return (x * lax.rsqrt(jnp.mean(x * x, axis=-1, keepdims=True) + EPS)) * g


def _const_spec(shape):
    zeros = (0,) * len(shape)
    return pl.BlockSpec(shape, lambda i: zeros, pipeline_mode=pl.Buffered(1))


def _params():
    return pltpu.CompilerParams(
        dimension_semantics=("arbitrary",), vmem_limit_bytes=VMEM_LIMIT_BYTES)


def _even_kernel(tiles_per_seq, x_ref, xp_ref, xn_ref, g_ref, win_ref, lng_ref, lnb_ref,
                 wsp_ref, bsp_ref, cw_ref, wo_ref, o_ref, hs_ref, conv_ref, y_ref):
    t = WIDE_TILE
    i = pl.program_id(0)
    pos = i % tiles_per_seq
    g = g_ref[...]
    h = _rms(x_ref[...], g).astype(_BF16)
    hp = jnp.where(pos == 0, 0.0, _rms(xp_ref[...], g)).astype(_BF16)
    hn = jnp.where(pos == tiles_per_seq - 1, 0.0, _rms(xn_ref[...], g)).astype(_BF16)
    hs_ref[0:HALO, :] = hp
    hs_ref[HALO:HALO + t, :] = h
    hs_ref[HALO + t:, :] = hn

    in_width_a = 2 * A_WIDTH + B_WIDTH
    pc = _dot(hs_ref[...], win_ref[:, in_width_a:])

    z = pc[:, :B_WIDTH] * pc[:, B_WIDTH:]
    n_rows = t + 2 * HALO
    z_prev = pltpu.roll(z, 1, 0)[HALO:HALO + t]
    z_next = pltpu.roll(z, n_rows - 1, 0)[HALO:HALO + t]
    cw = cw_ref[...]
    conv_ref[...] = z_prev * cw[0:1] + z[HALO:HALO + t] * cw[1:2] + z_next * cw[2:3]

    part = t // EVEN_PARTS
    parts = [slice(q * part, (q + 1) * part) for q in range(EVEN_PARTS)]
    pas = [_dot(hs_ref[HALO + rows.start:HALO + rows.stop, :], win_ref[:, :in_width_a])
           for rows in parts]
    for rows, pa in zip(parts, pas):
        u = _gelu(pa[:, :A_WIDTH])
        v = _gelu(pa[:, A_WIDTH:2 * A_WIDTH])
        mu = jnp.mean(v, axis=-1, keepdims=True)
        vc = v - mu
        vn = vc * lax.rsqrt(jnp.mean(vc * vc, axis=-1, keepdims=True) + EPS)
        vb = (vn * lng_ref[...] + lnb_ref[...]).astype(_BF16)
        chunks = [slice(c * CHUNK, (c + 1) * CHUNK) for c in range(part // CHUNK)]
        for grp in range(A_GROUPS):
            cols = slice(grp * A_GROUP_DIM, (grp + 1) * A_GROUP_DIM)
            v_chunks = jnp.concatenate([vb[crows, cols] for crows in chunks], axis=1)
            mixed = _dot(wsp_ref[grp], v_chunks) + bsp_ref[grp]
            for c, crows in enumerate(chunks):
                orows = slice(rows.start + crows.start, rows.start + crows.stop)
                y_ref[orows, cols] = (u[crows, cols]
                                      * mixed[:, c * A_GROUP_DIM:(c + 1) * A_GROUP_DIM]).astype(_BF16)
        y_ref[rows, A_WIDTH:] = (pa[:, 2 * A_WIDTH:] * conv_ref[rows, :]).astype(_BF16)
        o_ref[rows, :] = x_ref[rows, :] + _dot(y_ref[rows, :], wo_ref[...])


def _even_layer(x2, g, w_in, ln_g, ln_b, w_sp, b_sp, conv_w, w_out, seq_len):
    n, d = x2.shape
    t = WIDE_TILE
    halo_blocks_per_tile = t // HALO
    n_halo_blocks = n // HALO
    return pl.pallas_call(
        functools.partial(_even_kernel, seq_len // t),
        grid=(n // t,),
        in_specs=[
            pl.BlockSpec((t, d), lambda i: (i, 0)),
            pl.BlockSpec((HALO, d), lambda i: (jnp.maximum(i * halo_blocks_per_tile - 1, 0), 0)),
            pl.BlockSpec((HALO, d),
                         lambda i: (jnp.minimum((i + 1) * halo_blocks_per_tile, n_halo_blocks - 1), 0)),
            _const_spec((1, d)),
            _const_spec(w_in.shape),
            _const_spec((1, A_WIDTH)),
            _const_spec((1, A_WIDTH)),
            _const_spec((A_GROUPS, CHUNK, CHUNK)),
            _const_spec((A_GROUPS, CHUNK, 1)),
            _const_spec((CONV_W, B_WIDTH)),
            _const_spec((d, d)),
        ],
        out_specs=pl.BlockSpec((t, d), lambda i: (i, 0)),
        out_shape=jax.ShapeDtypeStruct((n, d), _F32),
        scratch_shapes=[
            pltpu.VMEM((t + 2 * HALO, d), _BF16),
            pltpu.VMEM((t, B_WIDTH), _F32),
            pltpu.VMEM((t, d), _BF16),
        ],
        compiler_params=_params(),
        name="even_mixer",
    )(x2, x2, x2, g.reshape(1, d), w_in.astype(_BF16), ln_g.reshape(1, -1), ln_b.reshape(1, -1),
      w_sp.astype(_BF16), b_sp[..., None], conv_w, w_out.astype(_BF16))


def _ffn_kernel(has_final_norm, x_ref, g_ref, wg_ref, wu_ref, wd_ref, *rest):
    if has_final_norm:
        gf_ref, o_ref, h_ref, acc_ref = rest
    else:
        o_ref, h_ref, acc_ref = rest
    x = x_ref[...]
    h_ref[...] = _rms(x, g_ref[...]).astype(_BF16)
    acc_ref[...] = x

    h = h_ref[...]
    for c in range(N_FF_CHUNKS):
        cols = slice(c * FF_CHUNK, (c + 1) * FF_CHUNK)
        gate = _dot(h, wg_ref[:, cols])
        up = _dot(h, wu_ref[:, cols])
        act = (jax.nn.silu(gate) * up).astype(_BF16)
        acc_ref[...] += _dot(act, wd_ref[cols, :])
    out = acc_ref[...]
    if has_final_norm:
        out = _rms(out, gf_ref[...])
    o_ref[...] = out


def _layer_spec(stacked_shape, layer):
    zeros = (0,) * (len(stacked_shape) - 1)
    return pl.BlockSpec((None,) + tuple(stacked_shape[1:]), lambda i: (layer,) + zeros,
                        pipeline_mode=pl.Buffered(1))


def _ffn_layer(x2, g, wg, wu, wd, layer, final_g=None):
    n, d = x2.shape
    t = WIDE_TILE
    in_specs = [
        pl.BlockSpec((t, d), lambda i: (i, 0)),
        _const_spec((1, d)),
        _layer_spec(wg.shape, layer),
        _layer_spec(wu.shape, layer),
        _layer_spec(wd.shape, layer),
    ]
    args = [x2, g.reshape(1, d), wg, wu, wd]
    if final_g is not None:
        in_specs.append(_const_spec((1, d)))
        args.append(final_g.reshape(1, d))
    return pl.pallas_call(
        functools.partial(_ffn_kernel, final_g is not None),
        grid=(n // t,),
        in_specs=in_specs,
        out_specs=pl.BlockSpec((t, d), lambda i: (i, 0)),
        out_shape=jax.ShapeDtypeStruct((n, d), _F32),
        scratch_shapes=[pltpu.VMEM((t, d), _BF16), pltpu.VMEM((t, d), _F32)],
        compiler_params=_params(),
        name="swiglu_ffn",
    )(*args)


def _qkv_kernel(x_ref, g_ref, wqv_ref, wk_ref, qt_ref, k_ref, vt_ref):
    h = _rms(x_ref[...], g_ref[...]).astype(_BF16)
    qv = _dot_nt(wqv_ref[...], h)
    q_rows = N_HEADS * HEAD_DIM
    scale = HEAD_DIM ** -0.5 * LOG2_E
    for j in range(WIDE_TILE // BLOCK):
        lanes = slice(j * BLOCK, (j + 1) * BLOCK)
        qt_ref[j] = (qv[:q_rows, lanes] * scale).astype(_BF16)
        vt_ref[j] = qv[q_rows:, lanes].astype(_BF16)
    k_ref[...] = _dot(h, wk_ref[...]).astype(_BF16)


def _qkv_layer(x2, g, w_qkv):
    n, d = x2.shape
    t = WIDE_TILE
    q_cols = N_HEADS * HEAD_DIM
    kv_cols = N_KV_HEADS * HEAD_DIM
    wq = w_qkv[:, :q_cols]
    wk = w_qkv[:, q_cols:q_cols + kv_cols]
    wv = w_qkv[:, q_cols + kv_cols:]
    wqv_t = jnp.concatenate([wq, wv], axis=1).T.astype(_BF16)
    n_blocks = n // BLOCK
    return pl.pallas_call(
        _qkv_kernel,
        grid=(n // t,),
        in_specs=[
            pl.BlockSpec((t, d), lambda i: (i, 0)),
            _const_spec((1, d)),
            _const_spec(wqv_t.shape),
            _const_spec((d, kv_cols)),
        ],
        out_specs=[
            pl.BlockSpec((t // BLOCK, q_cols, BLOCK), lambda i: (i, 0, 0)),
            pl.BlockSpec((t, kv_cols), lambda i: (i, 0)),
            pl.BlockSpec((t // BLOCK, kv_cols, BLOCK), lambda i: (i, 0, 0)),
        ],
        out_shape=[
            jax.ShapeDtypeStruct((n_blocks, q_cols, BLOCK), _BF16),
            jax.ShapeDtypeStruct((n, kv_cols), _BF16),
            jax.ShapeDtypeStruct((n_blocks, kv_cols, BLOCK), _BF16),
        ],
        compiler_params=_params(),
        name="qkv_proj",
    )(x2, g.reshape(1, d), wqv_t, wk.astype(_BF16))


def _t5_log_thresholds():
    nb = N_BUCKETS // 2
    max_exact = nb // 2
    steps = nb - max_exact
    ratio = MAX_DISTANCE // max_exact
    assert ratio * max_exact == MAX_DISTANCE
    thresholds = []
    n = max_exact
    for k in range(1, steps):
        while n ** steps < max_exact ** steps * ratio ** k:
            n += 1
        thresholds.append(n)
    return thresholds


def _t5_buckets(rel):
    nb = N_BUCKETS // 2
    max_exact = nb // 2
    n = jnp.abs(rel)
    large = jnp.full(rel.shape, max_exact, jnp.int32)
    for thr in _t5_log_thresholds():
        large = large + (n >= thr).astype(jnp.int32)
    return jnp.where(rel > 0, nb, 0) + jnp.where(n < max_exact, n, large)


def _attn_kernel(tiles_per_seq, sink_ref, rb_ref, x_ref, qt_ref, kc_ref, kp_ref, kn_ref,
                 vc_ref, vp_ref, vn_ref, wo_ref, o_ref,
                 bias_ref, kall_ref, vall_ref, sc_ref, p_ref, at_ref):
    i = pl.program_id(0)
    pos = i % tiles_per_seq

    @pl.when(i == 0)
    def _init():
        rel = (lax.broadcasted_iota(jnp.int32, (KEY_SPAN, BLOCK), 0) - BLOCK
               - lax.broadcasted_iota(jnp.int32, (KEY_SPAN, BLOCK), 1))
        band = jnp.abs(rel) <= WINDOW
        bkt = _t5_buckets(rel)

        def fill_head(hd, carry):
            tbl = jnp.zeros((KEY_SPAN, BLOCK), _F32)
            for b in range(N_BUCKETS):
                tbl = jnp.where(bkt == b, rb_ref[b * N_HEADS + hd], tbl)
            bias_ref[hd] = jnp.where(band, tbl * LOG2_E, MASK_VALUE)
            return carry

        lax.fori_loop(0, N_HEADS, fill_head, 0)

    kall_ref[0:BLOCK, :] = kp_ref[...]
    kall_ref[BLOCK:BLOCK + TOKEN_TILE, :] = kc_ref[...]
    kall_ref[BLOCK + TOKEN_TILE:, :] = kn_ref[...]
    vall_ref[0] = vp_ref[0]
    for j in range(SUB_BLOCKS):
        vall_ref[1 + j] = vc_ref[j]
    vall_ref[1 + SUB_BLOCKS] = vn_ref[0]

    pen_first = jnp.where(pos == 0, MASK_VALUE, 0.0)
    pen_last = jnp.where(pos == tiles_per_seq - 1, MASK_VALUE, 0.0)

    pair_lanes = 2 * HEAD_DIM
    slabs = [slice(r0, r0 + ROW_SLAB) for r0 in range(0, KEY_SPAN, ROW_SLAB)]
    heads = [(j, kvh) for j in range(SUB_BLOCKS) for kvh in range(N_KV_HEADS)]

    def scores(t):
        j, kvh = heads[t]
        pair, half = kvh // 2, kvh % 2
        q_heads = jnp.concatenate(
            [qt_ref[j, (kvh * Q_PER_KV + grp) * HEAD_DIM:(kvh * Q_PER_KV + grp + 1) * HEAD_DIM, :]
             for grp in range(Q_PER_KV)], axis=1)
        q_zero = jnp.zeros_like(q_heads)
        q2 = jnp.concatenate([q_heads, q_zero] if half == 0 else [q_zero, q_heads], axis=0)
        kwin = kall_ref[j * BLOCK:j * BLOCK + KEY_SPAN, pair * pair_lanes:(pair + 1) * pair_lanes]
        raw = _dot(kwin, q2)
        sc_buf = sc_ref.at[t % SCORE_AHEAD]
        maxes = []
        for grp in range(Q_PER_KV):
            hd = kvh * Q_PER_KV + grp
            lanes = slice(grp * BLOCK, (grp + 1) * BLOCK)
            mx = None
            for rows in slabs:
                s = raw[rows, lanes] + bias_ref[hd, rows, :]
                if j == 0 and rows.stop <= BLOCK:
                    s = s + pen_first
                if j == SUB_BLOCKS - 1 and rows.start >= 2 * BLOCK:
                    s = s + pen_last
                sc_buf[rows, lanes] = s
                mx = s if mx is None else jnp.maximum(mx, s)
            maxes.append(jnp.maximum(jnp.max(mx, axis=0, keepdims=True), sink_ref[hd] * LOG2_E))
        return maxes

    def softmax(t, maxes):
        j, kvh = heads[t]
        sc_buf = sc_ref.at[t % SCORE_AHEAD]
        p_buf = p_ref.at[t % SCORE_AHEAD]
        inv_l = []
        for grp in range(Q_PER_KV):
            hd = kvh * Q_PER_KV + grp
            lanes = slice(grp * BLOCK, (grp + 1) * BLOCK)
            m = maxes[grp]
            acc = None
            for rows in slabs:
                p = jnp.exp2(sc_buf[rows, lanes] - m)
                acc = p if acc is None else acc + p
                p_buf[rows, lanes] = p.astype(_BF16)
            l = jnp.sum(acc, axis=0, keepdims=True) + jnp.exp2(sink_ref[hd] * LOG2_E - m)
            inv_l.append(1.0 / l)
        return inv_l

    def weighted_values(t, inv_l):
        j, kvh = heads[t]
        vwin = jnp.concatenate(
            [vall_ref[j + b, kvh * HEAD_DIM:(kvh + 1) * HEAD_DIM, :] for b in range(3)], axis=1)
        ot = _dot(vwin, p_ref[t % SCORE_AHEAD])
        for grp in range(Q_PER_KV):
            hd = kvh * Q_PER_KV + grp
            at_ref[j, hd * HEAD_DIM:(hd + 1) * HEAD_DIM, :] = (
                ot[:, grp * BLOCK:(grp + 1) * BLOCK] * inv_l[grp]).astype(_BF16)

    def out_proj_piece(j, piece):
        rows = slice(j * BLOCK, (j + 1) * BLOCK)
        cols = slice(piece * OUT_PIECE, (piece + 1) * OUT_PIECE)
        o_ref[rows, cols] = x_ref[rows, cols] + _dot_tn(at_ref[j], wo_ref[:, cols])

    n_pieces = D_MODEL // OUT_PIECE
    maxes = {t: scores(t) for t in range(SCORE_AHEAD)}
    for t, (j, kvh) in enumerate(heads):
        inv_l = softmax(t, maxes.pop(t))
        weighted_values(t, inv_l)
        if j > 0:
            for piece in range(n_pieces):
                if (piece * N_KV_HEADS) // n_pieces == kvh:
                    out_proj_piece(j - 1, piece)
        if t + SCORE_AHEAD < len(heads):
            maxes[t + SCORE_AHEAD] = scores(t + SCORE_AHEAD)
    for piece in range(D_MODEL // OUT_PIECE):
        out_proj_piece(SUB_BLOCKS - 1, piece)


def _attn_layer(x2, qt, k, vt, sink, rel_bias, w_out, seq_len):
    n, d = x2.shape
    t = TOKEN_TILE
    n_blocks = n // BLOCK
    kv_cols = N_KV_HEADS * HEAD_DIM
    q_cols = N_HEADS * HEAD_DIM

    def prev_block(i):
        return jnp.maximum(i * SUB_BLOCKS - 1, 0)

    def next_block(i):
        return jnp.minimum((i + 1) * SUB_BLOCKS, n_blocks - 1)

    smem = pl.BlockSpec(memory_space=pltpu.SMEM)
    return pl.pallas_call(
        functools.partial(_attn_kernel, seq_len // t),
        grid=(n // t,),
        in_specs=[
            smem,
            smem,
            pl.BlockSpec((t, d), lambda i: (i, 0)),
            pl.BlockSpec((SUB_BLOCKS, q_cols, BLOCK), lambda i: (i, 0, 0)),
            pl.BlockSpec((t, kv_cols), lambda i: (i, 0)),
            pl.BlockSpec((BLOCK, kv_cols), lambda i: (prev_block(i), 0)),
            pl.BlockSpec((BLOCK, kv_cols), lambda i: (next_block(i), 0)),
            pl.BlockSpec((SUB_BLOCKS, kv_cols, BLOCK), lambda i: (i, 0, 0)),
            pl.BlockSpec((1, kv_cols, BLOCK), lambda i: (prev_block(i), 0, 0)),
            pl.BlockSpec((1, kv_cols, BLOCK), lambda i: (next_block(i), 0, 0)),
            _const_spec((d, d)),
        ],
        out_specs=pl.BlockSpec((t, d), lambda i: (i, 0)),
        out_shape=jax.ShapeDtypeStruct((n, d), _F32),
        scratch_shapes=[
            pltpu.VMEM((N_HEADS, KEY_SPAN, BLOCK), _F32),
            pltpu.VMEM((TOKEN_TILE + 2 * BLOCK, kv_cols), _BF16),
            pltpu.VMEM((SUB_BLOCKS + 2, kv_cols, BLOCK), _BF16),
            pltpu.VMEM((SCORE_AHEAD, KEY_SPAN, Q_PER_KV * BLOCK), _F32),
            pltpu.VMEM((SCORE_AHEAD, KEY_SPAN, Q_PER_KV * BLOCK), _BF16),
            pltpu.VMEM((SUB_BLOCKS, q_cols, BLOCK), _BF16),
        ],
        compiler_params=_params(),
        name="window_attention",
    )(sink, rel_bias.reshape(-1), x2, qt, k, k, k, vt, vt, vt, w_out.astype(_BF16))


def kernel(x, norm_mix, norm_ffn, even_w_in, even_v_ln_g, even_v_ln_b, even_w_spatial, even_b_spatial,
           even_conv_w, even_w_out, attn_w_qkv, attn_sink, rel_bias, attn_w_out, ffn_w_gate, ffn_w_up,
           ffn_w_down, final_norm):
    bsz, seq_len, d = x.shape
    assert d == D_MODEL and seq_len % WIDE_TILE == 0
    x2 = x.reshape(bsz * seq_len, d)
    wg, wu, wd = ffn_w_gate.astype(_BF16), ffn_w_up.astype(_BF16), ffn_w_down.astype(_BF16)
    x2 = _even_layer(x2, norm_mix[0], even_w_in[0], even_v_ln_g[0], even_v_ln_b[0], even_w_spatial[0],
                     even_b_spatial[0], even_conv_w[0], even_w_out[0], seq_len)
    x2 = _ffn_layer(x2, norm_ffn[0], wg, wu, wd, 0)
    qt, k, vt = _qkv_layer(x2, norm_mix[1], attn_w_qkv[0])
    x2 = _attn_layer(x2, qt, k, vt, attn_sink[0], rel_bias, attn_w_out[0], seq_len)
    x2 = _ffn_layer(x2, norm_ffn[1], wg, wu, wd, 1, final_g=final_norm)
    return x2.reshape(bsz, seq_len, d)
```

```python
import functools
import math

import jax
import jax.numpy as jnp
from jax import lax
from jax.experimental import pallas as pl
from jax.experimental.pallas import tpu as pltpu

D_MODEL = 1024
EPS = 1e-6
A_WIDTH = 512
B_WIDTH = 512
A_GROUPS = 4
A_GROUP_DIM = 128
CHUNK = 128
CONV_W = 3
N_HEADS = 16
N_KV_HEADS = 4
Q_PER_KV = 4
HEAD_DIM = 64
WINDOW = 128
BLOCK = 128
N_BUCKETS = 32
MAX_DISTANCE = 128
D_FF = 2816

TOKEN_TILE = 1024
WIDE_TILE = 1024
QKV_TILE = 2048
HALO = 16
EVEN_PARTS = 2
FF_CHUNK = 256
N_FF_CHUNKS = D_FF // FF_CHUNK
DOWN_GROUP = 2
SUB_BLOCKS = TOKEN_TILE // BLOCK
KEY_SPAN = 3 * BLOCK
MASK_VALUE = float("-inf")
LOG2_E = math.log2(math.e)
ROW_SLAB = 64
OUT_PIECE = 256
SCORE_AHEAD = 2
V7X_VMEM_BYTES = 64 * 1024 * 1024
VMEM_LIMIT_BYTES = V7X_VMEM_BYTES * 7 // 8

_BF16 = jnp.bfloat16
_F32 = jnp.float32


def _dot(a, b):
    return jnp.dot(a, b, preferred_element_type=_F32)


def _dot_nt(a, b):
    return lax.dot_general(a, b, (((1,), (1,)), ((), ())), preferred_element_type=_F32)


def _dot_tn(a, b):
    return lax.dot_general(a, b, (((0,), (0,)), ((), ())), preferred_element_type=_F32)


def _gelu(x):
    return 0.5 * x * (1.0 + lax.erf(x * math.sqrt(0.5)))


def _rms(x, g):
    return (x * lax.rsqrt(jnp.mean(x * x, axis=-1, keepdims=True) + EPS)) * g


def _const_spec(shape):
    zeros = (0,) * len(shape)
    return pl.BlockSpec(shape, lambda i: zeros, pipeline_mode=pl.Buffered(1))


def _params():
    return pltpu.CompilerParams(
        dimension_semantics=("arbitrary",), vmem_limit_bytes=VMEM_LIMIT_BYTES)


def _even_kernel(tiles_per_seq, x_ref, xp_ref, xn_ref, g_ref, win_ref, lng_ref, lnb_ref,
                 wsp_ref, bsp_ref, cw_ref, wo_ref, o_ref, hs_ref, conv_ref, y_ref):
    t = WIDE_TILE
    i = pl.program_id(0)
    pos = i % tiles_per_seq
    g = g_ref[...]
    h = _rms(x_ref[...], g).astype(_BF16)
    hp = jnp.where(pos == 0, 0.0, _rms(xp_ref[...], g)).astype(_BF16)
    hn = jnp.where(pos == tiles_per_seq - 1, 0.0, _rms(xn_ref[...], g)).astype(_BF16)
    hs_ref[0:HALO, :] = hp
    hs_ref[HALO:HALO + t, :] = h
    hs_ref[HALO + t:, :] = hn

    in_width_a = 2 * A_WIDTH + B_WIDTH
    pc = _dot(hs_ref[...], win_ref[:, in_width_a:])

    z = pc[:, :B_WIDTH] * pc[:, B_WIDTH:]
    n_rows = t + 2 * HALO
    z_prev = pltpu.roll(z, 1, 0)[HALO:HALO + t]
    z_next = pltpu.roll(z, n_rows - 1, 0)[HALO:HALO + t]
    cw = cw_ref[...]
    conv_ref[...] = z_prev * cw[0:1] + z[HALO:HALO + t] * cw[1:2] + z_next * cw[2:3]

    part = t // EVEN_PARTS
    parts = [slice(q * part, (q + 1) * part) for q in range(EVEN_PARTS)]
    pas = [_dot(hs_ref[HALO + rows.start:HALO + rows.stop, :], win_ref[:, :in_width_a])
           for rows in parts]
    for rows, pa in zip(parts, pas):
        u = _gelu(pa[:, :A_WIDTH])
        v = _gelu(pa[:, A_WIDTH:2 * A_WIDTH])
        mu = jnp.mean(v, axis=-1, keepdims=True)
        vc = v - mu
        vn = vc * lax.rsqrt(jnp.mean(vc * vc, axis=-1, keepdims=True) + EPS)
        vb = (vn * lng_ref[...] + lnb_ref[...]).astype(_BF16)
        chunks = [slice(c * CHUNK, (c + 1) * CHUNK) for c in range(part // CHUNK)]
        for grp in range(A_GROUPS):
            cols = slice(grp * A_GROUP_DIM, (grp + 1) * A_GROUP_DIM)
            v_chunks = jnp.concatenate([vb[crows, cols] for crows in chunks], axis=1)
            mixed = _dot(wsp_ref[grp], v_chunks) + bsp_ref[grp]
            for c, crows in enumerate(chunks):
                orows = slice(rows.start + crows.start, rows.start + crows.stop)
                y_ref[orows, cols] = (u[crows, cols]
                                      * mixed[:, c * A_GROUP_DIM:(c + 1) * A_GROUP_DIM]).astype(_BF16)
        y_ref[rows, A_WIDTH:] = (pa[:, 2 * A_WIDTH:] * conv_ref[rows, :]).astype(_BF16)
        o_ref[rows, :] = x_ref[rows, :] + _dot(y_ref[rows, :], wo_ref[...])


def _even_layer(x2, g, w_in, ln_g, ln_b, w_sp, b_sp, conv_w, w_out, seq_len):
    n, d = x2.shape
    t = WIDE_TILE
    halo_blocks_per_tile = t // HALO
    n_halo_blocks = n // HALO
    return pl.pallas_call(
        functools.partial(_even_kernel, seq_len // t),
        grid=(n // t,),
        in_specs=[
            pl.BlockSpec((t, d), lambda i: (i, 0)),
            pl.BlockSpec((HALO, d), lambda i: (jnp.maximum(i * halo_blocks_per_tile - 1, 0), 0)),
            pl.BlockSpec((HALO, d),
                         lambda i: (jnp.minimum((i + 1) * halo_blocks_per_tile, n_halo_blocks - 1), 0)),
            _const_spec((1, d)),
            _const_spec(w_in.shape),
            _const_spec((1, A_WIDTH)),
            _const_spec((1, A_WIDTH)),
            _const_spec((A_GROUPS, CHUNK, CHUNK)),
            _const_spec((A_GROUPS, CHUNK, 1)),
            _const_spec((CONV_W, B_WIDTH)),
            _const_spec((d, d)),
        ],
        out_specs=pl.BlockSpec((t, d), lambda i: (i, 0)),
        out_shape=jax.ShapeDtypeStruct((n, d), _F32),
        scratch_shapes=[
            pltpu.VMEM((t + 2 * HALO, d), _BF16),
            pltpu.VMEM((t, B_WIDTH), _F32),
            pltpu.VMEM((t, d), _BF16),
        ],
        compiler_params=_params(),
        name="even_mixer",
    )(x2, x2, x2, g.reshape(1, d), w_in.astype(_BF16), ln_g.reshape(1, -1), ln_b.reshape(1, -1),
      w_sp.astype(_BF16), b_sp[..., None], conv_w, w_out.astype(_BF16))


def _ffn_kernel(has_final_norm, x_ref, g_ref, wg_ref, wu_ref, wd_ref, *rest):
    if has_final_norm:
        gf_ref, o_ref, h_ref, acc_ref = rest
    else:
        o_ref, h_ref, acc_ref = rest
    x = x_ref[...]
    h_ref[...] = _rms(x, g_ref[...]).astype(_BF16)
    acc_ref[...] = x

    h = h_ref[...]
    pending = []
    for c in range(N_FF_CHUNKS):
        cols = slice(c * FF_CHUNK, (c + 1) * FF_CHUNK)
        gate = _dot(h, wg_ref[:, cols])
        up = _dot(h, wu_ref[:, cols])
        pending.append((jax.nn.silu(gate) * up).astype(_BF16))
        if len(pending) == DOWN_GROUP or c == N_FF_CHUNKS - 1:
            rows = slice((c + 1 - len(pending)) * FF_CHUNK, (c + 1) * FF_CHUNK)
            act = pending[0] if len(pending) == 1 else jnp.concatenate(pending, axis=1)
            acc_ref[...] += _dot(act, wd_ref[rows, :])
            pending = []
    out = acc_ref[...]
    if has_final_norm:
        out = _rms(out, gf_ref[...])
    o_ref[...] = out


def _layer_spec(stacked_shape, layer):
    zeros = (0,) * (len(stacked_shape) - 1)
    return pl.BlockSpec((None,) + tuple(stacked_shape[1:]), lambda i: (layer,) + zeros,
                        pipeline_mode=pl.Buffered(1))


def _ffn_layer(x2, g, wg, wu, wd, layer, final_g=None):
    n, d = x2.shape
    t = WIDE_TILE
    in_specs = [
        pl.BlockSpec((t, d), lambda i: (i, 0)),
        _const_spec((1, d)),
        _layer_spec(wg.shape, layer),
        _layer_spec(wu.shape, layer),
        _layer_spec(wd.shape, layer),
    ]
    args = [x2, g.reshape(1, d), wg, wu, wd]
    if final_g is not None:
        in_specs.append(_const_spec((1, d)))
        args.append(final_g.reshape(1, d))
    return pl.pallas_call(
        functools.partial(_ffn_kernel, final_g is not None),
        grid=(n // t,),
        in_specs=in_specs,
        out_specs=pl.BlockSpec((t, d), lambda i: (i, 0)),
        out_shape=jax.ShapeDtypeStruct((n, d), _F32),
        scratch_shapes=[pltpu.VMEM((t, d), _BF16), pltpu.VMEM((t, d), _F32)],
        compiler_params=_params(),
        name="swiglu_ffn",
    )(*args)


def _qkv_kernel(x_ref, g_ref, wqv_ref, wk_ref, qt_ref, k_ref, vt_ref):
    h = _rms(x_ref[...], g_ref[...]).astype(_BF16)
    qv = _dot_nt(wqv_ref[...], h)
    q_rows = N_HEADS * HEAD_DIM
    scale = HEAD_DIM ** -0.5 * LOG2_E
    for j in range(x_ref.shape[0] // BLOCK):
        lanes = slice(j * BLOCK, (j + 1) * BLOCK)
        qt_ref[j] = (qv[:q_rows, lanes] * scale).astype(_BF16)
        vt_ref[j] = qv[q_rows:, lanes].astype(_BF16)
    k_ref[...] = _dot(h, wk_ref[...]).astype(_BF16)


def _qkv_layer(x2, g, w_qkv):
    n, d = x2.shape
    t = QKV_TILE
    q_cols = N_HEADS * HEAD_DIM
    kv_cols = N_KV_HEADS * HEAD_DIM
    wq = w_qkv[:, :q_cols]
    wk = w_qkv[:, q_cols:q_cols + kv_cols]
    wv = w_qkv[:, q_cols + kv_cols:]
    wqv_t = jnp.concatenate([wq, wv], axis=1).T.astype(_BF16)
    n_blocks = n // BLOCK
    return pl.pallas_call(
        _qkv_kernel,
        grid=(n // t,),
        in_specs=[
            pl.BlockSpec((t, d), lambda i: (i, 0)),
            _const_spec((1, d)),
            _const_spec(wqv_t.shape),
            _const_spec((d, kv_cols)),
        ],
        out_specs=[
            pl.BlockSpec((t // BLOCK, q_cols, BLOCK), lambda i: (i, 0, 0)),
            pl.BlockSpec((t, kv_cols), lambda i: (i, 0)),
            pl.BlockSpec((t // BLOCK, kv_cols, BLOCK), lambda i: (i, 0, 0)),
        ],
        out_shape=[
            jax.ShapeDtypeStruct((n_blocks, q_cols, BLOCK), _BF16),
            jax.ShapeDtypeStruct((n, kv_cols), _BF16),
            jax.ShapeDtypeStruct((n_blocks, kv_cols, BLOCK), _BF16),
        ],
        compiler_params=_params(),
        name="qkv_proj",
    )(x2, g.reshape(1, d), wqv_t, wk.astype(_BF16))


def _t5_log_thresholds():
    nb = N_BUCKETS // 2
    max_exact = nb // 2
    steps = nb - max_exact
    ratio = MAX_DISTANCE // max_exact
    assert ratio * max_exact == MAX_DISTANCE
    thresholds = []
    n = max_exact
    for k in range(1, steps):
        while n ** steps < max_exact ** steps * ratio ** k:
            n += 1
        thresholds.append(n)
    return thresholds


def _t5_buckets(rel):
    nb = N_BUCKETS // 2
    max_exact = nb // 2
    n = jnp.abs(rel)
    large = jnp.full(rel.shape, max_exact, jnp.int32)
    for thr in _t5_log_thresholds():
        large = large + (n >= thr).astype(jnp.int32)
    return jnp.where(rel > 0, nb, 0) + jnp.where(n < max_exact, n, large)


def _attn_kernel(tiles_per_seq, sink_ref, rb_ref, x_ref, qt_ref, kc_ref, kp_ref, kn_ref,
                 vc_ref, vp_ref, vn_ref, wo_ref, o_ref,
                 bias_ref, kall_ref, vall_ref, sc_ref, p_ref, at_ref):
    i = pl.program_id(0)
    pos = i % tiles_per_seq

    @pl.when(i == 0)
    def _init():
        rel = (lax.broadcasted_iota(jnp.int32, (KEY_SPAN, BLOCK), 0) - BLOCK
               - lax.broadcasted_iota(jnp.int32, (KEY_SPAN, BLOCK), 1))
        band = jnp.abs(rel) <= WINDOW
        bkt = _t5_buckets(rel)

        def fill_head(hd, carry):
            tbl = jnp.zeros((KEY_SPAN, BLOCK), _F32)
            for b in range(N_BUCKETS):
                tbl = jnp.where(bkt == b, rb_ref[b * N_HEADS + hd], tbl)
            bias_ref[hd] = jnp.where(band, tbl * LOG2_E, MASK_VALUE)
            return carry

        lax.fori_loop(0, N_HEADS, fill_head, 0)

    kall_ref[0:BLOCK, :] = kp_ref[...]
    kall_ref[BLOCK:BLOCK + TOKEN_TILE, :] = kc_ref[...]
    kall_ref[BLOCK + TOKEN_TILE:, :] = kn_ref[...]
    vall_ref[0] = vp_ref[0]
    for j in range(SUB_BLOCKS):
        vall_ref[1 + j] = vc_ref[j]
    vall_ref[1 + SUB_BLOCKS] = vn_ref[0]

    pen_first = jnp.where(pos == 0, MASK_VALUE, 0.0)
    pen_last = jnp.where(pos == tiles_per_seq - 1, MASK_VALUE, 0.0)

    pair_lanes = 2 * HEAD_DIM
    slabs = [slice(r0, r0 + ROW_SLAB) for r0 in range(0, KEY_SPAN, ROW_SLAB)]
    heads = [(j, kvh) for j in range(SUB_BLOCKS) for kvh in range(N_KV_HEADS)]

    def scores(t):
        j, kvh = heads[t]
        pair, half = kvh // 2, kvh % 2
        q_heads = jnp.concatenate(
            [qt_ref[j, (kvh * Q_PER_KV + grp) * HEAD_DIM:(kvh * Q_PER_KV + grp + 1) * HEAD_DIM, :]
             for grp in range(Q_PER_KV)], axis=1)
        q_zero = jnp.zeros_like(q_heads)
        q2 = jnp.concatenate([q_heads, q_zero] if half == 0 else [q_zero, q_heads], axis=0)
        kwin = kall_ref[j * BLOCK:j * BLOCK + KEY_SPAN, pair * pair_lanes:(pair + 1) * pair_lanes]
        raw = _dot(kwin, q2)
        sc_buf = sc_ref.at[t % SCORE_AHEAD]
        maxes = []
        for grp in range(Q_PER_KV):
            hd = kvh * Q_PER_KV + grp
            lanes = slice(grp * BLOCK, (grp + 1) * BLOCK)
            mx = None
            for rows in slabs:
                s = raw[rows, lanes] + bias_ref[hd, rows, :]
                if j == 0 and rows.stop <= BLOCK:
                    s = s + pen_first
                if j == SUB_BLOCKS - 1 and rows.start >= 2 * BLOCK:
                    s = s + pen_last
                sc_buf[rows, lanes] = s
                mx = s if mx is None else jnp.maximum(mx, s)
            maxes.append(jnp.maximum(jnp.max(mx, axis=0, keepdims=True), sink_ref[hd] * LOG2_E))
        return maxes

    def softmax(t, maxes):
        j, kvh = heads[t]
        sc_buf = sc_ref.at[t % SCORE_AHEAD]
        p_buf = p_ref.at[t % SCORE_AHEAD]
        inv_l = []
        for grp in range(Q_PER_KV):
            hd = kvh * Q_PER_KV + grp
            lanes = slice(grp * BLOCK, (grp + 1) * BLOCK)
            m = maxes[grp]
            acc = None
            for rows in slabs:
                p = jnp.exp2(sc_buf[rows, lanes] - m)
                acc = p if acc is None else acc + p
                p_buf[rows, lanes] = p.astype(_BF16)
            l = jnp.sum(acc, axis=0, keepdims=True) + jnp.exp2(sink_ref[hd] * LOG2_E - m)
            inv_l.append(1.0 / l)
        return inv_l

    def weighted_values(t, inv_l):
        j, kvh = heads[t]
        vwin = jnp.concatenate(
            [vall_ref[j + b, kvh * HEAD_DIM:(kvh + 1) * HEAD_DIM, :] for b in range(3)], axis=1)
        ot = _dot(vwin, p_ref[t % SCORE_AHEAD])
        for grp in range(Q_PER_KV):
            hd = kvh * Q_PER_KV + grp
            at_ref[j, hd * HEAD_DIM:(hd + 1) * HEAD_DIM, :] = (
                ot[:, grp * BLOCK:(grp + 1) * BLOCK] * inv_l[grp]).astype(_BF16)

    def out_proj_piece(j, piece):
        rows = slice(j * BLOCK, (j + 1) * BLOCK)
        cols = slice(piece * OUT_PIECE, (piece + 1) * OUT_PIECE)
        o_ref[rows, cols] = x_ref[rows, cols] + _dot_tn(at_ref[j], wo_ref[:, cols])

    n_pieces = D_MODEL // OUT_PIECE
    maxes = {t: scores(t) for t in range(SCORE_AHEAD)}
    for t, (j, kvh) in enumerate(heads):
        inv_l = softmax(t, maxes.pop(t))
        weighted_values(t, inv_l)
        if j > 0:
            for piece in range(n_pieces):
                if (piece * N_KV_HEADS) // n_pieces == kvh:
                    out_proj_piece(j - 1, piece)
        if t + SCORE_AHEAD < len(heads):
            maxes[t + SCORE_AHEAD] = scores(t + SCORE_AHEAD)
    for piece in range(D_MODEL // OUT_PIECE):
        out_proj_piece(SUB_BLOCKS - 1, piece)


def _attn_layer(x2, qt, k, vt, sink, rel_bias, w_out, seq_len):
    n, d = x2.shape
    t = TOKEN_TILE
    n_blocks = n // BLOCK
    kv_cols = N_KV_HEADS * HEAD_DIM
    q_cols = N_HEADS * HEAD_DIM

    def prev_block(i):
        return jnp.maximum(i * SUB_BLOCKS - 1, 0)

    def next_block(i):
        return jnp.minimum((i + 1) * SUB_BLOCKS, n_blocks - 1)

    smem = pl.BlockSpec(memory_space=pltpu.SMEM)
    return pl.pallas_call(
        functools.partial(_attn_kernel, seq_len // t),
        grid=(n // t,),
        in_specs=[
            smem,
            smem,
            pl.BlockSpec((t, d), lambda i: (i, 0)),
            pl.BlockSpec((SUB_BLOCKS, q_cols, BLOCK), lambda i: (i, 0, 0)),
            pl.BlockSpec((t, kv_cols), lambda i: (i, 0)),
            pl.BlockSpec((BLOCK, kv_cols), lambda i: (prev_block(i), 0)),
            pl.BlockSpec((BLOCK, kv_cols), lambda i: (next_block(i), 0)),
            pl.BlockSpec((SUB_BLOCKS, kv_cols, BLOCK), lambda i: (i, 0, 0)),
            pl.BlockSpec((1, kv_cols, BLOCK), lambda i: (prev_block(i), 0, 0)),
            pl.BlockSpec((1, kv_cols, BLOCK), lambda i: (next_block(i), 0, 0)),
            _const_spec((d, d)),
        ],
        out_specs=pl.BlockSpec((t, d), lambda i: (i, 0)),
        out_shape=jax.ShapeDtypeStruct((n, d), _F32),
        scratch_shapes=[
            pltpu.VMEM((N_HEADS, KEY_SPAN, BLOCK), _F32),
            pltpu.VMEM((TOKEN_TILE + 2 * BLOCK, kv_cols), _BF16),
            pltpu.VMEM((SUB_BLOCKS + 2, kv_cols, BLOCK), _BF16),
            pltpu.VMEM((SCORE_AHEAD, KEY_SPAN, Q_PER_KV * BLOCK), _F32),
            pltpu.VMEM((SCORE_AHEAD, KEY_SPAN, Q_PER_KV * BLOCK), _BF16),
            pltpu.VMEM((SUB_BLOCKS, q_cols, BLOCK), _BF16),
        ],
        compiler_params=_params(),
        name="window_attention",
    )(sink, rel_bias.reshape(-1), x2, qt, k, k, k, vt, vt, vt, w_out.astype(_BF16))


def kernel(x, norm_mix, norm_ffn, even_w_in, even_v_ln_g, even_v_ln_b, even_w_spatial, even_b_spatial,
           even_conv_w, even_w_out, attn_w_qkv, attn_sink, rel_bias, attn_w_out, ffn_w_gate, ffn_w_up,
           ffn_w_down, final_norm):
    bsz, seq_len, d = x.shape
    assert d == D_MODEL and seq_len % WIDE_TILE == 0 and (bsz * seq_len) % QKV_TILE == 0
    x2 = x.reshape(bsz * seq_len, d)
    wg, wu, wd = ffn_w_gate.astype(_BF16), ffn_w_up.astype(_BF16), ffn_w_down.astype(_BF16)
    x2 = _even_layer(x2, norm_mix[0], even_w_in[0], even_v_ln_g[0], even_v_ln_b[0], even_w_spatial[0],
                     even_b_spatial[0], even_conv_w[0], even_w_out[0], seq_len)
    x2 = _ffn_layer(x2, norm_ffn[0], wg, wu, wd, 0)
    qt, k, vt = _qkv_layer(x2, norm_mix[1], attn_w_qkv[0])
    x2 = _attn_layer(x2, qt, k, vt, attn_sink[0], rel_bias, attn_w_out[0], seq_len)
    x2 = _ffn_layer(x2, norm_ffn[1], wg, wu, wd, 1, final_g=final_norm)
    return x2.reshape(bsz, seq_len, d)
```

```python
import functools
import math

import jax
import jax.numpy as jnp
from jax import lax
from jax.experimental import pallas as pl
from jax.experimental.pallas import tpu as pltpu

D_MODEL = 1024
EPS = 1e-6
A_WIDTH = 512
B_WIDTH = 512
A_GROUPS = 4
A_GROUP_DIM = 128
CHUNK = 128
CONV_W = 3
N_HEADS = 16
N_KV_HEADS = 4
Q_PER_KV = 4
HEAD_DIM = 64
WINDOW = 128
BLOCK = 128
N_BUCKETS = 32
MAX_DISTANCE = 128
D_FF = 2816

TOKEN_TILE = 1024
WIDE_TILE = 1024
QKV_TILE = 2048
HALO = 16
EVEN_PARTS = 2
FF_CHUNK = 256
N_FF_CHUNKS = D_FF // FF_CHUNK
SUB_BLOCKS = TOKEN_TILE // BLOCK
KEY_SPAN = 3 * BLOCK
MASK_VALUE = float("-inf")
LOG2_E = math.log2(math.e)
ROW_SLAB = 64
SCORE_AHEAD = 2
V7X_VMEM_BYTES = 64 * 1024 * 1024
VMEM_LIMIT_BYTES = V7X_VMEM_BYTES * 7 // 8

_BF16 = jnp.bfloat16
_F32 = jnp.float32


def _dot(a, b):
    return jnp.dot(a, b, preferred_element_type=_F32)


def _dot_nt(a, b):
    return lax.dot_general(a, b, (((1,), (1,)), ((), ())), preferred_element_type=_F32)


def _dot_tn(a, b):
    return lax.dot_general(a, b, (((0,), (0,)), ((), ())), preferred_element_type=_F32)


def _gelu(x):
    return 0.5 * x * (1.0 + lax.erf(x * math.sqrt(0.5)))


def _rms(x, g):
    return (x * lax.rsqrt(jnp.mean(x * x, axis=-1, keepdims=True) + EPS)) * g


def _const_spec(shape):
    zeros = (0,) * len(shape)
    return pl.BlockSpec(shape, lambda i: zeros, pipeline_mode=pl.Buffered(1))


def _params():
    return pltpu.CompilerParams(
        dimension_semantics=("arbitrary",), vmem_limit_bytes=VMEM_LIMIT_BYTES)


def _even_kernel(tiles_per_seq, x_ref, xp_ref, xn_ref, g_ref, win_ref, lng_ref, lnb_ref,
                 wsp_ref, bsp_ref, cw_ref, wo_ref, o_ref, hs_ref, conv_ref, y_ref):
    t = WIDE_TILE
    i = pl.program_id(0)
    pos = i % tiles_per_seq
    g = g_ref[...]
    h = _rms(x_ref[...], g).astype(_BF16)
    hp = jnp.where(pos == 0, 0.0, _rms(xp_ref[...], g)).astype(_BF16)
    hn = jnp.where(pos == tiles_per_seq - 1, 0.0, _rms(xn_ref[...], g)).astype(_BF16)
    hs_ref[0:HALO, :] = hp
    hs_ref[HALO:HALO + t, :] = h
    hs_ref[HALO + t:, :] = hn

    in_width_a = 2 * A_WIDTH + B_WIDTH
    pc = _dot(hs_ref[...], win_ref[:, in_width_a:])

    z = pc[:, :B_WIDTH] * pc[:, B_WIDTH:]
    n_rows = t + 2 * HALO
    z_prev = pltpu.roll(z, 1, 0)[HALO:HALO + t]
    z_next = pltpu.roll(z, n_rows - 1, 0)[HALO:HALO + t]
    cw = cw_ref[...]
    conv_ref[...] = z_prev * cw[0:1] + z[HALO:HALO + t] * cw[1:2] + z_next * cw[2:3]

    part = t // EVEN_PARTS
    parts = [slice(q * part, (q + 1) * part) for q in range(EVEN_PARTS)]
    pas = [_dot(hs_ref[HALO + rows.start:HALO + rows.stop, :], win_ref[:, :in_width_a])
           for rows in parts]
    for rows, pa in zip(parts, pas):
        u = _gelu(pa[:, :A_WIDTH])
        v = _gelu(pa[:, A_WIDTH:2 * A_WIDTH])
        mu = jnp.mean(v, axis=-1, keepdims=True)
        vc = v - mu
        vn = vc * lax.rsqrt(jnp.mean(vc * vc, axis=-1, keepdims=True) + EPS)
        vb = (vn * lng_ref[...] + lnb_ref[...]).astype(_BF16)
        chunks = [slice(c * CHUNK, (c + 1) * CHUNK) for c in range(part // CHUNK)]
        for grp in range(A_GROUPS):
            cols = slice(grp * A_GROUP_DIM, (grp + 1) * A_GROUP_DIM)
            v_chunks = jnp.concatenate([vb[crows, cols] for crows in chunks], axis=1)
            mixed = _dot(wsp_ref[grp], v_chunks) + bsp_ref[grp]
            for c, crows in enumerate(chunks):
                orows = slice(rows.start + crows.start, rows.start + crows.stop)
                y_ref[orows, cols] = (u[crows, cols]
                                      * mixed[:, c * A_GROUP_DIM:(c + 1) * A_GROUP_DIM]).astype(_BF16)
        y_ref[rows, A_WIDTH:] = (pa[:, 2 * A_WIDTH:] * conv_ref[rows, :]).astype(_BF16)
        o_ref[rows, :] = x_ref[rows, :] + _dot(y_ref[rows, :], wo_ref[...])


def _even_layer(x2, g, w_in, ln_g, ln_b, w_sp, b_sp, conv_w, w_out, seq_len):
    n, d = x2.shape
    t = WIDE_TILE
    halo_blocks_per_tile = t // HALO
    n_halo_blocks = n // HALO
    return pl.pallas_call(
        functools.partial(_even_kernel, seq_len // t),
        grid=(n // t,),
        in_specs=[
            pl.BlockSpec((t, d), lambda i: (i, 0)),
            pl.BlockSpec((HALO, d), lambda i: (jnp.maximum(i * halo_blocks_per_tile - 1, 0), 0)),
            pl.BlockSpec((HALO, d),
                         lambda i: (jnp.minimum((i + 1) * halo_blocks_per_tile, n_halo_blocks - 1), 0)),
            _const_spec((1, d)),
            _const_spec(w_in.shape),
            _const_spec((1, A_WIDTH)),
            _const_spec((1, A_WIDTH)),
            _const_spec((A_GROUPS, CHUNK, CHUNK)),
            _const_spec((A_GROUPS, CHUNK, 1)),
            _const_spec((CONV_W, B_WIDTH)),
            _const_spec((d, d)),
        ],
        out_specs=pl.BlockSpec((t, d), lambda i: (i, 0)),
        out_shape=jax.ShapeDtypeStruct((n, d), _F32),
        scratch_shapes=[
            pltpu.VMEM((t + 2 * HALO, d), _BF16),
            pltpu.VMEM((t, B_WIDTH), _F32),
            pltpu.VMEM((t, d), _BF16),
        ],
        compiler_params=_params(),
        name="even_mixer",
    )(x2, x2, x2, g.reshape(1, d), w_in.astype(_BF16), ln_g.reshape(1, -1), ln_b.reshape(1, -1),
      w_sp.astype(_BF16), b_sp[..., None], conv_w, w_out.astype(_BF16))


def _ffn_kernel(has_final_norm, x_ref, g_ref, wg_ref, wu_ref, wd_ref, *rest):
    if has_final_norm:
        gf_ref, o_ref, h_ref, acc_ref = rest
    else:
        o_ref, h_ref, acc_ref = rest
    x = x_ref[...]
    h_ref[...] = _rms(x, g_ref[...]).astype(_BF16)
    acc_ref[...] = x

    h = h_ref[...]
    for c in range(N_FF_CHUNKS):
        cols = slice(c * FF_CHUNK, (c + 1) * FF_CHUNK)
        gate = _dot(h, wg_ref[:, cols])
        up = _dot(h, wu_ref[:, cols])
        act = (jax.nn.silu(gate) * up).astype(_BF16)
        acc_ref[...] += _dot(act, wd_ref[cols, :])
    out = acc_ref[...]
    if has_final_norm:
        out = _rms(out, gf_ref[...])
    o_ref[...] = out


def _layer_spec(stacked_shape, layer):
    zeros = (0,) * (len(stacked_shape) - 1)
    return pl.BlockSpec((None,) + tuple(stacked_shape[1:]), lambda i: (layer,) + zeros,
                        pipeline_mode=pl.Buffered(1))


def _ffn_layer(x2, g, wg, wu, wd, layer, final_g=None):
    n, d = x2.shape
    t = WIDE_TILE
    in_specs = [
        pl.BlockSpec((t, d), lambda i: (i, 0)),
        _const_spec((1, d)),
        _layer_spec(wg.shape, layer),
        _layer_spec(wu.shape, layer),
        _layer_spec(wd.shape, layer),
    ]
    args = [x2, g.reshape(1, d), wg, wu, wd]
    if final_g is not None:
        in_specs.append(_const_spec((1, d)))
        args.append(final_g.reshape(1, d))
    return pl.pallas_call(
        functools.partial(_ffn_kernel, final_g is not None),
        grid=(n // t,),
        in_specs=in_specs,
        out_specs=pl.BlockSpec((t, d), lambda i: (i, 0)),
        out_shape=jax.ShapeDtypeStruct((n, d), _F32),
        scratch_shapes=[pltpu.VMEM((t, d), _BF16), pltpu.VMEM((t, d), _F32)],
        compiler_params=_params(),
        name="swiglu_ffn",
    )(*args)


def _qkv_kernel(x_ref, g_ref, wqv_ref, wk_ref, qt_ref, k_ref, vt_ref):
    h = _rms(x_ref[...], g_ref[...]).astype(_BF16)
    qv = _dot_nt(wqv_ref[...], h)
    q_rows = N_HEADS * HEAD_DIM
    scale = HEAD_DIM ** -0.5 * LOG2_E
    for j in range(x_ref.shape[0] // BLOCK):
        lanes = slice(j * BLOCK, (j + 1) * BLOCK)
        qt_ref[j] = (qv[:q_rows, lanes] * scale).astype(_BF16)
        vt_ref[j] = qv[q_rows:, lanes].astype(_BF16)
    k_ref[...] = _dot(h, wk_ref[...]).astype(_BF16)


def _qkv_layer(x2, g, w_qkv):
    n, d = x2.shape
    t = QKV_TILE
    q_cols = N_HEADS * HEAD_DIM
    kv_cols = N_KV_HEADS * HEAD_DIM
    wq = w_qkv[:, :q_cols]
    wk = w_qkv[:, q_cols:q_cols + kv_cols]
    wv = w_qkv[:, q_cols + kv_cols:]
    wqv_t = jnp.concatenate([wq, wv], axis=1).T.astype(_BF16)
    n_blocks = n // BLOCK
    return pl.pallas_call(
        _qkv_kernel,
        grid=(n // t,),
        in_specs=[
            pl.BlockSpec((t, d), lambda i: (i, 0)),
            _const_spec((1, d)),
            _const_spec(wqv_t.shape),
            _const_spec((d, kv_cols)),
        ],
        out_specs=[
            pl.BlockSpec((t // BLOCK, q_cols, BLOCK), lambda i: (i, 0, 0)),
            pl.BlockSpec((t, kv_cols), lambda i: (i, 0)),
            pl.BlockSpec((t // BLOCK, kv_cols, BLOCK), lambda i: (i, 0, 0)),
        ],
        out_shape=[
            jax.ShapeDtypeStruct((n_blocks, q_cols, BLOCK), _BF16),
            jax.ShapeDtypeStruct((n, kv_cols), _BF16),
            jax.ShapeDtypeStruct((n_blocks, kv_cols, BLOCK), _BF16),
        ],
        compiler_params=_params(),
        name="qkv_proj",
    )(x2, g.reshape(1, d), wqv_t, wk.astype(_BF16))


def _t5_log_thresholds():
    nb = N_BUCKETS // 2
    max_exact = nb // 2
    steps = nb - max_exact
    ratio = MAX_DISTANCE // max_exact
    assert ratio * max_exact == MAX_DISTANCE
    thresholds = []
    n = max_exact
    for k in range(1, steps):
        while n ** steps < max_exact ** steps * ratio ** k:
            n += 1
        thresholds.append(n)
    return thresholds


def _t5_buckets(rel):
    nb = N_BUCKETS // 2
    max_exact = nb // 2
    n = jnp.abs(rel)
    large = jnp.full(rel.shape, max_exact, jnp.int32)
    for thr in _t5_log_thresholds():
        large = large + (n >= thr).astype(jnp.int32)
    return jnp.where(rel > 0, nb, 0) + jnp.where(n < max_exact, n, large)


def _attn_kernel(tiles_per_seq, sink_ref, rb_ref, x_ref, qt_ref, kc_ref, kp_ref, kn_ref,
                 vc_ref, vp_ref, vn_ref, wo_ref, o_ref,
                 bias_ref, kall_ref, vall_ref, sc_ref, p_ref, at_ref):
    i = pl.program_id(0)
    pos = i % tiles_per_seq

    @pl.when(i == 0)
    def _init():
        rel = (lax.broadcasted_iota(jnp.int32, (KEY_SPAN, BLOCK), 0) - BLOCK
               - lax.broadcasted_iota(jnp.int32, (KEY_SPAN, BLOCK), 1))
        band = jnp.abs(rel) <= WINDOW
        bkt = _t5_buckets(rel)

        def fill_head(hd, carry):
            tbl = jnp.zeros((KEY_SPAN, BLOCK), _F32)
            for b in range(N_BUCKETS):
                tbl = jnp.where(bkt == b, rb_ref[b * N_HEADS + hd], tbl)
            bias_ref[hd] = jnp.where(band, tbl * LOG2_E, MASK_VALUE)
            return carry

        lax.fori_loop(0, N_HEADS, fill_head, 0)

    kall_ref[0:BLOCK, :] = kp_ref[...]
    kall_ref[BLOCK:BLOCK + TOKEN_TILE, :] = kc_ref[...]
    kall_ref[BLOCK + TOKEN_TILE:, :] = kn_ref[...]
    vall_ref[0] = vp_ref[0]
    for j in range(SUB_BLOCKS):
        vall_ref[1 + j] = vc_ref[j]
    vall_ref[1 + SUB_BLOCKS] = vn_ref[0]

    pen_first = jnp.where(pos == 0, MASK_VALUE, 0.0)
    pen_last = jnp.where(pos == tiles_per_seq - 1, MASK_VALUE, 0.0)

    pair_lanes = 2 * HEAD_DIM
    slabs = [slice(r0, r0 + ROW_SLAB) for r0 in range(0, KEY_SPAN, ROW_SLAB)]
    heads = [(j, kvh) for j in range(SUB_BLOCKS) for kvh in range(N_KV_HEADS)]

    def scores(t):
        j, kvh = heads[t]
        pair, half = kvh // 2, kvh % 2
        q_heads = jnp.concatenate(
            [qt_ref[j, (kvh * Q_PER_KV + grp) * HEAD_DIM:(kvh * Q_PER_KV + grp + 1) * HEAD_DIM, :]
             for grp in range(Q_PER_KV)], axis=1)
        q_zero = jnp.zeros_like(q_heads)
        q2 = jnp.concatenate([q_heads, q_zero] if half == 0 else [q_zero, q_heads], axis=0)
        kwin = kall_ref[j * BLOCK:j * BLOCK + KEY_SPAN, pair * pair_lanes:(pair + 1) * pair_lanes]
        raw = _dot(kwin, q2)
        sc_buf = sc_ref.at[t % SCORE_AHEAD]
        maxes = []
        for grp in range(Q_PER_KV):
            hd = kvh * Q_PER_KV + grp
            lanes = slice(grp * BLOCK, (grp + 1) * BLOCK)
            mx = None
            for rows in slabs:
                s = raw[rows, lanes] + bias_ref[hd, rows, :]
                if j == 0 and rows.stop <= BLOCK:
                    s = s + pen_first
                if j == SUB_BLOCKS - 1 and rows.start >= 2 * BLOCK:
                    s = s + pen_last
                sc_buf[rows, lanes] = s
                mx = s if mx is None else jnp.maximum(mx, s)
            maxes.append(jnp.maximum(jnp.max(mx, axis=0, keepdims=True), sink_ref[hd] * LOG2_E))
        return maxes

    def softmax(t, maxes):
        j, kvh = heads[t]
        sc_buf = sc_ref.at[t % SCORE_AHEAD]
        p_buf = p_ref.at[t % SCORE_AHEAD]
        inv_l = []
        for grp in range(Q_PER_KV):
            hd = kvh * Q_PER_KV + grp
            lanes = slice(grp * BLOCK, (grp + 1) * BLOCK)
            m = maxes[grp]
            acc = None
            for rows in slabs:
                p = jnp.exp2(sc_buf[rows, lanes] - m)
                acc = p if acc is None else acc + p
                p_buf[rows, lanes] = p.astype(_BF16)
            l = jnp.sum(acc, axis=0, keepdims=True) + jnp.exp2(sink_ref[hd] * LOG2_E - m)
            inv_l.append(1.0 / l)
        return inv_l

    def weighted_values(t, inv_l):
        j, kvh = heads[t]
        vwin = jnp.concatenate(
            [vall_ref[j + b, kvh * HEAD_DIM:(kvh + 1) * HEAD_DIM, :] for b in range(3)], axis=1)
        ot = _dot(vwin, p_ref[t % SCORE_AHEAD])
        for grp in range(Q_PER_KV):
            hd = kvh * Q_PER_KV + grp
            at_ref[j, hd * HEAD_DIM:(hd + 1) * HEAD_DIM, :] = (
                ot[:, grp * BLOCK:(grp + 1) * BLOCK] * inv_l[grp]).astype(_BF16)

    def out_proj_part(t):
        j, kvh = heads[t]
        rows = slice(j * BLOCK, (j + 1) * BLOCK)
        feat = slice(kvh * Q_PER_KV * HEAD_DIM, (kvh + 1) * Q_PER_KV * HEAD_DIM)
        part = _dot_tn(at_ref[j, feat, :], wo_ref[feat, :])
        if kvh == 0:
            o_ref[rows, :] = x_ref[rows, :] + part
        else:
            o_ref[rows, :] += part

    maxes = {t: scores(t) for t in range(SCORE_AHEAD)}
    for t in range(len(heads)):
        inv_l = softmax(t, maxes.pop(t))
        weighted_values(t, inv_l)
        if t > 0:
            out_proj_part(t - 1)
        if t + SCORE_AHEAD < len(heads):
            maxes[t + SCORE_AHEAD] = scores(t + SCORE_AHEAD)
    out_proj_part(len(heads) - 1)


def _attn_layer(x2, qt, k, vt, sink, rel_bias, w_out, seq_len):
    n, d = x2.shape
    t = TOKEN_TILE
    n_blocks = n // BLOCK
    kv_cols = N_KV_HEADS * HEAD_DIM
    q_cols = N_HEADS * HEAD_DIM

    def prev_block(i):
        return jnp.maximum(i * SUB_BLOCKS - 1, 0)

    def next_block(i):
        return jnp.minimum((i + 1) * SUB_BLOCKS, n_blocks - 1)

    smem = pl.BlockSpec(memory_space=pltpu.SMEM)
    return pl.pallas_call(
        functools.partial(_attn_kernel, seq_len // t),
        grid=(n // t,),
        in_specs=[
            smem,
            smem,
            pl.BlockSpec((t, d), lambda i: (i, 0)),
            pl.BlockSpec((SUB_BLOCKS, q_cols, BLOCK), lambda i: (i, 0, 0)),
            pl.BlockSpec((t, kv_cols), lambda i: (i, 0)),
            pl.BlockSpec((BLOCK, kv_cols), lambda i: (prev_block(i), 0)),
            pl.BlockSpec((BLOCK, kv_cols), lambda i: (next_block(i), 0)),
            pl.BlockSpec((SUB_BLOCKS, kv_cols, BLOCK), lambda i: (i, 0, 0)),
            pl.BlockSpec((1, kv_cols, BLOCK), lambda i: (prev_block(i), 0, 0)),
            pl.BlockSpec((1, kv_cols, BLOCK), lambda i: (next_block(i), 0, 0)),
            _const_spec((d, d)),
        ],
        out_specs=pl.BlockSpec((t, d), lambda i: (i, 0)),
        out_shape=jax.ShapeDtypeStruct((n, d), _F32),
        scratch_shapes=[
            pltpu.VMEM((N_HEADS, KEY_SPAN, BLOCK), _F32),
            pltpu.VMEM((TOKEN_TILE + 2 * BLOCK, kv_cols), _BF16),
            pltpu.VMEM((SUB_BLOCKS + 2, kv_cols, BLOCK), _BF16),
            pltpu.VMEM((SCORE_AHEAD, KEY_SPAN, Q_PER_KV * BLOCK), _F32),
            pltpu.VMEM((SCORE_AHEAD, KEY_SPAN, Q_PER_KV * BLOCK), _BF16),
            pltpu.VMEM((SUB_BLOCKS, q_cols, BLOCK), _BF16),
        ],
        compiler_params=_params(),
        name="window_attention",
    )(sink, rel_bias.reshape(-1), x2, qt, k, k, k, vt, vt, vt, w_out.astype(_BF16))


def kernel(x, norm_mix, norm_ffn, even_w_in, even_v_ln_g, even_v_ln_b, even_w_spatial, even_b_spatial,
           even_conv_w, even_w_out, attn_w_qkv, attn_sink, rel_bias, attn_w_out, ffn_w_gate, ffn_w_up,
           ffn_w_down, final_norm):
    bsz, seq_len, d = x.shape
    assert d == D_MODEL and seq_len % WIDE_TILE == 0 and (bsz * seq_len) % QKV_TILE == 0
    x2 = x.reshape(bsz * seq_len, d)
    wg, wu, wd = ffn_w_gate.astype(_BF16), ffn_w_up.astype(_BF16), ffn_w_down.astype(_BF16)
    x2 = _even_layer(x2, norm_mix[0], even_w_in[0], even_v_ln_g[0], even_v_ln_b[0], even_w_spatial[0],
                     even_b_spatial[0], even_conv_w[0], even_w_out[0], seq_len)
    x2 = _ffn_layer(x2, norm_ffn[0], wg, wu, wd, 0)
    qt, k, vt = _qkv_layer(x2, norm_mix[1], attn_w_qkv[0])
    x2 = _attn_layer(x2, qt, k, vt, attn_sink[0], rel_bias, attn_w_out[0], seq_len)
    x2 = _ffn_layer(x2, norm_ffn[1], wg, wu, wd, 1, final_g=final_norm)
    return x2.reshape(bsz, seq_len, d)
```

```python
import functools
import math

import jax
import jax.numpy as jnp
from jax import lax
from jax.experimental import pallas as pl
from jax.experimental.pallas import tpu as pltpu

D_MODEL = 1024
EPS = 1e-6
A_WIDTH = 512
B_WIDTH = 512
A_GROUPS = 4
A_GROUP_DIM = 128
CHUNK = 128
CONV_W = 3
N_HEADS = 16
N_KV_HEADS = 4
Q_PER_KV = 4
HEAD_DIM = 64
WINDOW = 128
BLOCK = 128
N_BUCKETS = 32
MAX_DISTANCE = 128
D_FF = 2816

TOKEN_TILE = 1024
WIDE_TILE = 1024
QKV_TILE = 2048
HALO = 16
EVEN_PARTS = 2
FF_CHUNK = 256
N_FF_CHUNKS = D_FF // FF_CHUNK
SUB_BLOCKS = TOKEN_TILE // BLOCK
KEY_SPAN = 3 * BLOCK
MASK_VALUE = float("-inf")
LOG2_E = math.log2(math.e)
ROW_SLAB = 64
OUT_PIECE = 256
SCORE_AHEAD = 3
V7X_VMEM_BYTES = 64 * 1024 * 1024
VMEM_LIMIT_BYTES = V7X_VMEM_BYTES * 7 // 8

_BF16 = jnp.bfloat16
_F32 = jnp.float32


def _dot(a, b):
    return jnp.dot(a, b, preferred_element_type=_F32)


def _dot_nt(a, b):
    return lax.dot_general(a, b, (((1,), (1,)), ((), ())), preferred_element_type=_F32)


def _dot_tn(a, b):
    return lax.dot_general(a, b, (((0,), (0,)), ((), ())), preferred_element_type=_F32)


def _gelu(x):
    return 0.5 * x * (1.0 + lax.erf(x * math.sqrt(0.5)))


def _rms(x, g):
    return (x * lax.rsqrt(jnp.mean(x * x, axis=-1, keepdims=True) + EPS)) * g


def _const_spec(shape):
    zeros = (0,) * len(shape)
    return pl.BlockSpec(shape, lambda i: zeros, pipeline_mode=pl.Buffered(1))


def _params():
    return pltpu.CompilerParams(
        dimension_semantics=("arbitrary",), vmem_limit_bytes=VMEM_LIMIT_BYTES)


def _even_kernel(tiles_per_seq, x_ref, xp_ref, xn_ref, g_ref, win_ref, lng_ref, lnb_ref,
                 wsp_ref, bsp_ref, cw_ref, wo_ref, o_ref, hs_ref, conv_ref, y_ref):
    t = WIDE_TILE
    i = pl.program_id(0)
    pos = i % tiles_per_seq
    g = g_ref[...]
    h = _rms(x_ref[...], g).astype(_BF16)
    hp = jnp.where(pos == 0, 0.0, _rms(xp_ref[...], g)).astype(_BF16)
    hn = jnp.where(pos == tiles_per_seq - 1, 0.0, _rms(xn_ref[...], g)).astype(_BF16)
    hs_ref[0:HALO, :] = hp
    hs_ref[HALO:HALO + t, :] = h
    hs_ref[HALO + t:, :] = hn

    in_width_a = 2 * A_WIDTH + B_WIDTH
    pc = _dot(hs_ref[...], win_ref[:, in_width_a:])

    z = pc[:, :B_WIDTH] * pc[:, B_WIDTH:]
    n_rows = t + 2 * HALO
    z_prev = pltpu.roll(z, 1, 0)[HALO:HALO + t]
    z_next = pltpu.roll(z, n_rows - 1, 0)[HALO:HALO + t]
    cw = cw_ref[...]
    conv_ref[...] = z_prev * cw[0:1] + z[HALO:HALO + t] * cw[1:2] + z_next * cw[2:3]

    part = t // EVEN_PARTS
    parts = [slice(q * part, (q + 1) * part) for q in range(EVEN_PARTS)]
    pas = [_dot(hs_ref[HALO + rows.start:HALO + rows.stop, :], win_ref[:, :in_width_a])
           for rows in parts]
    for rows, pa in zip(parts, pas):
        u = _gelu(pa[:, :A_WIDTH])
        v = _gelu(pa[:, A_WIDTH:2 * A_WIDTH])
        mu = jnp.mean(v, axis=-1, keepdims=True)
        vc = v - mu
        vn = vc * lax.rsqrt(jnp.mean(vc * vc, axis=-1, keepdims=True) + EPS)
        vb = (vn * lng_ref[...] + lnb_ref[...]).astype(_BF16)
        chunks = [slice(c * CHUNK, (c + 1) * CHUNK) for c in range(part // CHUNK)]
        for grp in range(A_GROUPS):
            cols = slice(grp * A_GROUP_DIM, (grp + 1) * A_GROUP_DIM)
            v_chunks = jnp.concatenate([vb[crows, cols] for crows in chunks], axis=1)
            mixed = _dot(wsp_ref[grp], v_chunks) + bsp_ref[grp]
            for c, crows in enumerate(chunks):
                orows = slice(rows.start + crows.start, rows.start + crows.stop)
                y_ref[orows, cols] = (u[crows, cols]
                                      * mixed[:, c * A_GROUP_DIM:(c + 1) * A_GROUP_DIM]).astype(_BF16)
        y_ref[rows, A_WIDTH:] = (pa[:, 2 * A_WIDTH:] * conv_ref[rows, :]).astype(_BF16)
        o_ref[rows, :] = x_ref[rows, :] + _dot(y_ref[rows, :], wo_ref[...])


def _even_layer(x2, g, w_in, ln_g, ln_b, w_sp, b_sp, conv_w, w_out, seq_len):
    n, d = x2.shape
    t = WIDE_TILE
    halo_blocks_per_tile = t // HALO
    n_halo_blocks = n // HALO
    return pl.pallas_call(
        functools.partial(_even_kernel, seq_len // t),
        grid=(n // t,),
        in_specs=[
            pl.BlockSpec((t, d), lambda i: (i, 0)),
            pl.BlockSpec((HALO, d), lambda i: (jnp.maximum(i * halo_blocks_per_tile - 1, 0), 0)),
            pl.BlockSpec((HALO, d),
                         lambda i: (jnp.minimum((i + 1) * halo_blocks_per_tile, n_halo_blocks - 1), 0)),
            _const_spec((1, d)),
            _const_spec(w_in.shape),
            _const_spec((1, A_WIDTH)),
            _const_spec((1, A_WIDTH)),
            _const_spec((A_GROUPS, CHUNK, CHUNK)),
            _const_spec((A_GROUPS, CHUNK, 1)),
            _const_spec((CONV_W, B_WIDTH)),
            _const_spec((d, d)),
        ],
        out_specs=pl.BlockSpec((t, d), lambda i: (i, 0)),
        out_shape=jax.ShapeDtypeStruct((n, d), _F32),
        scratch_shapes=[
            pltpu.VMEM((t + 2 * HALO, d), _BF16),
            pltpu.VMEM((t, B_WIDTH), _F32),
            pltpu.VMEM((t, d), _BF16),
        ],
        compiler_params=_params(),
        name="even_mixer",
    )(x2, x2, x2, g.reshape(1, d), w_in.astype(_BF16), ln_g.reshape(1, -1), ln_b.reshape(1, -1),
      w_sp.astype(_BF16), b_sp[..., None], conv_w, w_out.astype(_BF16))


def _ffn_kernel(has_final_norm, x_ref, g_ref, wg_ref, wu_ref, wd_ref, *rest):
    if has_final_norm:
        gf_ref, o_ref, h_ref, acc_ref = rest
    else:
        o_ref, h_ref, acc_ref = rest
    x = x_ref[...]
    h_ref[...] = _rms(x, g_ref[...]).astype(_BF16)
    acc_ref[...] = x

    h = h_ref[...]
    for c in range(N_FF_CHUNKS):
        cols = slice(c * FF_CHUNK, (c + 1) * FF_CHUNK)
        gate = _dot(h, wg_ref[:, cols])
        up = _dot(h, wu_ref[:, cols])
        act = (jax.nn.silu(gate) * up).astype(_BF16)
        acc_ref[...] += _dot(act, wd_ref[cols, :])
    out = acc_ref[...]
    if has_final_norm:
        out = _rms(out, gf_ref[...])
    o_ref[...] = out


def _layer_spec(stacked_shape, layer):
    zeros = (0,) * (len(stacked_shape) - 1)
    return pl.BlockSpec((None,) + tuple(stacked_shape[1:]), lambda i: (layer,) + zeros,
                        pipeline_mode=pl.Buffered(1))


def _ffn_layer(x2, g, wg, wu, wd, layer, final_g=None):
    n, d = x2.shape
    t = WIDE_TILE
    in_specs = [
        pl.BlockSpec((t, d), lambda i: (i, 0)),
        _const_spec((1, d)),
        _layer_spec(wg.shape, layer),
        _layer_spec(wu.shape, layer),
        _layer_spec(wd.shape, layer),
    ]
    args = [x2, g.reshape(1, d), wg, wu, wd]
    if final_g is not None:
        in_specs.append(_const_spec((1, d)))
        args.append(final_g.reshape(1, d))
    return pl.pallas_call(
        functools.partial(_ffn_kernel, final_g is not None),
        grid=(n // t,),
        in_specs=in_specs,
        out_specs=pl.BlockSpec((t, d), lambda i: (i, 0)),
        out_shape=jax.ShapeDtypeStruct((n, d), _F32),
        scratch_shapes=[pltpu.VMEM((t, d), _BF16), pltpu.VMEM((t, d), _F32)],
        compiler_params=_params(),
        name="swiglu_ffn",
    )(*args)


def _qkv_kernel(x_ref, g_ref, wqv_ref, wk_ref, qt_ref, k_ref, vt_ref):
    h = _rms(x_ref[...], g_ref[...]).astype(_BF16)
    qv = _dot_nt(wqv_ref[...], h)
    q_rows = N_HEADS * HEAD_DIM
    scale = HEAD_DIM ** -0.5 * LOG2_E
    for j in range(x_ref.shape[0] // BLOCK):
        lanes = slice(j * BLOCK, (j + 1) * BLOCK)
        qt_ref[j] = (qv[:q_rows, lanes] * scale).astype(_BF16)
        vt_ref[j] = qv[q_rows:, lanes].astype(_BF16)
    k_ref[...] = _dot(h, wk_ref[...]).astype(_BF16)


def _qkv_layer(x2, g, w_qkv):
    n, d = x2.shape
    t = QKV_TILE
    q_cols = N_HEADS * HEAD_DIM
    kv_cols = N_KV_HEADS * HEAD_DIM
    wq = w_qkv[:, :q_cols]
    wk = w_qkv[:, q_cols:q_cols + kv_cols]
    wv = w_qkv[:, q_cols + kv_cols:]
    wqv_t = jnp.concatenate([wq, wv], axis=1).T.astype(_BF16)
    n_blocks = n // BLOCK
    return pl.pallas_call(
        _qkv_kernel,
        grid=(n // t,),
        in_specs=[
            pl.BlockSpec((t, d), lambda i: (i, 0)),
            _const_spec((1, d)),
            _const_spec(wqv_t.shape),
            _const_spec((d, kv_cols)),
        ],
        out_specs=[
            pl.BlockSpec((t // BLOCK, q_cols, BLOCK), lambda i: (i, 0, 0)),
            pl.BlockSpec((t, kv_cols), lambda i: (i, 0)),
            pl.BlockSpec((t // BLOCK, kv_cols, BLOCK), lambda i: (i, 0, 0)),
        ],
        out_shape=[
            jax.ShapeDtypeStruct((n_blocks, q_cols, BLOCK), _BF16),
            jax.ShapeDtypeStruct((n, kv_cols), _BF16),
            jax.ShapeDtypeStruct((n_blocks, kv_cols, BLOCK), _BF16),
        ],
        compiler_params=_params(),
        name="qkv_proj",
    )(x2, g.reshape(1, d), wqv_t, wk.astype(_BF16))


def _t5_log_thresholds():
    nb = N_BUCKETS // 2
    max_exact = nb // 2
    steps = nb - max_exact
    ratio = MAX_DISTANCE // max_exact
    assert ratio * max_exact == MAX_DISTANCE
    thresholds = []
    n = max_exact
    for k in range(1, steps):
        while n ** steps < max_exact ** steps * ratio ** k:
            n += 1
        thresholds.append(n)
    return thresholds


def _t5_buckets(rel):
    nb = N_BUCKETS // 2
    max_exact = nb // 2
    n = jnp.abs(rel)
    large = jnp.full(rel.shape, max_exact, jnp.int32)
    for thr in _t5_log_thresholds():
        large = large + (n >= thr).astype(jnp.int32)
    return jnp.where(rel > 0, nb, 0) + jnp.where(n < max_exact, n, large)


def _attn_kernel(tiles_per_seq, sink_ref, rb_ref, x_ref, qt_ref, kc_ref, kp_ref, kn_ref,
                 vc_ref, vp_ref, vn_ref, wo_ref, o_ref,
                 bias_ref, kall_ref, vall_ref, sc_ref, p_ref, at_ref):
    i = pl.program_id(0)
    pos = i % tiles_per_seq

    @pl.when(i == 0)
    def _init():
        rel = (lax.broadcasted_iota(jnp.int32, (KEY_SPAN, BLOCK), 0) - BLOCK
               - lax.broadcasted_iota(jnp.int32, (KEY_SPAN, BLOCK), 1))
        band = jnp.abs(rel) <= WINDOW
        bkt = _t5_buckets(rel)

        def fill_head(hd, carry):
            tbl = jnp.zeros((KEY_SPAN, BLOCK), _F32)
            for b in range(N_BUCKETS):
                tbl = jnp.where(bkt == b, rb_ref[b * N_HEADS + hd], tbl)
            bias_ref[hd] = jnp.where(band, tbl * LOG2_E, MASK_VALUE)
            return carry

        lax.fori_loop(0, N_HEADS, fill_head, 0)

    kall_ref[0:BLOCK, :] = kp_ref[...]
    kall_ref[BLOCK:BLOCK + TOKEN_TILE, :] = kc_ref[...]
    kall_ref[BLOCK + TOKEN_TILE:, :] = kn_ref[...]
    vall_ref[0] = vp_ref[0]
    for j in range(SUB_BLOCKS):
        vall_ref[1 + j] = vc_ref[j]
    vall_ref[1 + SUB_BLOCKS] = vn_ref[0]

    pen_first = jnp.where(pos == 0, MASK_VALUE, 0.0)
    pen_last = jnp.where(pos == tiles_per_seq - 1, MASK_VALUE, 0.0)

    pair_lanes = 2 * HEAD_DIM
    slabs = [slice(r0, r0 + ROW_SLAB) for r0 in range(0, KEY_SPAN, ROW_SLAB)]
    heads = [(j, kvh) for j in range(SUB_BLOCKS) for kvh in range(N_KV_HEADS)]

    def scores(t):
        j, kvh = heads[t]
        pair, half = kvh // 2, kvh % 2
        q_heads = jnp.concatenate(
            [qt_ref[j, (kvh * Q_PER_KV + grp) * HEAD_DIM:(kvh * Q_PER_KV + grp + 1) * HEAD_DIM, :]
             for grp in range(Q_PER_KV)], axis=1)
        q_zero = jnp.zeros_like(q_heads)
        q2 = jnp.concatenate([q_heads, q_zero] if half == 0 else [q_zero, q_heads], axis=0)
        kwin = kall_ref[j * BLOCK:j * BLOCK + KEY_SPAN, pair * pair_lanes:(pair + 1) * pair_lanes]
        raw = _dot(kwin, q2)
        sc_buf = sc_ref.at[t % SCORE_AHEAD]
        maxes = []
        for grp in range(Q_PER_KV):
            hd = kvh * Q_PER_KV + grp
            lanes = slice(grp * BLOCK, (grp + 1) * BLOCK)
            mx = None
            for rows in slabs:
                s = raw[rows, lanes] + bias_ref[hd, rows, :]
                if j == 0 and rows.stop <= BLOCK:
                    s = s + pen_first
                if j == SUB_BLOCKS - 1 and rows.start >= 2 * BLOCK:
                    s = s + pen_last
                sc_buf[rows, lanes] = s
                mx = s if mx is None else jnp.maximum(mx, s)
            maxes.append(jnp.maximum(jnp.max(mx, axis=0, keepdims=True), sink_ref[hd] * LOG2_E))
        return maxes

    def softmax(t, maxes):
        j, kvh = heads[t]
        sc_buf = sc_ref.at[t % SCORE_AHEAD]
        p_buf = p_ref.at[t % SCORE_AHEAD]
        inv_l = []
        for grp in range(Q_PER_KV):
            hd = kvh * Q_PER_KV + grp
            lanes = slice(grp * BLOCK, (grp + 1) * BLOCK)
            m = maxes[grp]
            acc = None
            for rows in slabs:
                p = jnp.exp2(sc_buf[rows, lanes] - m)
                acc = p if acc is None else acc + p
                p_buf[rows, lanes] = p.astype(_BF16)
            l = jnp.sum(acc, axis=0, keepdims=True) + jnp.exp2(sink_ref[hd] * LOG2_E - m)
            inv_l.append(1.0 / l)
        return inv_l

    def weighted_values(t, inv_l):
        j, kvh = heads[t]
        vwin = jnp.concatenate(
            [vall_ref[j + b, kvh * HEAD_DIM:(kvh + 1) * HEAD_DIM, :] for b in range(3)], axis=1)
        ot = _dot(vwin, p_ref[t % SCORE_AHEAD])
        for grp in range(Q_PER_KV):
            hd = kvh * Q_PER_KV + grp
            at_ref[j, hd * HEAD_DIM:(hd + 1) * HEAD_DIM, :] = (
                ot[:, grp * BLOCK:(grp + 1) * BLOCK] * inv_l[grp]).astype(_BF16)

    def out_proj_piece(j, piece):
        rows = slice(j * BLOCK, (j + 1) * BLOCK)
        cols = slice(piece * OUT_PIECE, (piece + 1) * OUT_PIECE)
        o_ref[rows, cols] = x_ref[rows, cols] + _dot_tn(at_ref[j], wo_ref[:, cols])

    n_pieces = D_MODEL // OUT_PIECE
    maxes = {t: scores(t) for t in range(SCORE_AHEAD)}
    for t, (j, kvh) in enumerate(heads):
        inv_l = softmax(t, maxes.pop(t))
        weighted_values(t, inv_l)
        if j > 0:
            for piece in range(n_pieces):
                if (piece * N_KV_HEADS) // n_pieces == kvh:
                    out_proj_piece(j - 1, piece)
        if t + SCORE_AHEAD < len(heads):
            maxes[t + SCORE_AHEAD] = scores(t + SCORE_AHEAD)
    for piece in range(D_MODEL // OUT_PIECE):
        out_proj_piece(SUB_BLOCKS - 1, piece)


def _attn_layer(x2, qt, k, vt, sink, rel_bias, w_out, seq_len):
    n, d = x2.shape
    t = TOKEN_TILE
    n_blocks = n // BLOCK
    kv_cols = N_KV_HEADS * HEAD_DIM
    q_cols = N_HEADS * HEAD_DIM

    def prev_block(i):
        return jnp.maximum(i * SUB_BLOCKS - 1, 0)

    def next_block(i):
        return jnp.minimum((i + 1) * SUB_BLOCKS, n_blocks - 1)

    smem = pl.BlockSpec(memory_space=pltpu.SMEM)
    return pl.pallas_call(
        functools.partial(_attn_kernel, seq_len // t),
        grid=(n // t,),
        in_specs=[
            smem,
            smem,
            pl.BlockSpec((t, d), lambda i: (i, 0)),
            pl.BlockSpec((SUB_BLOCKS, q_cols, BLOCK), lambda i: (i, 0, 0)),
            pl.BlockSpec((t, kv_cols), lambda i: (i, 0)),
            pl.BlockSpec((BLOCK, kv_cols), lambda i: (prev_block(i), 0)),
            pl.BlockSpec((BLOCK, kv_cols), lambda i: (next_block(i), 0)),
            pl.BlockSpec((SUB_BLOCKS, kv_cols, BLOCK), lambda i: (i, 0, 0)),
            pl.BlockSpec((1, kv_cols, BLOCK), lambda i: (prev_block(i), 0, 0)),
            pl.BlockSpec((1, kv_cols, BLOCK), lambda i: (next_block(i), 0, 0)),
            _const_spec((d, d)),
        ],
        out_specs=pl.BlockSpec((t, d), lambda i: (i, 0)),
        out_shape=jax.ShapeDtypeStruct((n, d), _F32),
        scratch_shapes=[
            pltpu.VMEM((N_HEADS, KEY_SPAN, BLOCK), _F32),
            pltpu.VMEM((TOKEN_TILE + 2 * BLOCK, kv_cols), _BF16),
            pltpu.VMEM((SUB_BLOCKS + 2, kv_cols, BLOCK), _BF16),
            pltpu.VMEM((SCORE_AHEAD, KEY_SPAN, Q_PER_KV * BLOCK), _F32),
            pltpu.VMEM((SCORE_AHEAD, KEY_SPAN, Q_PER_KV * BLOCK), _BF16),
            pltpu.VMEM((SUB_BLOCKS, q_cols, BLOCK), _BF16),
        ],
        compiler_params=_params(),
        name="window_attention",
    )(sink, rel_bias.reshape(-1), x2, qt, k, k, k, vt, vt, vt, w_out.astype(_BF16))


def kernel(x, norm_mix, norm_ffn, even_w_in, even_v_ln_g, even_v_ln_b, even_w_spatial, even_b_spatial,
           even_conv_w, even_w_out, attn_w_qkv, attn_sink, rel_bias, attn_w_out, ffn_w_gate, ffn_w_up,
           ffn_w_down, final_norm):
    bsz, seq_len, d = x.shape
    assert d == D_MODEL and seq_len % WIDE_TILE == 0 and (bsz * seq_len) % QKV_TILE == 0
    x2 = x.reshape(bsz * seq_len, d)
    wg, wu, wd = ffn_w_gate.astype(_BF16), ffn_w_up.astype(_BF16), ffn_w_down.astype(_BF16)
    x2 = _even_layer(x2, norm_mix[0], even_w_in[0], even_v_ln_g[0], even_v_ln_b[0], even_w_spatial[0],
                     even_b_spatial[0], even_conv_w[0], even_w_out[0], seq_len)
    x2 = _ffn_layer(x2, norm_ffn[0], wg, wu, wd, 0)
    qt, k, vt = _qkv_layer(x2, norm_mix[1], attn_w_qkv[0])
    x2 = _attn_layer(x2, qt, k, vt, attn_sink[0], rel_bias, attn_w_out[0], seq_len)
    x2 = _ffn_layer(x2, norm_ffn[1], wg, wu, wd, 1, final_g=final_norm)
    return x2.reshape(bsz, seq_len, d)
```

```python
import functools
import math

import jax
import jax.numpy as jnp
from jax import lax
from jax.experimental import pallas as pl
from jax.experimental.pallas import tpu as pltpu

D_MODEL = 1024
EPS = 1e-6
A_WIDTH = 512
B_WIDTH = 512
A_GROUPS = 4
A_GROUP_DIM = 128
CHUNK = 128
CONV_W = 3
N_HEADS = 16
N_KV_HEADS = 4
Q_PER_KV = 4
HEAD_DIM = 64
WINDOW = 128
BLOCK = 128
N_BUCKETS = 32
MAX_DISTANCE = 128
D_FF = 2816

TOKEN_TILE = 1024
WIDE_TILE = 1024
QKV_TILE = 2048
HALO = 16
EVEN_PARTS = 2
FF_CHUNK = 256
N_FF_CHUNKS = D_FF // FF_CHUNK
SUB_BLOCKS = TOKEN_TILE // BLOCK
KEY_SPAN = 3 * BLOCK
MASK_VALUE = float("-inf")
LOG2_E = math.log2(math.e)
ROW_SLAB = 64
OUT_PIECE = 256
HEAD_GROUP = 2
SCORE_AHEAD = 2 * HEAD_GROUP
V7X_VMEM_BYTES = 64 * 1024 * 1024
VMEM_LIMIT_BYTES = V7X_VMEM_BYTES * 7 // 8

_BF16 = jnp.bfloat16
_F32 = jnp.float32


def _dot(a, b):
    return jnp.dot(a, b, preferred_element_type=_F32)


def _dot_nt(a, b):
    return lax.dot_general(a, b, (((1,), (1,)), ((), ())), preferred_element_type=_F32)


def _dot_tn(a, b):
    return lax.dot_general(a, b, (((0,), (0,)), ((), ())), preferred_element_type=_F32)


def _gelu(x):
    return 0.5 * x * (1.0 + lax.erf(x * math.sqrt(0.5)))


def _rms(x, g):
    return (x * lax.rsqrt(jnp.mean(x * x, axis=-1, keepdims=True) + EPS)) * g


def _const_spec(shape):
    zeros = (0,) * len(shape)
    return pl.BlockSpec(shape, lambda i: zeros, pipeline_mode=pl.Buffered(1))


def _params():
    return pltpu.CompilerParams(
        dimension_semantics=("arbitrary",), vmem_limit_bytes=VMEM_LIMIT_BYTES)


def _even_kernel(tiles_per_seq, x_ref, xp_ref, xn_ref, g_ref, win_ref, lng_ref, lnb_ref,
                 wsp_ref, bsp_ref, cw_ref, wo_ref, o_ref, hs_ref, conv_ref, y_ref):
    t = WIDE_TILE
    i = pl.program_id(0)
    pos = i % tiles_per_seq
    g = g_ref[...]
    h = _rms(x_ref[...], g).astype(_BF16)
    hp = jnp.where(pos == 0, 0.0, _rms(xp_ref[...], g)).astype(_BF16)
    hn = jnp.where(pos == tiles_per_seq - 1, 0.0, _rms(xn_ref[...], g)).astype(_BF16)
    hs_ref[0:HALO, :] = hp
    hs_ref[HALO:HALO + t, :] = h
    hs_ref[HALO + t:, :] = hn

    in_width_a = 2 * A_WIDTH + B_WIDTH
    pc = _dot(hs_ref[...], win_ref[:, in_width_a:])

    z = pc[:, :B_WIDTH] * pc[:, B_WIDTH:]
    n_rows = t + 2 * HALO
    z_prev = pltpu.roll(z, 1, 0)[HALO:HALO + t]
    z_next = pltpu.roll(z, n_rows - 1, 0)[HALO:HALO + t]
    cw = cw_ref[...]
    conv_ref[...] = z_prev * cw[0:1] + z[HALO:HALO + t] * cw[1:2] + z_next * cw[2:3]

    part = t // EVEN_PARTS
    parts = [slice(q * part, (q + 1) * part) for q in range(EVEN_PARTS)]
    pas = [_dot(hs_ref[HALO + rows.start:HALO + rows.stop, :], win_ref[:, :in_width_a])
           for rows in parts]
    for rows, pa in zip(parts, pas):
        u = _gelu(pa[:, :A_WIDTH])
        v = _gelu(pa[:, A_WIDTH:2 * A_WIDTH])
        mu = jnp.mean(v, axis=-1, keepdims=True)
        vc = v - mu
        vn = vc * lax.rsqrt(jnp.mean(vc * vc, axis=-1, keepdims=True) + EPS)
        vb = (vn * lng_ref[...] + lnb_ref[...]).astype(_BF16)
        chunks = [slice(c * CHUNK, (c + 1) * CHUNK) for c in range(part // CHUNK)]
        for grp in range(A_GROUPS):
            cols = slice(grp * A_GROUP_DIM, (grp + 1) * A_GROUP_DIM)
            v_chunks = jnp.concatenate([vb[crows, cols] for crows in chunks], axis=1)
            mixed = _dot(wsp_ref[grp], v_chunks) + bsp_ref[grp]
            for c, crows in enumerate(chunks):
                orows = slice(rows.start + crows.start, rows.start + crows.stop)
                y_ref[orows, cols] = (u[crows, cols]
                                      * mixed[:, c * A_GROUP_DIM:(c + 1) * A_GROUP_DIM]).astype(_BF16)
        y_ref[rows, A_WIDTH:] = (pa[:, 2 * A_WIDTH:] * conv_ref[rows, :]).astype(_BF16)
        o_ref[rows, :] = x_ref[rows, :] + _dot(y_ref[rows, :], wo_ref[...])


def _even_layer(x2, g, w_in, ln_g, ln_b, w_sp, b_sp, conv_w, w_out, seq_len):
    n, d = x2.shape
    t = WIDE_TILE
    halo_blocks_per_tile = t // HALO
    n_halo_blocks = n // HALO
    return pl.pallas_call(
        functools.partial(_even_kernel, seq_len // t),
        grid=(n // t,),
        in_specs=[
            pl.BlockSpec((t, d), lambda i: (i, 0)),
            pl.BlockSpec((HALO, d), lambda i: (jnp.maximum(i * halo_blocks_per_tile - 1, 0), 0)),
            pl.BlockSpec((HALO, d),
                         lambda i: (jnp.minimum((i + 1) * halo_blocks_per_tile, n_halo_blocks - 1), 0)),
            _const_spec((1, d)),
            _const_spec(w_in.shape),
            _const_spec((1, A_WIDTH)),
            _const_spec((1, A_WIDTH)),
            _const_spec((A_GROUPS, CHUNK, CHUNK)),
            _const_spec((A_GROUPS, CHUNK, 1)),
            _const_spec((CONV_W, B_WIDTH)),
            _const_spec((d, d)),
        ],
        out_specs=pl.BlockSpec((t, d), lambda i: (i, 0)),
        out_shape=jax.ShapeDtypeStruct((n, d), _F32),
        scratch_shapes=[
            pltpu.VMEM((t + 2 * HALO, d), _BF16),
            pltpu.VMEM((t, B_WIDTH), _F32),
            pltpu.VMEM((t, d), _BF16),
        ],
        compiler_params=_params(),
        name="even_mixer",
    )(x2, x2, x2, g.reshape(1, d), w_in.astype(_BF16), ln_g.reshape(1, -1), ln_b.reshape(1, -1),
      w_sp.astype(_BF16), b_sp[..., None], conv_w, w_out.astype(_BF16))


def _ffn_kernel(has_final_norm, x_ref, g_ref, wg_ref, wu_ref, wd_ref, *rest):
    if has_final_norm:
        gf_ref, o_ref, h_ref, acc_ref = rest
    else:
        o_ref, h_ref, acc_ref = rest
    x = x_ref[...]
    h_ref[...] = _rms(x, g_ref[...]).astype(_BF16)
    acc_ref[...] = x

    h = h_ref[...]
    for c in range(N_FF_CHUNKS):
        cols = slice(c * FF_CHUNK, (c + 1) * FF_CHUNK)
        gate = _dot(h, wg_ref[:, cols])
        up = _dot(h, wu_ref[:, cols])
        act = (jax.nn.silu(gate) * up).astype(_BF16)
        acc_ref[...] += _dot(act, wd_ref[cols, :])
    out = acc_ref[...]
    if has_final_norm:
        out = _rms(out, gf_ref[...])
    o_ref[...] = out


def _layer_spec(stacked_shape, layer):
    zeros = (0,) * (len(stacked_shape) - 1)
    return pl.BlockSpec((None,) + tuple(stacked_shape[1:]), lambda i: (layer,) + zeros,
                        pipeline_mode=pl.Buffered(1))


def _ffn_layer(x2, g, wg, wu, wd, layer, final_g=None):
    n, d = x2.shape
    t = WIDE_TILE
    in_specs = [
        pl.BlockSpec((t, d), lambda i: (i, 0)),
        _const_spec((1, d)),
        _layer_spec(wg.shape, layer),
        _layer_spec(wu.shape, layer),
        _layer_spec(wd.shape, layer),
    ]
    args = [x2, g.reshape(1, d), wg, wu, wd]
    if final_g is not None:
        in_specs.append(_const_spec((1, d)))
        args.append(final_g.reshape(1, d))
    return pl.pallas_call(
        functools.partial(_ffn_kernel, final_g is not None),
        grid=(n // t,),
        in_specs=in_specs,
        out_specs=pl.BlockSpec((t, d), lambda i: (i, 0)),
        out_shape=jax.ShapeDtypeStruct((n, d), _F32),
        scratch_shapes=[pltpu.VMEM((t, d), _BF16), pltpu.VMEM((t, d), _F32)],
        compiler_params=_params(),
        name="swiglu_ffn",
    )(*args)


def _qkv_kernel(x_ref, g_ref, wqv_ref, wk_ref, qt_ref, k_ref, vt_ref):
    h = _rms(x_ref[...], g_ref[...]).astype(_BF16)
    qv = _dot_nt(wqv_ref[...], h)
    q_rows = N_HEADS * HEAD_DIM
    scale = HEAD_DIM ** -0.5 * LOG2_E
    for j in range(x_ref.shape[0] // BLOCK):
        lanes = slice(j * BLOCK, (j + 1) * BLOCK)
        qt_ref[j] = (qv[:q_rows, lanes] * scale).astype(_BF16)
        vt_ref[j] = qv[q_rows:, lanes].astype(_BF16)
    k_ref[...] = _dot(h, wk_ref[...]).astype(_BF16)


def _qkv_layer(x2, g, w_qkv):
    n, d = x2.shape
    t = QKV_TILE
    q_cols = N_HEADS * HEAD_DIM
    kv_cols = N_KV_HEADS * HEAD_DIM
    wq = w_qkv[:, :q_cols]
    wk = w_qkv[:, q_cols:q_cols + kv_cols]
    wv = w_qkv[:, q_cols + kv_cols:]
    wqv_t = jnp.concatenate([wq, wv], axis=1).T.astype(_BF16)
    n_blocks = n // BLOCK
    return pl.pallas_call(
        _qkv_kernel,
        grid=(n // t,),
        in_specs=[
            pl.BlockSpec((t, d), lambda i: (i, 0)),
            _const_spec((1, d)),
            _const_spec(wqv_t.shape),
            _const_spec((d, kv_cols)),
        ],
        out_specs=[
            pl.BlockSpec((t // BLOCK, q_cols, BLOCK), lambda i: (i, 0, 0)),
            pl.BlockSpec((t, kv_cols), lambda i: (i, 0)),
            pl.BlockSpec((t // BLOCK, kv_cols, BLOCK), lambda i: (i, 0, 0)),
        ],
        out_shape=[
            jax.ShapeDtypeStruct((n_blocks, q_cols, BLOCK), _BF16),
            jax.ShapeDtypeStruct((n, kv_cols), _BF16),
            jax.ShapeDtypeStruct((n_blocks, kv_cols, BLOCK), _BF16),
        ],
        compiler_params=_params(),
        name="qkv_proj",
    )(x2, g.reshape(1, d), wqv_t, wk.astype(_BF16))


def _t5_log_thresholds():
    nb = N_BUCKETS // 2
    max_exact = nb // 2
    steps = nb - max_exact
    ratio = MAX_DISTANCE // max_exact
    assert ratio * max_exact == MAX_DISTANCE
    thresholds = []
    n = max_exact
    for k in range(1, steps):
        while n ** steps < max_exact ** steps * ratio ** k:
            n += 1
        thresholds.append(n)
    return thresholds


def _t5_buckets(rel):
    nb = N_BUCKETS // 2
    max_exact = nb // 2
    n = jnp.abs(rel)
    large = jnp.full(rel.shape, max_exact, jnp.int32)
    for thr in _t5_log_thresholds():
        large = large + (n >= thr).astype(jnp.int32)
    return jnp.where(rel > 0, nb, 0) + jnp.where(n < max_exact, n, large)


def _attn_kernel(tiles_per_seq, sink_ref, rb_ref, x_ref, qt_ref, kc_ref, kp_ref, kn_ref,
                 vc_ref, vp_ref, vn_ref, wo_ref, o_ref,
                 bias_ref, kall_ref, vall_ref, sc_ref, p_ref, at_ref):
    i = pl.program_id(0)
    pos = i % tiles_per_seq

    @pl.when(i == 0)
    def _init():
        rel = (lax.broadcasted_iota(jnp.int32, (KEY_SPAN, BLOCK), 0) - BLOCK
               - lax.broadcasted_iota(jnp.int32, (KEY_SPAN, BLOCK), 1))
        band = jnp.abs(rel) <= WINDOW
        bkt = _t5_buckets(rel)

        def fill_head(hd, carry):
            tbl = jnp.zeros((KEY_SPAN, BLOCK), _F32)
            for b in range(N_BUCKETS):
                tbl = jnp.where(bkt == b, rb_ref[b * N_HEADS + hd], tbl)
            bias_ref[hd] = jnp.where(band, tbl * LOG2_E, MASK_VALUE)
            return carry

        lax.fori_loop(0, N_HEADS, fill_head, 0)

    kall_ref[0:BLOCK, :] = kp_ref[...]
    kall_ref[BLOCK:BLOCK + TOKEN_TILE, :] = kc_ref[...]
    kall_ref[BLOCK + TOKEN_TILE:, :] = kn_ref[...]
    vall_ref[0] = vp_ref[0]
    for j in range(SUB_BLOCKS):
        vall_ref[1 + j] = vc_ref[j]
    vall_ref[1 + SUB_BLOCKS] = vn_ref[0]

    pen_first = jnp.where(pos == 0, MASK_VALUE, 0.0)
    pen_last = jnp.where(pos == tiles_per_seq - 1, MASK_VALUE, 0.0)

    pair_lanes = 2 * HEAD_DIM
    slabs = [slice(r0, r0 + ROW_SLAB) for r0 in range(0, KEY_SPAN, ROW_SLAB)]
    heads = [(j, kvh) for j in range(SUB_BLOCKS) for kvh in range(N_KV_HEADS)]

    def scores(t):
        j, kvh = heads[t]
        pair, half = kvh // 2, kvh % 2
        q_heads = jnp.concatenate(
            [qt_ref[j, (kvh * Q_PER_KV + grp) * HEAD_DIM:(kvh * Q_PER_KV + grp + 1) * HEAD_DIM, :]
             for grp in range(Q_PER_KV)], axis=1)
        q_zero = jnp.zeros_like(q_heads)
        q2 = jnp.concatenate([q_heads, q_zero] if half == 0 else [q_zero, q_heads], axis=0)
        kwin = kall_ref[j * BLOCK:j * BLOCK + KEY_SPAN, pair * pair_lanes:(pair + 1) * pair_lanes]
        raw = _dot(kwin, q2)
        sc_buf = sc_ref.at[t % SCORE_AHEAD]
        maxes = []
        for grp in range(Q_PER_KV):
            hd = kvh * Q_PER_KV + grp
            lanes = slice(grp * BLOCK, (grp + 1) * BLOCK)
            mx = None
            for rows in slabs:
                s = raw[rows, lanes] + bias_ref[hd, rows, :]
                if j == 0 and rows.stop <= BLOCK:
                    s = s + pen_first
                if j == SUB_BLOCKS - 1 and rows.start >= 2 * BLOCK:
                    s = s + pen_last
                sc_buf[rows, lanes] = s
                mx = s if mx is None else jnp.maximum(mx, s)
            maxes.append(jnp.maximum(jnp.max(mx, axis=0, keepdims=True), sink_ref[hd] * LOG2_E))
        return maxes

    def softmax(t, maxes):
        j, kvh = heads[t]
        sc_buf = sc_ref.at[t % SCORE_AHEAD]
        p_buf = p_ref.at[t % SCORE_AHEAD]
        inv_l = []
        for grp in range(Q_PER_KV):
            hd = kvh * Q_PER_KV + grp
            lanes = slice(grp * BLOCK, (grp + 1) * BLOCK)
            m = maxes[grp]
            acc = None
            for rows in slabs:
                p = jnp.exp2(sc_buf[rows, lanes] - m)
                acc = p if acc is None else acc + p
                p_buf[rows, lanes] = p.astype(_BF16)
            l = jnp.sum(acc, axis=0, keepdims=True) + jnp.exp2(sink_ref[hd] * LOG2_E - m)
            inv_l.append(1.0 / l)
        return inv_l

    def weighted_values(t, inv_l):
        j, kvh = heads[t]
        vwin = jnp.concatenate(
            [vall_ref[j + b, kvh * HEAD_DIM:(kvh + 1) * HEAD_DIM, :] for b in range(3)], axis=1)
        ot = _dot(vwin, p_ref[t % SCORE_AHEAD])
        for grp in range(Q_PER_KV):
            hd = kvh * Q_PER_KV + grp
            at_ref[j, hd * HEAD_DIM:(hd + 1) * HEAD_DIM, :] = (
                ot[:, grp * BLOCK:(grp + 1) * BLOCK] * inv_l[grp]).astype(_BF16)

    def out_proj_piece(j, piece):
        rows = slice(j * BLOCK, (j + 1) * BLOCK)
        cols = slice(piece * OUT_PIECE, (piece + 1) * OUT_PIECE)
        o_ref[rows, cols] = x_ref[rows, cols] + _dot_tn(at_ref[j], wo_ref[:, cols])

    n_pieces = D_MODEL // OUT_PIECE
    maxes = {t: scores(t) for t in range(SCORE_AHEAD)}
    for t0 in range(0, len(heads), HEAD_GROUP):
        group = range(t0, t0 + HEAD_GROUP)
        inv_ls = {t: softmax(t, maxes.pop(t)) for t in group}
        for t in group:
            weighted_values(t, inv_ls[t])
        for t in group:
            j, kvh = heads[t]
            if j > 0:
                for piece in range(n_pieces):
                    if (piece * N_KV_HEADS) // n_pieces == kvh:
                        out_proj_piece(j - 1, piece)
        for t in group:
            if t + SCORE_AHEAD < len(heads):
                maxes[t + SCORE_AHEAD] = scores(t + SCORE_AHEAD)
    for piece in range(D_MODEL // OUT_PIECE):
        out_proj_piece(SUB_BLOCKS - 1, piece)


def _attn_layer(x2, qt, k, vt, sink, rel_bias, w_out, seq_len):
    n, d = x2.shape
    t = TOKEN_TILE
    n_blocks = n // BLOCK
    kv_cols = N_KV_HEADS * HEAD_DIM
    q_cols = N_HEADS * HEAD_DIM

    def prev_block(i):
        return jnp.maximum(i * SUB_BLOCKS - 1, 0)

    def next_block(i):
        return jnp.minimum((i + 1) * SUB_BLOCKS, n_blocks - 1)

    smem = pl.BlockSpec(memory_space=pltpu.SMEM)
    return pl.pallas_call(
        functools.partial(_attn_kernel, seq_len // t),
        grid=(n // t,),
        in_specs=[
            smem,
            smem,
            pl.BlockSpec((t, d), lambda i: (i, 0)),
            pl.BlockSpec((SUB_BLOCKS, q_cols, BLOCK), lambda i: (i, 0, 0)),
            pl.BlockSpec((t, kv_cols), lambda i: (i, 0)),
            pl.BlockSpec((BLOCK, kv_cols), lambda i: (prev_block(i), 0)),
            pl.BlockSpec((BLOCK, kv_cols), lambda i: (next_block(i), 0)),
            pl.BlockSpec((SUB_BLOCKS, kv_cols, BLOCK), lambda i: (i, 0, 0)),
            pl.BlockSpec((1, kv_cols, BLOCK), lambda i: (prev_block(i), 0, 0)),
            pl.BlockSpec((1, kv_cols, BLOCK), lambda i: (next_block(i), 0, 0)),
            _const_spec((d, d)),
        ],
        out_specs=pl.BlockSpec((t, d), lambda i: (i, 0)),
        out_shape=jax.ShapeDtypeStruct((n, d), _F32),
        scratch_shapes=[
            pltpu.VMEM((N_HEADS, KEY_SPAN, BLOCK), _F32),
            pltpu.VMEM((TOKEN_TILE + 2 * BLOCK, kv_cols), _BF16),
            pltpu.VMEM((SUB_BLOCKS + 2, kv_cols, BLOCK), _BF16),
            pltpu.VMEM((SCORE_AHEAD, KEY_SPAN, Q_PER_KV * BLOCK), _F32),
            pltpu.VMEM((SCORE_AHEAD, KEY_SPAN, Q_PER_KV * BLOCK), _BF16),
            pltpu.VMEM((SUB_BLOCKS, q_cols, BLOCK), _BF16),
        ],
        compiler_params=_params(),
        name="window_attention",
    )(sink, rel_bias.reshape(-1), x2, qt, k, k, k, vt, vt, vt, w_out.astype(_BF16))


def kernel(x, norm_mix, norm_ffn, even_w_in, even_v_ln_g, even_v_ln_b, even_w_spatial, even_b_spatial,
           even_conv_w, even_w_out, attn_w_qkv, attn_sink, rel_bias, attn_w_out, ffn_w_gate, ffn_w_up,
           ffn_w_down, final_norm):
    bsz, seq_len, d = x.shape
    assert d == D_MODEL and seq_len % WIDE_TILE == 0 and (bsz * seq_len) % QKV_TILE == 0
    x2 = x.reshape(bsz * seq_len, d)
    wg, wu, wd = ffn_w_gate.astype(_BF16), ffn_w_up.astype(_BF16), ffn_w_down.astype(_BF16)
    x2 = _even_layer(x2, norm_mix[0], even_w_in[0], even_v_ln_g[0], even_v_ln_b[0], even_w_spatial[0],
                     even_b_spatial[0], even_conv_w[0], even_w_out[0], seq_len)
    x2 = _ffn_layer(x2, norm_ffn[0], wg, wu, wd, 0)
    qt, k, vt = _qkv_layer(x2, norm_mix[1], attn_w_qkv[0])
    x2 = _attn_layer(x2, qt, k, vt, attn_sink[0], rel_bias, attn_w_out[0], seq_len)
    x2 = _ffn_layer(x2, norm_ffn[1], wg, wu, wd, 1, final_g=final_norm)
    return x2.reshape(bsz, seq_len, d)
```

```python
import functools
import math

import jax
import jax.numpy as jnp
from jax import lax
from jax.experimental import pallas as pl
from jax.experimental.pallas import tpu as pltpu

D_MODEL = 1024
EPS = 1e-6
A_WIDTH = 512
B_WIDTH = 512
A_GROUPS = 4
A_GROUP_DIM = 128
CHUNK = 128
CONV_W = 3
N_HEADS = 16
N_KV_HEADS = 4
Q_PER_KV = 4
HEAD_DIM = 64
WINDOW = 128
BLOCK = 128
N_BUCKETS = 32
MAX_DISTANCE = 128
D_FF = 2816

TOKEN_TILE = 1024
WIDE_TILE = 1024
QKV_TILE = 2048
HALO = 16
EVEN_PARTS = 2
FF_CHUNK = 256
N_FF_CHUNKS = D_FF // FF_CHUNK
SUB_BLOCKS = TOKEN_TILE // BLOCK
KEY_SPAN = 3 * BLOCK
MASK_VALUE = float("-inf")
LOG2_E = math.log2(math.e)
ROW_SLAB = 64
OUT_PIECE = 256
SCORE_AHEAD = 2
V7X_VMEM_BYTES = 64 * 1024 * 1024
VMEM_LIMIT_BYTES = V7X_VMEM_BYTES * 7 // 8

_BF16 = jnp.bfloat16
_F32 = jnp.float32


def _dot(a, b):
    return jnp.dot(a, b, preferred_element_type=_F32)


def _dot_nt(a, b):
    return lax.dot_general(a, b, (((1,), (1,)), ((), ())), preferred_element_type=_F32)


def _dot_tn(a, b):
    return lax.dot_general(a, b, (((0,), (0,)), ((), ())), preferred_element_type=_F32)


def _gelu(x):
    return 0.5 * x * (1.0 + lax.erf(x * math.sqrt(0.5)))


def _rms(x, g):
    return (x * lax.rsqrt(jnp.mean(x * x, axis=-1, keepdims=True) + EPS)) * g


def _const_spec(shape):
    zeros = (0,) * len(shape)
    return pl.BlockSpec(shape, lambda i: zeros, pipeline_mode=pl.Buffered(1))


def _params():
    return pltpu.CompilerParams(
        dimension_semantics=("arbitrary",), vmem_limit_bytes=VMEM_LIMIT_BYTES)


def _even_kernel(tiles_per_seq, x_ref, xp_ref, xn_ref, g_ref, win_ref, lng_ref, lnb_ref,
                 wsp_ref, bsp_ref, cw_ref, wo_ref, ffn_g_ref, ffn_u_ref, ffn_d_ref,
                 o_ref, ffn_g16_ref, ffn_u16_ref, ffn_d16_ref, hs_ref, conv_ref, y_ref):
    t = WIDE_TILE
    i = pl.program_id(0)
    pos = i % tiles_per_seq
    ffn_g16_ref[...] = ffn_g_ref[...].astype(_BF16)
    ffn_u16_ref[...] = ffn_u_ref[...].astype(_BF16)
    ffn_d16_ref[...] = ffn_d_ref[...].astype(_BF16)
    g = g_ref[...]
    h = _rms(x_ref[...], g).astype(_BF16)
    hp = jnp.where(pos == 0, 0.0, _rms(xp_ref[...], g)).astype(_BF16)
    hn = jnp.where(pos == tiles_per_seq - 1, 0.0, _rms(xn_ref[...], g)).astype(_BF16)
    hs_ref[0:HALO, :] = hp
    hs_ref[HALO:HALO + t, :] = h
    hs_ref[HALO + t:, :] = hn

    in_width_a = 2 * A_WIDTH + B_WIDTH
    pc = _dot(hs_ref[...], win_ref[:, in_width_a:])

    z = pc[:, :B_WIDTH] * pc[:, B_WIDTH:]
    n_rows = t + 2 * HALO
    z_prev = pltpu.roll(z, 1, 0)[HALO:HALO + t]
    z_next = pltpu.roll(z, n_rows - 1, 0)[HALO:HALO + t]
    cw = cw_ref[...]
    conv_ref[...] = z_prev * cw[0:1] + z[HALO:HALO + t] * cw[1:2] + z_next * cw[2:3]

    part = t // EVEN_PARTS
    parts = [slice(q * part, (q + 1) * part) for q in range(EVEN_PARTS)]
    pas = [_dot(hs_ref[HALO + rows.start:HALO + rows.stop, :], win_ref[:, :in_width_a])
           for rows in parts]
    for rows, pa in zip(parts, pas):
        u = _gelu(pa[:, :A_WIDTH])
        v = _gelu(pa[:, A_WIDTH:2 * A_WIDTH])
        mu = jnp.mean(v, axis=-1, keepdims=True)
        vc = v - mu
        vn = vc * lax.rsqrt(jnp.mean(vc * vc, axis=-1, keepdims=True) + EPS)
        vb = (vn * lng_ref[...] + lnb_ref[...]).astype(_BF16)
        chunks = [slice(c * CHUNK, (c + 1) * CHUNK) for c in range(part // CHUNK)]
        for grp in range(A_GROUPS):
            cols = slice(grp * A_GROUP_DIM, (grp + 1) * A_GROUP_DIM)
            v_chunks = jnp.concatenate([vb[crows, cols] for crows in chunks], axis=1)
            mixed = _dot(wsp_ref[grp], v_chunks) + bsp_ref[grp]
            for c, crows in enumerate(chunks):
                orows = slice(rows.start + crows.start, rows.start + crows.stop)
                y_ref[orows, cols] = (u[crows, cols]
                                      * mixed[:, c * A_GROUP_DIM:(c + 1) * A_GROUP_DIM]).astype(_BF16)
        y_ref[rows, A_WIDTH:] = (pa[:, 2 * A_WIDTH:] * conv_ref[rows, :]).astype(_BF16)
        o_ref[rows, :] = x_ref[rows, :] + _dot(y_ref[rows, :], wo_ref[...])


def _even_layer(x2, g, w_in, ln_g, ln_b, w_sp, b_sp, conv_w, w_out, ffn_w_gate, ffn_w_up, ffn_w_down, seq_len):
    n, d = x2.shape
    t = WIDE_TILE
    n_steps = n // t
    halo_blocks_per_tile = t // HALO
    n_halo_blocks = n // HALO
    gu_rows = d // n_steps
    dn_rows = ffn_w_down.shape[1] // (n_steps // 2)
    assert gu_rows * n_steps == d and dn_rows * (n_steps // 2) == ffn_w_down.shape[1]
    assert gu_rows % 16 == 0 and dn_rows % 16 == 0
    depth, _, d_ff = ffn_w_gate.shape
    gu_spec = pl.BlockSpec((depth, gu_rows, d_ff), lambda i: (0, i, 0))
    dn_spec = pl.BlockSpec((depth, dn_rows, d), lambda i: (0, i // 2, 0))
    return pl.pallas_call(
        functools.partial(_even_kernel, seq_len // t),
        grid=(n_steps,),
        in_specs=[
            pl.BlockSpec((t, d), lambda i: (i, 0)),
            pl.BlockSpec((HALO, d), lambda i: (jnp.maximum(i * halo_blocks_per_tile - 1, 0), 0)),
            pl.BlockSpec((HALO, d),
                         lambda i: (jnp.minimum((i + 1) * halo_blocks_per_tile, n_halo_blocks - 1), 0)),
            _const_spec((1, d)),
            _const_spec(w_in.shape),
            _const_spec((1, A_WIDTH)),
            _const_spec((1, A_WIDTH)),
            _const_spec((A_GROUPS, CHUNK, CHUNK)),
            _const_spec((A_GROUPS, CHUNK, 1)),
            _const_spec((CONV_W, B_WIDTH)),
            _const_spec((d, d)),
            gu_spec,
            gu_spec,
            dn_spec,
        ],
        out_specs=[pl.BlockSpec((t, d), lambda i: (i, 0)), gu_spec, gu_spec, dn_spec],
        out_shape=[
            jax.ShapeDtypeStruct((n, d), _F32),
            jax.ShapeDtypeStruct(ffn_w_gate.shape, _BF16),
            jax.ShapeDtypeStruct(ffn_w_up.shape, _BF16),
            jax.ShapeDtypeStruct(ffn_w_down.shape, _BF16),
        ],
        scratch_shapes=[
            pltpu.VMEM((t + 2 * HALO, d), _BF16),
            pltpu.VMEM((t, B_WIDTH), _F32),
            pltpu.VMEM((t, d), _BF16),
        ],
        compiler_params=_params(),
        name="even_mixer",
    )(x2, x2, x2, g.reshape(1, d), w_in.astype(_BF16), ln_g.reshape(1, -1), ln_b.reshape(1, -1),
      w_sp.astype(_BF16), b_sp[..., None], conv_w, w_out.astype(_BF16), ffn_w_gate, ffn_w_up, ffn_w_down)


def _ffn_kernel(has_final_norm, x_ref, g_ref, wg_ref, wu_ref, wd_ref, *rest):
    if has_final_norm:
        gf_ref, o_ref, h_ref, acc_ref = rest
    else:
        o_ref, h_ref, acc_ref = rest
    x = x_ref[...]
    h_ref[...] = _rms(x, g_ref[...]).astype(_BF16)
    acc_ref[...] = x

    h = h_ref[...]
    for c in range(N_FF_CHUNKS):
        cols = slice(c * FF_CHUNK, (c + 1) * FF_CHUNK)
        gate = _dot(h, wg_ref[:, cols])
        up = _dot(h, wu_ref[:, cols])
        act = (jax.nn.silu(gate) * up).astype(_BF16)
        acc_ref[...] += _dot(act, wd_ref[cols, :])
    out = acc_ref[...]
    if has_final_norm:
        out = _rms(out, gf_ref[...])
    o_ref[...] = out


def _layer_spec(stacked_shape, layer):
    zeros = (0,) * (len(stacked_shape) - 1)
    return pl.BlockSpec((None,) + tuple(stacked_shape[1:]), lambda i: (layer,) + zeros,
                        pipeline_mode=pl.Buffered(1))


def _ffn_layer(x2, g, wg, wu, wd, layer, final_g=None):
    n, d = x2.shape
    t = WIDE_TILE
    in_specs = [
        pl.BlockSpec((t, d), lambda i: (i, 0)),
        _const_spec((1, d)),
        _layer_spec(wg.shape, layer),
        _layer_spec(wu.shape, layer),
        _layer_spec(wd.shape, layer),
    ]
    args = [x2, g.reshape(1, d), wg, wu, wd]
    if final_g is not None:
        in_specs.append(_const_spec((1, d)))
        args.append(final_g.reshape(1, d))
    return pl.pallas_call(
        functools.partial(_ffn_kernel, final_g is not None),
        grid=(n // t,),
        in_specs=in_specs,
        out_specs=pl.BlockSpec((t, d), lambda i: (i, 0)),
        out_shape=jax.ShapeDtypeStruct((n, d), _F32),
        scratch_shapes=[pltpu.VMEM((t, d), _BF16), pltpu.VMEM((t, d), _F32)],
        compiler_params=_params(),
        name="swiglu_ffn",
    )(*args)


def _qkv_kernel(x_ref, g_ref, wqv_ref, wk_ref, qt_ref, k_ref, vt_ref):
    h = _rms(x_ref[...], g_ref[...]).astype(_BF16)
    qv = _dot_nt(wqv_ref[...], h)
    q_rows = N_HEADS * HEAD_DIM
    scale = HEAD_DIM ** -0.5 * LOG2_E
    for j in range(x_ref.shape[0] // BLOCK):
        lanes = slice(j * BLOCK, (j + 1) * BLOCK)
        qt_ref[j] = (qv[:q_rows, lanes] * scale).astype(_BF16)
        vt_ref[j] = qv[q_rows:, lanes].astype(_BF16)
    k_ref[...] = _dot(h, wk_ref[...]).astype(_BF16)


def _qkv_layer(x2, g, w_qkv):
    n, d = x2.shape
    t = QKV_TILE
    q_cols = N_HEADS * HEAD_DIM
    kv_cols = N_KV_HEADS * HEAD_DIM
    wq = w_qkv[:, :q_cols]
    wk = w_qkv[:, q_cols:q_cols + kv_cols]
    wv = w_qkv[:, q_cols + kv_cols:]
    wqv_t = jnp.concatenate([wq, wv], axis=1).T.astype(_BF16)
    n_blocks = n // BLOCK
    return pl.pallas_call(
        _qkv_kernel,
        grid=(n // t,),
        in_specs=[
            pl.BlockSpec((t, d), lambda i: (i, 0)),
            _const_spec((1, d)),
            _const_spec(wqv_t.shape),
            _const_spec((d, kv_cols)),
        ],
        out_specs=[
            pl.BlockSpec((t // BLOCK, q_cols, BLOCK), lambda i: (i, 0, 0)),
            pl.BlockSpec((t, kv_cols), lambda i: (i, 0)),
            pl.BlockSpec((t // BLOCK, kv_cols, BLOCK), lambda i: (i, 0, 0)),
        ],
        out_shape=[
            jax.ShapeDtypeStruct((n_blocks, q_cols, BLOCK), _BF16),
            jax.ShapeDtypeStruct((n, kv_cols), _BF16),
            jax.ShapeDtypeStruct((n_blocks, kv_cols, BLOCK), _BF16),
        ],
        compiler_params=_params(),
        name="qkv_proj",
    )(x2, g.reshape(1, d), wqv_t, wk.astype(_BF16))


def _t5_log_thresholds():
    nb = N_BUCKETS // 2
    max_exact = nb // 2
    steps = nb - max_exact
    ratio = MAX_DISTANCE // max_exact
    assert ratio * max_exact == MAX_DISTANCE
    thresholds = []
    n = max_exact
    for k in range(1, steps):
        while n ** steps < max_exact ** steps * ratio ** k:
            n += 1
        thresholds.append(n)
    return thresholds


def _t5_buckets(rel):
    nb = N_BUCKETS // 2
    max_exact = nb // 2
    n = jnp.abs(rel)
    large = jnp.full(rel.shape, max_exact, jnp.int32)
    for thr in _t5_log_thresholds():
        large = large + (n >= thr).astype(jnp.int32)
    return jnp.where(rel > 0, nb, 0) + jnp.where(n < max_exact, n, large)


def _attn_kernel(tiles_per_seq, sink_ref, rb_ref, x_ref, qt_ref, kc_ref, kp_ref, kn_ref,
                 vc_ref, vp_ref, vn_ref, wo_ref, o_ref,
                 bias_ref, kall_ref, vall_ref, sc_ref, p_ref, at_ref):
    i = pl.program_id(0)
    pos = i % tiles_per_seq

    @pl.when(i == 0)
    def _init():
        rel = (lax.broadcasted_iota(jnp.int32, (KEY_SPAN, BLOCK), 0) - BLOCK
               - lax.broadcasted_iota(jnp.int32, (KEY_SPAN, BLOCK), 1))
        band = jnp.abs(rel) <= WINDOW
        bkt = _t5_buckets(rel)

        def fill_head(hd, carry):
            tbl = jnp.zeros((KEY_SPAN, BLOCK), _F32)
            for b in range(N_BUCKETS):
                tbl = jnp.where(bkt == b, rb_ref[b * N_HEADS + hd], tbl)
            bias_ref[hd] = jnp.where(band, tbl * LOG2_E, MASK_VALUE)
            return carry

        lax.fori_loop(0, N_HEADS, fill_head, 0)

    kall_ref[0:BLOCK, :] = kp_ref[...]
    kall_ref[BLOCK:BLOCK + TOKEN_TILE, :] = kc_ref[...]
    kall_ref[BLOCK + TOKEN_TILE:, :] = kn_ref[...]
    vall_ref[0] = vp_ref[0]
    for j in range(SUB_BLOCKS):
        vall_ref[1 + j] = vc_ref[j]
    vall_ref[1 + SUB_BLOCKS] = vn_ref[0]

    pen_first = jnp.where(pos == 0, MASK_VALUE, 0.0)
    pen_last = jnp.where(pos == tiles_per_seq - 1, MASK_VALUE, 0.0)

    pair_lanes = 2 * HEAD_DIM
    slabs = [slice(r0, r0 + ROW_SLAB) for r0 in range(0, KEY_SPAN, ROW_SLAB)]
    heads = [(j, kvh) for j in range(SUB_BLOCKS) for kvh in range(N_KV_HEADS)]

    def scores(t):
        j, kvh = heads[t]
        pair, half = kvh // 2, kvh % 2
        q_heads = jnp.concatenate(
            [qt_ref[j, (kvh * Q_PER_KV + grp) * HEAD_DIM:(kvh * Q_PER_KV + grp + 1) * HEAD_DIM, :]
             for grp in range(Q_PER_KV)], axis=1)
        q_zero = jnp.zeros_like(q_heads)
        q2 = jnp.concatenate([q_heads, q_zero] if half == 0 else [q_zero, q_heads], axis=0)
        kwin = kall_ref[j * BLOCK:j * BLOCK + KEY_SPAN, pair * pair_lanes:(pair + 1) * pair_lanes]
        raw = _dot(kwin, q2)
        sc_buf = sc_ref.at[t % SCORE_AHEAD]
        maxes = []
        for grp in range(Q_PER_KV):
            hd = kvh * Q_PER_KV + grp
            lanes = slice(grp * BLOCK, (grp + 1) * BLOCK)
            mx = None
            for rows in slabs:
                s = raw[rows, lanes] + bias_ref[hd, rows, :]
                if j == 0 and rows.stop <= BLOCK:
                    s = s + pen_first
                if j == SUB_BLOCKS - 1 and rows.start >= 2 * BLOCK:
                    s = s + pen_last
                sc_buf[rows, lanes] = s
                mx = s if mx is None else jnp.maximum(mx, s)
            maxes.append(jnp.maximum(jnp.max(mx, axis=0, keepdims=True), sink_ref[hd] * LOG2_E))
        return maxes

    def softmax(t, maxes):
        j, kvh = heads[t]
        sc_buf = sc_ref.at[t % SCORE_AHEAD]
        p_buf = p_ref.at[t % SCORE_AHEAD]
        inv_l = []
        for grp in range(Q_PER_KV):
            hd = kvh * Q_PER_KV + grp
            lanes = slice(grp * BLOCK, (grp + 1) * BLOCK)
            m = maxes[grp]
            acc = None
            for rows in slabs:
                p = jnp.exp2(sc_buf[rows, lanes] - m)
                acc = p if acc is None else acc + p
                p_buf[rows, lanes] = p.astype(_BF16)
            l = jnp.sum(acc, axis=0, keepdims=True) + jnp.exp2(sink_ref[hd] * LOG2_E - m)
            inv_l.append(1.0 / l)
        return inv_l

    def weighted_values(t, inv_l):
        j, kvh = heads[t]
        vwin = jnp.concatenate(
            [vall_ref[j + b, kvh * HEAD_DIM:(kvh + 1) * HEAD_DIM, :] for b in range(3)], axis=1)
        ot = _dot(vwin, p_ref[t % SCORE_AHEAD])
        for grp in range(Q_PER_KV):
            hd = kvh * Q_PER_KV + grp
            at_ref[j, hd * HEAD_DIM:(hd + 1) * HEAD_DIM, :] = (
                ot[:, grp * BLOCK:(grp + 1) * BLOCK] * inv_l[grp]).astype(_BF16)

    def out_proj_piece(j, piece):
        rows = slice(j * BLOCK, (j + 1) * BLOCK)
        cols = slice(piece * OUT_PIECE, (piece + 1) * OUT_PIECE)
        o_ref[rows, cols] = x_ref[rows, cols] + _dot_tn(at_ref[j], wo_ref[:, cols])

    n_pieces = D_MODEL // OUT_PIECE
    maxes = {t: scores(t) for t in range(SCORE_AHEAD)}
    for t, (j, kvh) in enumerate(heads):
        inv_l = softmax(t, maxes.pop(t))
        weighted_values(t, inv_l)
        if j > 0:
            for piece in range(n_pieces):
                if (piece * N_KV_HEADS) // n_pieces == kvh:
                    out_proj_piece(j - 1, piece)
        if t + SCORE_AHEAD < len(heads):
            maxes[t + SCORE_AHEAD] = scores(t + SCORE_AHEAD)
    for piece in range(D_MODEL // OUT_PIECE):
        out_proj_piece(SUB_BLOCKS - 1, piece)


def _attn_layer(x2, qt, k, vt, sink, rel_bias, w_out, seq_len):
    n, d = x2.shape
    t = TOKEN_TILE
    n_blocks = n // BLOCK
    kv_cols = N_KV_HEADS * HEAD_DIM
    q_cols = N_HEADS * HEAD_DIM

    def prev_block(i):
        return jnp.maximum(i * SUB_BLOCKS - 1, 0)

    def next_block(i):
        return jnp.minimum((i + 1) * SUB_BLOCKS, n_blocks - 1)

    smem = pl.BlockSpec(memory_space=pltpu.SMEM)
    return pl.pallas_call(
        functools.partial(_attn_kernel, seq_len // t),
        grid=(n // t,),
        in_specs=[
            smem,
            smem,
            pl.BlockSpec((t, d), lambda i: (i, 0)),
            pl.BlockSpec((SUB_BLOCKS, q_cols, BLOCK), lambda i: (i, 0, 0)),
            pl.BlockSpec((t, kv_cols), lambda i: (i, 0)),
            pl.BlockSpec((BLOCK, kv_cols), lambda i: (prev_block(i), 0)),
            pl.BlockSpec((BLOCK, kv_cols), lambda i: (next_block(i), 0)),
            pl.BlockSpec((SUB_BLOCKS, kv_cols, BLOCK), lambda i: (i, 0, 0)),
            pl.BlockSpec((1, kv_cols, BLOCK), lambda i: (prev_block(i), 0, 0)),
            pl.BlockSpec((1, kv_cols, BLOCK), lambda i: (next_block(i), 0, 0)),
            _const_spec((d, d)),
        ],
        out_specs=pl.BlockSpec((t, d), lambda i: (i, 0)),
        out_shape=jax.ShapeDtypeStruct((n, d), _F32),
        scratch_shapes=[
            pltpu.VMEM((N_HEADS, KEY_SPAN, BLOCK), _F32),
            pltpu.VMEM((TOKEN_TILE + 2 * BLOCK, kv_cols), _BF16),
            pltpu.VMEM((SUB_BLOCKS + 2, kv_cols, BLOCK), _BF16),
            pltpu.VMEM((SCORE_AHEAD, KEY_SPAN, Q_PER_KV * BLOCK), _F32),
            pltpu.VMEM((SCORE_AHEAD, KEY_SPAN, Q_PER_KV * BLOCK), _BF16),
            pltpu.VMEM((SUB_BLOCKS, q_cols, BLOCK), _BF16),
        ],
        compiler_params=_params(),
        name="window_attention",
    )(sink, rel_bias.reshape(-1), x2, qt, k, k, k, vt, vt, vt, w_out.astype(_BF16))


def kernel(x, norm_mix, norm_ffn, even_w_in, even_v_ln_g, even_v_ln_b, even_w_spatial, even_b_spatial,
           even_conv_w, even_w_out, attn_w_qkv, attn_sink, rel_bias, attn_w_out, ffn_w_gate, ffn_w_up,
           ffn_w_down, final_norm):
    bsz, seq_len, d = x.shape
    assert d == D_MODEL and seq_len % WIDE_TILE == 0 and (bsz * seq_len) % QKV_TILE == 0
    x2 = x.reshape(bsz * seq_len, d)
    x2, wg, wu, wd = _even_layer(x2, norm_mix[0], even_w_in[0], even_v_ln_g[0], even_v_ln_b[0],
                                 even_w_spatial[0], even_b_spatial[0], even_conv_w[0], even_w_out[0],
                                 ffn_w_gate, ffn_w_up, ffn_w_down, seq_len)
    x2 = _ffn_layer(x2, norm_ffn[0], wg, wu, wd, 0)
    qt, k, vt = _qkv_layer(x2, norm_mix[1], attn_w_qkv[0])
    x2 = _attn_layer(x2, qt, k, vt, attn_sink[0], rel_bias, attn_w_out[0], seq_len)
    x2 = _ffn_layer(x2, norm_ffn[1], wg, wu, wd, 1, final_g=final_norm)
    return x2.reshape(bsz, seq_len, d)
```

```python
import functools
import math

import jax
import jax.numpy as jnp
from jax import lax
from jax.experimental import pallas as pl
from jax.experimental.pallas import tpu as pltpu

D_MODEL = 1024
EPS = 1e-6
A_WIDTH = 512
B_WIDTH = 512
A_GROUPS = 4
A_GROUP_DIM = 128
CHUNK = 128
CONV_W = 3
N_HEADS = 16
N_KV_HEADS = 4
Q_PER_KV = 4
HEAD_DIM = 64
WINDOW = 128
BLOCK = 128
N_BUCKETS = 32
MAX_DISTANCE = 128
D_FF = 2816

TOKEN_TILE = 1024
WIDE_TILE = 1024
QKV_TILE = 2048
HALO = 16
EVEN_PARTS = 2
FF_CHUNK = 256
N_FF_CHUNKS = D_FF // FF_CHUNK
FFN_OUT_PARTS = 4
SUB_BLOCKS = TOKEN_TILE // BLOCK
KEY_SPAN = 3 * BLOCK
MASK_VALUE = float("-inf")
LOG2_E = math.log2(math.e)
ROW_SLAB = 64
OUT_PIECE = 256
SCORE_AHEAD = 2
V7X_VMEM_BYTES = 64 * 1024 * 1024
VMEM_LIMIT_BYTES = V7X_VMEM_BYTES * 7 // 8

_BF16 = jnp.bfloat16
_F32 = jnp.float32


def _dot(a, b):
    return jnp.dot(a, b, preferred_element_type=_F32)


def _dot_nt(a, b):
    return lax.dot_general(a, b, (((1,), (1,)), ((), ())), preferred_element_type=_F32)


def _dot_tn(a, b):
    return lax.dot_general(a, b, (((0,), (0,)), ((), ())), preferred_element_type=_F32)


def _gelu(x):
    return 0.5 * x * (1.0 + lax.erf(x * math.sqrt(0.5)))


def _rms(x, g):
    return (x * lax.rsqrt(jnp.mean(x * x, axis=-1, keepdims=True) + EPS)) * g


def _const_spec(shape):
    zeros = (0,) * len(shape)
    return pl.BlockSpec(shape, lambda i: zeros, pipeline_mode=pl.Buffered(1))


def _params():
    return pltpu.CompilerParams(
        dimension_semantics=("arbitrary",), vmem_limit_bytes=VMEM_LIMIT_BYTES)


def _even_kernel(tiles_per_seq, x_ref, xp_ref, xn_ref, g_ref, win_ref, lng_ref, lnb_ref,
                 wsp_ref, bsp_ref, cw_ref, wo_ref, ffn_g_ref, ffn_u_ref, ffn_d_ref,
                 o_ref, ffn_g16_ref, ffn_u16_ref, ffn_d16_ref, hs_ref, conv_ref, y_ref):
    t = WIDE_TILE
    i = pl.program_id(0)
    pos = i % tiles_per_seq
    ffn_g16_ref[...] = ffn_g_ref[...].astype(_BF16)
    ffn_u16_ref[...] = ffn_u_ref[...].astype(_BF16)
    ffn_d16_ref[...] = ffn_d_ref[...].astype(_BF16)
    g = g_ref[...]
    h = _rms(x_ref[...], g).astype(_BF16)
    hp = jnp.where(pos == 0, 0.0, _rms(xp_ref[...], g)).astype(_BF16)
    hn = jnp.where(pos == tiles_per_seq - 1, 0.0, _rms(xn_ref[...], g)).astype(_BF16)
    hs_ref[0:HALO, :] = hp
    hs_ref[HALO:HALO + t, :] = h
    hs_ref[HALO + t:, :] = hn

    in_width_a = 2 * A_WIDTH + B_WIDTH
    pc = _dot(hs_ref[...], win_ref[:, in_width_a:])

    z = pc[:, :B_WIDTH] * pc[:, B_WIDTH:]
    n_rows = t + 2 * HALO
    z_prev = pltpu.roll(z, 1, 0)[HALO:HALO + t]
    z_next = pltpu.roll(z, n_rows - 1, 0)[HALO:HALO + t]
    cw = cw_ref[...]
    conv_ref[...] = z_prev * cw[0:1] + z[HALO:HALO + t] * cw[1:2] + z_next * cw[2:3]

    part = t // EVEN_PARTS
    parts = [slice(q * part, (q + 1) * part) for q in range(EVEN_PARTS)]
    pas = [_dot(hs_ref[HALO + rows.start:HALO + rows.stop, :], win_ref[:, :in_width_a])
           for rows in parts]
    for rows, pa in zip(parts, pas):
        u = _gelu(pa[:, :A_WIDTH])
        v = _gelu(pa[:, A_WIDTH:2 * A_WIDTH])
        mu = jnp.mean(v, axis=-1, keepdims=True)
        vc = v - mu
        vn = vc * lax.rsqrt(jnp.mean(vc * vc, axis=-1, keepdims=True) + EPS)
        vb = (vn * lng_ref[...] + lnb_ref[...]).astype(_BF16)
        chunks = [slice(c * CHUNK, (c + 1) * CHUNK) for c in range(part // CHUNK)]
        for grp in range(A_GROUPS):
            cols = slice(grp * A_GROUP_DIM, (grp + 1) * A_GROUP_DIM)
            v_chunks = jnp.concatenate([vb[crows, cols] for crows in chunks], axis=1)
            mixed = _dot(wsp_ref[grp], v_chunks) + bsp_ref[grp]
            for c, crows in enumerate(chunks):
                orows = slice(rows.start + crows.start, rows.start + crows.stop)
                y_ref[orows, cols] = (u[crows, cols]
                                      * mixed[:, c * A_GROUP_DIM:(c + 1) * A_GROUP_DIM]).astype(_BF16)
        y_ref[rows, A_WIDTH:] = (pa[:, 2 * A_WIDTH:] * conv_ref[rows, :]).astype(_BF16)
        o_ref[rows, :] = x_ref[rows, :] + _dot(y_ref[rows, :], wo_ref[...])


def _even_layer(x2, g, w_in, ln_g, ln_b, w_sp, b_sp, conv_w, w_out, ffn_w_gate, ffn_w_up, ffn_w_down, seq_len):
    n, d = x2.shape
    t = WIDE_TILE
    n_steps = n // t
    halo_blocks_per_tile = t // HALO
    n_halo_blocks = n // HALO
    gu_rows = d // n_steps
    dn_rows = ffn_w_down.shape[1] // (n_steps // 2)
    assert gu_rows * n_steps == d and dn_rows * (n_steps // 2) == ffn_w_down.shape[1]
    assert gu_rows % 16 == 0 and dn_rows % 16 == 0
    depth, _, d_ff = ffn_w_gate.shape
    gu_spec = pl.BlockSpec((depth, gu_rows, d_ff), lambda i: (0, i, 0))
    dn_spec = pl.BlockSpec((depth, dn_rows, d), lambda i: (0, i // 2, 0))
    return pl.pallas_call(
        functools.partial(_even_kernel, seq_len // t),
        grid=(n_steps,),
        in_specs=[
            pl.BlockSpec((t, d), lambda i: (i, 0)),
            pl.BlockSpec((HALO, d), lambda i: (jnp.maximum(i * halo_blocks_per_tile - 1, 0), 0)),
            pl.BlockSpec((HALO, d),
                         lambda i: (jnp.minimum((i + 1) * halo_blocks_per_tile, n_halo_blocks - 1), 0)),
            _const_spec((1, d)),
            _const_spec(w_in.shape),
            _const_spec((1, A_WIDTH)),
            _const_spec((1, A_WIDTH)),
            _const_spec((A_GROUPS, CHUNK, CHUNK)),
            _const_spec((A_GROUPS, CHUNK, 1)),
            _const_spec((CONV_W, B_WIDTH)),
            _const_spec((d, d)),
            gu_spec,
            gu_spec,
            dn_spec,
        ],
        out_specs=[pl.BlockSpec((t, d), lambda i: (i, 0)), gu_spec, gu_spec, dn_spec],
        out_shape=[
            jax.ShapeDtypeStruct((n, d), _F32),
            jax.ShapeDtypeStruct(ffn_w_gate.shape, _BF16),
            jax.ShapeDtypeStruct(ffn_w_up.shape, _BF16),
            jax.ShapeDtypeStruct(ffn_w_down.shape, _BF16),
        ],
        scratch_shapes=[
            pltpu.VMEM((t + 2 * HALO, d), _BF16),
            pltpu.VMEM((t, B_WIDTH), _F32),
            pltpu.VMEM((t, d), _BF16),
        ],
        compiler_params=_params(),
        name="even_mixer",
    )(x2, x2, x2, g.reshape(1, d), w_in.astype(_BF16), ln_g.reshape(1, -1), ln_b.reshape(1, -1),
      w_sp.astype(_BF16), b_sp[..., None], conv_w, w_out.astype(_BF16), ffn_w_gate, ffn_w_up, ffn_w_down)


def _ffn_kernel(has_final_norm, x_ref, g_ref, wg_ref, wu_ref, wd_ref, *rest):
    if has_final_norm:
        gf_ref, o_ref, h_ref, acc_ref = rest
    else:
        o_ref, h_ref, acc_ref = rest
    x = x_ref[...]
    h_ref[...] = _rms(x, g_ref[...]).astype(_BF16)
    acc_ref[...] = x

    h = h_ref[...]
    for c in range(N_FF_CHUNKS):
        cols = slice(c * FF_CHUNK, (c + 1) * FF_CHUNK)
        gate = _dot(h, wg_ref[:, cols])
        up = _dot(h, wu_ref[:, cols])
        act = (jax.nn.silu(gate) * up).astype(_BF16)
        down = _dot(act, wd_ref[cols, :])
        if c < N_FF_CHUNKS - 1:
            acc_ref[...] += down
    part = x_ref.shape[0] // FFN_OUT_PARTS
    for q in range(FFN_OUT_PARTS):
        rows = slice(q * part, (q + 1) * part)
        out = acc_ref[rows, :] + down[rows, :]
        if has_final_norm:
            out = _rms(out, gf_ref[...])
        o_ref[rows, :] = out


def _layer_spec(stacked_shape, layer):
    zeros = (0,) * (len(stacked_shape) - 1)
    return pl.BlockSpec((None,) + tuple(stacked_shape[1:]), lambda i: (layer,) + zeros,
                        pipeline_mode=pl.Buffered(1))


def _ffn_layer(x2, g, wg, wu, wd, layer, final_g=None):
    n, d = x2.shape
    t = WIDE_TILE
    in_specs = [
        pl.BlockSpec((t, d), lambda i: (i, 0)),
        _const_spec((1, d)),
        _layer_spec(wg.shape, layer),
        _layer_spec(wu.shape, layer),
        _layer_spec(wd.shape, layer),
    ]
    args = [x2, g.reshape(1, d), wg, wu, wd]
    if final_g is not None:
        in_specs.append(_const_spec((1, d)))
        args.append(final_g.reshape(1, d))
    return pl.pallas_call(
        functools.partial(_ffn_kernel, final_g is not None),
        grid=(n // t,),
        in_specs=in_specs,
        out_specs=pl.BlockSpec((t, d), lambda i: (i, 0)),
        out_shape=jax.ShapeDtypeStruct((n, d), _F32),
        scratch_shapes=[pltpu.VMEM((t, d), _BF16), pltpu.VMEM((t, d), _F32)],
        compiler_params=_params(),
        name="swiglu_ffn",
    )(*args)


def _qkv_kernel(x_ref, g_ref, wqv_ref, wk_ref, qt_ref, k_ref, vt_ref):
    h = _rms(x_ref[...], g_ref[...]).astype(_BF16)
    qv = _dot_nt(wqv_ref[...], h)
    q_rows = N_HEADS * HEAD_DIM
    scale = HEAD_DIM ** -0.5 * LOG2_E
    for j in range(x_ref.shape[0] // BLOCK):
        lanes = slice(j * BLOCK, (j + 1) * BLOCK)
        qt_ref[j] = (qv[:q_rows, lanes] * scale).astype(_BF16)
        vt_ref[j] = qv[q_rows:, lanes].astype(_BF16)
    k_ref[...] = _dot(h, wk_ref[...]).astype(_BF16)


def _qkv_layer(x2, g, w_qkv):
    n, d = x2.shape
    t = QKV_TILE
    q_cols = N_HEADS * HEAD_DIM
    kv_cols = N_KV_HEADS * HEAD_DIM
    wq = w_qkv[:, :q_cols]
    wk = w_qkv[:, q_cols:q_cols + kv_cols]
    wv = w_qkv[:, q_cols + kv_cols:]
    wqv_t = jnp.concatenate([wq, wv], axis=1).T.astype(_BF16)
    n_blocks = n // BLOCK
    return pl.pallas_call(
        _qkv_kernel,
        grid=(n // t,),
        in_specs=[
            pl.BlockSpec((t, d), lambda i: (i, 0)),
            _const_spec((1, d)),
            _const_spec(wqv_t.shape),
            _const_spec((d, kv_cols)),
        ],
        out_specs=[
            pl.BlockSpec((t // BLOCK, q_cols, BLOCK), lambda i: (i, 0, 0)),
            pl.BlockSpec((t, kv_cols), lambda i: (i, 0)),
            pl.BlockSpec((t // BLOCK, kv_cols, BLOCK), lambda i: (i, 0, 0)),
        ],
        out_shape=[
            jax.ShapeDtypeStruct((n_blocks, q_cols, BLOCK), _BF16),
            jax.ShapeDtypeStruct((n, kv_cols), _BF16),
            jax.ShapeDtypeStruct((n_blocks, kv_cols, BLOCK), _BF16),
        ],
        compiler_params=_params(),
        name="qkv_proj",
    )(x2, g.reshape(1, d), wqv_t, wk.astype(_BF16))


def _t5_log_thresholds():
    nb = N_BUCKETS // 2
    max_exact = nb // 2
    steps = nb - max_exact
    ratio = MAX_DISTANCE // max_exact
    assert ratio * max_exact == MAX_DISTANCE
    thresholds = []
    n = max_exact
    for k in range(1, steps):
        while n ** steps < max_exact ** steps * ratio ** k:
            n += 1
        thresholds.append(n)
    return thresholds


def _t5_buckets(rel):
    nb = N_BUCKETS // 2
    max_exact = nb // 2
    n = jnp.abs(rel)
    large = jnp.full(rel.shape, max_exact, jnp.int32)
    for thr in _t5_log_thresholds():
        large = large + (n >= thr).astype(jnp.int32)
    return jnp.where(rel > 0, nb, 0) + jnp.where(n < max_exact, n, large)


def _attn_kernel(tiles_per_seq, sink_ref, rb_ref, x_ref, qt_ref, kc_ref, kp_ref, kn_ref,
                 vc_ref, vp_ref, vn_ref, wo_ref, o_ref,
                 bias_ref, kall_ref, vall_ref, sc_ref, p_ref, at_ref):
    i = pl.program_id(0)
    pos = i % tiles_per_seq

    @pl.when(i == 0)
    def _init():
        rel = (lax.broadcasted_iota(jnp.int32, (KEY_SPAN, BLOCK), 0) - BLOCK
               - lax.broadcasted_iota(jnp.int32, (KEY_SPAN, BLOCK), 1))
        band = jnp.abs(rel) <= WINDOW
        bkt = _t5_buckets(rel)

        def fill_head(hd, carry):
            tbl = jnp.zeros((KEY_SPAN, BLOCK), _F32)
            for b in range(N_BUCKETS):
                tbl = jnp.where(bkt == b, rb_ref[b * N_HEADS + hd], tbl)
            bias_ref[hd] = jnp.where(band, tbl * LOG2_E, MASK_VALUE)
            return carry

        lax.fori_loop(0, N_HEADS, fill_head, 0)

    kall_ref[0:BLOCK, :] = kp_ref[...]
    kall_ref[BLOCK:BLOCK + TOKEN_TILE, :] = kc_ref[...]
    kall_ref[BLOCK + TOKEN_TILE:, :] = kn_ref[...]
    vall_ref[0] = vp_ref[0]
    for j in range(SUB_BLOCKS):
        vall_ref[1 + j] = vc_ref[j]
    vall_ref[1 + SUB_BLOCKS] = vn_ref[0]

    pen_first = jnp.where(pos == 0, MASK_VALUE, 0.0)
    pen_last = jnp.where(pos == tiles_per_seq - 1, MASK_VALUE, 0.0)

    pair_lanes = 2 * HEAD_DIM
    slabs = [slice(r0, r0 + ROW_SLAB) for r0 in range(0, KEY_SPAN, ROW_SLAB)]
    heads = [(j, kvh) for j in range(SUB_BLOCKS) for kvh in range(N_KV_HEADS)]

    def scores(t):
        j, kvh = heads[t]
        pair, half = kvh // 2, kvh % 2
        q_heads = jnp.concatenate(
            [qt_ref[j, (kvh * Q_PER_KV + grp) * HEAD_DIM:(kvh * Q_PER_KV + grp + 1) * HEAD_DIM, :]
             for grp in range(Q_PER_KV)], axis=1)
        q_zero = jnp.zeros_like(q_heads)
        q2 = jnp.concatenate([q_heads, q_zero] if half == 0 else [q_zero, q_heads], axis=0)
        kwin = kall_ref[j * BLOCK:j * BLOCK + KEY_SPAN, pair * pair_lanes:(pair + 1) * pair_lanes]
        raw = _dot(kwin, q2)
        sc_buf = sc_ref.at[t % SCORE_AHEAD]
        maxes = []
        for grp in range(Q_PER_KV):
            hd = kvh * Q_PER_KV + grp
            lanes = slice(grp * BLOCK, (grp + 1) * BLOCK)
            mx = None
            for rows in slabs:
                s = raw[rows, lanes] + bias_ref[hd, rows, :]
                if j == 0 and rows.stop <= BLOCK:
                    s = s + pen_first
                if j == SUB_BLOCKS - 1 and rows.start >= 2 * BLOCK:
                    s = s + pen_last
                sc_buf[rows, lanes] = s
                mx = s if mx is None else jnp.maximum(mx, s)
            maxes.append(jnp.maximum(jnp.max(mx, axis=0, keepdims=True), sink_ref[hd] * LOG2_E))
        return maxes

    def softmax(t, maxes):
        j, kvh = heads[t]
        sc_buf = sc_ref.at[t % SCORE_AHEAD]
        p_buf = p_ref.at[t % SCORE_AHEAD]
        inv_l = []
        for grp in range(Q_PER_KV):
            hd = kvh * Q_PER_KV + grp
            lanes = slice(grp * BLOCK, (grp + 1) * BLOCK)
            m = maxes[grp]
            acc = None
            for rows in slabs:
                p = jnp.exp2(sc_buf[rows, lanes] - m)
                acc = p if acc is None else acc + p
                p_buf[rows, lanes] = p.astype(_BF16)
            l = jnp.sum(acc, axis=0, keepdims=True) + jnp.exp2(sink_ref[hd] * LOG2_E - m)
            inv_l.append(1.0 / l)
        return inv_l

    def weighted_values(t, inv_l):
        j, kvh = heads[t]
        vwin = jnp.concatenate(
            [vall_ref[j + b, kvh * HEAD_DIM:(kvh + 1) * HEAD_DIM, :] for b in range(3)], axis=1)
        ot = _dot(vwin, p_ref[t % SCORE_AHEAD])
        for grp in range(Q_PER_KV):
            hd = kvh * Q_PER_KV + grp
            at_ref[j, hd * HEAD_DIM:(hd + 1) * HEAD_DIM, :] = (
                ot[:, grp * BLOCK:(grp + 1) * BLOCK] * inv_l[grp]).astype(_BF16)

    def out_proj_piece(j, piece):
        rows = slice(j * BLOCK, (j + 1) * BLOCK)
        cols = slice(piece * OUT_PIECE, (piece + 1) * OUT_PIECE)
        o_ref[rows, cols] = x_ref[rows, cols] + _dot_tn(at_ref[j], wo_ref[:, cols])

    n_pieces = D_MODEL // OUT_PIECE
    maxes = {t: scores(t) for t in range(SCORE_AHEAD)}
    for t, (j, kvh) in enumerate(heads):
        inv_l = softmax(t, maxes.pop(t))
        weighted_values(t, inv_l)
        if j > 0:
            for piece in range(n_pieces):
                if (piece * N_KV_HEADS) // n_pieces == kvh:
                    out_proj_piece(j - 1, piece)
        if t + SCORE_AHEAD < len(heads):
            maxes[t + SCORE_AHEAD] = scores(t + SCORE_AHEAD)
    for piece in range(D_MODEL // OUT_PIECE):
        out_proj_piece(SUB_BLOCKS - 1, piece)


def _attn_layer(x2, qt, k, vt, sink, rel_bias, w_out, seq_len):
    n, d = x2.shape
    t = TOKEN_TILE
    n_blocks = n // BLOCK
    kv_cols = N_KV_HEADS * HEAD_DIM
    q_cols = N_HEADS * HEAD_DIM

    def prev_block(i):
        return jnp.maximum(i * SUB_BLOCKS - 1, 0)

    def next_block(i):
        return jnp.minimum((i + 1) * SUB_BLOCKS, n_blocks - 1)

    smem = pl.BlockSpec(memory_space=pltpu.SMEM)
    return pl.pallas_call(
        functools.partial(_attn_kernel, seq_len // t),
        grid=(n // t,),
        in_specs=[
            smem,
            smem,
            pl.BlockSpec((t, d), lambda i: (i, 0)),
            pl.BlockSpec((SUB_BLOCKS, q_cols, BLOCK), lambda i: (i, 0, 0)),
            pl.BlockSpec((t, kv_cols), lambda i: (i, 0)),
            pl.BlockSpec((BLOCK, kv_cols), lambda i: (prev_block(i), 0)),
            pl.BlockSpec((BLOCK, kv_cols), lambda i: (next_block(i), 0)),
            pl.BlockSpec((SUB_BLOCKS, kv_cols, BLOCK), lambda i: (i, 0, 0)),
            pl.BlockSpec((1, kv_cols, BLOCK), lambda i: (prev_block(i), 0, 0)),
            pl.BlockSpec((1, kv_cols, BLOCK), lambda i: (next_block(i), 0, 0)),
            _const_spec((d, d)),
        ],
        out_specs=pl.BlockSpec((t, d), lambda i: (i, 0)),
        out_shape=jax.ShapeDtypeStruct((n, d), _F32),
        scratch_shapes=[
            pltpu.VMEM((N_HEADS, KEY_SPAN, BLOCK), _F32),
            pltpu.VMEM((TOKEN_TILE + 2 * BLOCK, kv_cols), _BF16),
            pltpu.VMEM((SUB_BLOCKS + 2, kv_cols, BLOCK), _BF16),
            pltpu.VMEM((SCORE_AHEAD, KEY_SPAN, Q_PER_KV * BLOCK), _F32),
            pltpu.VMEM((SCORE_AHEAD, KEY_SPAN, Q_PER_KV * BLOCK), _BF16),
            pltpu.VMEM((SUB_BLOCKS, q_cols, BLOCK), _BF16),
        ],
        compiler_params=_params(),
        name="window_attention",
    )(sink, rel_bias.reshape(-1), x2, qt, k, k, k, vt, vt, vt, w_out.astype(_BF16))


def kernel(x, norm_mix, norm_ffn, even_w_in, even_v_ln_g, even_v_ln_b, even_w_spatial, even_b_spatial,
           even_conv_w, even_w_out, attn_w_qkv, attn_sink, rel_bias, attn_w_out, ffn_w_gate, ffn_w_up,
           ffn_w_down, final_norm):
    bsz, seq_len, d = x.shape
    assert d == D_MODEL and seq_len % WIDE_TILE == 0 and (bsz * seq_len) % QKV_TILE == 0
    x2 = x.reshape(bsz * seq_len, d)
    x2, wg, wu, wd = _even_layer(x2, norm_mix[0], even_w_in[0], even_v_ln_g[0], even_v_ln_b[0],
                                 even_w_spatial[0], even_b_spatial[0], even_conv_w[0], even_w_out[0],
                                 ffn_w_gate, ffn_w_up, ffn_w_down, seq_len)
    x2 = _ffn_layer(x2, norm_ffn[0], wg, wu, wd, 0)
    qt, k, vt = _qkv_layer(x2, norm_mix[1], attn_w_qkv[0])
    x2 = _attn_layer(x2, qt, k, vt, attn_sink[0], rel_bias, attn_w_out[0], seq_len)
    x2 = _ffn_layer(x2, norm_ffn[1], wg, wu, wd, 1, final_g=final_norm)
    return x2.reshape(bsz, seq_len, d)
```

```python
import functools
import math

import jax
import jax.numpy as jnp
from jax import lax
from jax.experimental import pallas as pl
from jax.experimental.pallas import tpu as pltpu

D_MODEL = 1024
EPS = 1e-6
A_WIDTH = 512
B_WIDTH = 512
A_GROUPS = 4
A_GROUP_DIM = 128
CHUNK = 128
CONV_W = 3
N_HEADS = 16
N_KV_HEADS = 4
Q_PER_KV = 4
HEAD_DIM = 64
WINDOW = 128
BLOCK = 128
N_BUCKETS = 32
MAX_DISTANCE = 128
D_FF = 2816

TOKEN_TILE = 1024
WIDE_TILE = 1024
QKV_TILE = 2048
HALO = 16
EVEN_PARTS = 2
FF_CHUNK = 256
N_FF_CHUNKS = D_FF // FF_CHUNK
FFN_OUT_PARTS = 4
SUB_BLOCKS = TOKEN_TILE // BLOCK
KEY_SPAN = 3 * BLOCK
MASK_VALUE = float("-inf")
LOG2_E = math.log2(math.e)
ROW_SLAB = 64
OUT_PIECE = 256
SCORE_AHEAD = 2
V7X_VMEM_BYTES = 64 * 1024 * 1024
VMEM_LIMIT_BYTES = V7X_VMEM_BYTES * 7 // 8

_BF16 = jnp.bfloat16
_F32 = jnp.float32


def _dot(a, b):
    return jnp.dot(a, b, preferred_element_type=_F32)


def _dot_nt(a, b):
    return lax.dot_general(a, b, (((1,), (1,)), ((), ())), preferred_element_type=_F32)


def _dot_tn(a, b):
    return lax.dot_general(a, b, (((0,), (0,)), ((), ())), preferred_element_type=_F32)


def _gelu(x):
    return 0.5 * x * (1.0 + lax.erf(x * math.sqrt(0.5)))


def _rms(x, g):
    return (x * lax.rsqrt(jnp.mean(x * x, axis=-1, keepdims=True) + EPS)) * g


def _const_spec(shape):
    zeros = (0,) * len(shape)
    return pl.BlockSpec(shape, lambda i: zeros, pipeline_mode=pl.Buffered(1))


def _params():
    return pltpu.CompilerParams(
        dimension_semantics=("arbitrary",), vmem_limit_bytes=VMEM_LIMIT_BYTES)


def _even_kernel(tiles_per_seq, x_ref, xp_ref, xn_ref, g_ref, win_ref, lng_ref, lnb_ref,
                 wsp_ref, bsp_ref, cw_ref, wo_ref, ffn_g_ref, ffn_u_ref, ffn_d_ref,
                 o_ref, ffn_g16_ref, ffn_u16_ref, ffn_d16_ref, hs_ref, conv_ref, y_ref):
    t = WIDE_TILE
    i = pl.program_id(0)
    pos = i % tiles_per_seq
    ffn_g16_ref[...] = ffn_g_ref[...].astype(_BF16)
    ffn_u16_ref[...] = ffn_u_ref[...].astype(_BF16)
    ffn_d16_ref[...] = ffn_d_ref[...].astype(_BF16)
    g = g_ref[...]
    h = _rms(x_ref[...], g).astype(_BF16)
    hp = jnp.where(pos == 0, 0.0, _rms(xp_ref[...], g)).astype(_BF16)
    hn = jnp.where(pos == tiles_per_seq - 1, 0.0, _rms(xn_ref[...], g)).astype(_BF16)
    hs_ref[0:HALO, :] = hp
    hs_ref[HALO:HALO + t, :] = h
    hs_ref[HALO + t:, :] = hn

    in_width_a = 2 * A_WIDTH + B_WIDTH
    pc = _dot(hs_ref[...], win_ref[:, in_width_a:])

    z = pc[:, :B_WIDTH] * pc[:, B_WIDTH:]
    n_rows = t + 2 * HALO
    z_prev = pltpu.roll(z, 1, 0)[HALO:HALO + t]
    z_next = pltpu.roll(z, n_rows - 1, 0)[HALO:HALO + t]
    cw = cw_ref[...]
    conv_ref[...] = z_prev * cw[0:1] + z[HALO:HALO + t] * cw[1:2] + z_next * cw[2:3]

    part = t // EVEN_PARTS
    parts = [slice(q * part, (q + 1) * part) for q in range(EVEN_PARTS)]
    pas = [_dot(hs_ref[HALO + rows.start:HALO + rows.stop, :], win_ref[:, :in_width_a])
           for rows in parts]
    for rows, pa in zip(parts, pas):
        u = _gelu(pa[:, :A_WIDTH])
        v = _gelu(pa[:, A_WIDTH:2 * A_WIDTH])
        mu = jnp.mean(v, axis=-1, keepdims=True)
        vc = v - mu
        vn = vc * lax.rsqrt(jnp.mean(vc * vc, axis=-1, keepdims=True) + EPS)
        vb = (vn * lng_ref[...] + lnb_ref[...]).astype(_BF16)
        chunks = [slice(c * CHUNK, (c + 1) * CHUNK) for c in range(part // CHUNK)]
        for grp in range(A_GROUPS):
            cols = slice(grp * A_GROUP_DIM, (grp + 1) * A_GROUP_DIM)
            v_chunks = jnp.concatenate([vb[crows, cols] for crows in chunks], axis=1)
            mixed = _dot(wsp_ref[grp], v_chunks) + bsp_ref[grp]
            for c, crows in enumerate(chunks):
                orows = slice(rows.start + crows.start, rows.start + crows.stop)
                y_ref[orows, cols] = (u[crows, cols]
                                      * mixed[:, c * A_GROUP_DIM:(c + 1) * A_GROUP_DIM]).astype(_BF16)
        y_ref[rows, A_WIDTH:] = (pa[:, 2 * A_WIDTH:] * conv_ref[rows, :]).astype(_BF16)
        o_ref[rows, :] = x_ref[rows, :] + _dot(y_ref[rows, :], wo_ref[...])


def _even_layer(x2, g, w_in, ln_g, ln_b, w_sp, b_sp, conv_w, w_out, ffn_w_gate, ffn_w_up, ffn_w_down, seq_len):
    n, d = x2.shape
    t = WIDE_TILE
    n_steps = n // t
    halo_blocks_per_tile = t // HALO
    n_halo_blocks = n // HALO
    gu_rows = d // n_steps
    dn_rows = ffn_w_down.shape[1] // (n_steps // 2)
    assert gu_rows * n_steps == d and dn_rows * (n_steps // 2) == ffn_w_down.shape[1]
    assert gu_rows % 16 == 0 and dn_rows % 16 == 0
    depth, _, d_ff = ffn_w_gate.shape
    gu_spec = pl.BlockSpec((depth, gu_rows, d_ff), lambda i: (0, i, 0))
    dn_spec = pl.BlockSpec((depth, dn_rows, d), lambda i: (0, i // 2, 0))
    return pl.pallas_call(
        functools.partial(_even_kernel, seq_len // t),
        grid=(n_steps,),
        in_specs=[
            pl.BlockSpec((t, d), lambda i: (i, 0)),
            pl.BlockSpec((HALO, d), lambda i: (jnp.maximum(i * halo_blocks_per_tile - 1, 0), 0)),
            pl.BlockSpec((HALO, d),
                         lambda i: (jnp.minimum((i + 1) * halo_blocks_per_tile, n_halo_blocks - 1), 0)),
            _const_spec((1, d)),
            _const_spec(w_in.shape),
            _const_spec((1, A_WIDTH)),
            _const_spec((1, A_WIDTH)),
            _const_spec((A_GROUPS, CHUNK, CHUNK)),
            _const_spec((A_GROUPS, CHUNK, 1)),
            _const_spec((CONV_W, B_WIDTH)),
            _const_spec((d, d)),
            gu_spec,
            gu_spec,
            dn_spec,
        ],
        out_specs=[pl.BlockSpec((t, d), lambda i: (i, 0)), gu_spec, gu_spec, dn_spec],
        out_shape=[
            jax.ShapeDtypeStruct((n, d), _F32),
            jax.ShapeDtypeStruct(ffn_w_gate.shape, _BF16),
            jax.ShapeDtypeStruct(ffn_w_up.shape, _BF16),
            jax.ShapeDtypeStruct(ffn_w_down.shape, _BF16),
        ],
        scratch_shapes=[
            pltpu.VMEM((t + 2 * HALO, d), _BF16),
            pltpu.VMEM((t, B_WIDTH), _F32),
            pltpu.VMEM((t, d), _BF16),
        ],
        compiler_params=_params(),
        name="even_mixer",
    )(x2, x2, x2, g.reshape(1, d), w_in.astype(_BF16), ln_g.reshape(1, -1), ln_b.reshape(1, -1),
      w_sp.astype(_BF16), b_sp[..., None], conv_w, w_out.astype(_BF16), ffn_w_gate, ffn_w_up, ffn_w_down)


def _ffn_kernel(has_final_norm, x_ref, g_ref, wg_ref, wu_ref, wd_ref, *rest):
    if has_final_norm:
        gf_ref, o_ref, h_ref, acc_ref = rest
    else:
        o_ref, h_ref, acc_ref = rest
    x = x_ref[...]
    h_ref[...] = _rms(x, g_ref[...]).astype(_BF16)
    acc_ref[...] = x

    h = h_ref[...]
    pending = []
    for c in range(N_FF_CHUNKS):
        cols = slice(c * FF_CHUNK, (c + 1) * FF_CHUNK)
        gate = _dot(h, wg_ref[:, cols])
        up = _dot(h, wu_ref[:, cols])
        pending.append((jax.nn.silu(gate) * up).astype(_BF16))
        if len(pending) == 2 or c == N_FF_CHUNKS - 1:
            rows = slice((c + 1 - len(pending)) * FF_CHUNK, (c + 1) * FF_CHUNK)
            act = pending[0] if len(pending) == 1 else jnp.concatenate(pending, axis=1)
            down = _dot(act, wd_ref[rows, :])
            pending = []
            if c < N_FF_CHUNKS - 1:
                acc_ref[...] += down
    part = x_ref.shape[0] // FFN_OUT_PARTS
    for q in range(FFN_OUT_PARTS):
        rows = slice(q * part, (q + 1) * part)
        out = acc_ref[rows, :] + down[rows, :]
        if has_final_norm:
            out = _rms(out, gf_ref[...])
        o_ref[rows, :] = out


def _layer_spec(stacked_shape, layer):
    zeros = (0,) * (len(stacked_shape) - 1)
    return pl.BlockSpec((None,) + tuple(stacked_shape[1:]), lambda i: (layer,) + zeros,
                        pipeline_mode=pl.Buffered(1))


def _ffn_layer(x2, g, wg, wu, wd, layer, final_g=None):
    n, d = x2.shape
    t = WIDE_TILE
    in_specs = [
        pl.BlockSpec((t, d), lambda i: (i, 0)),
        _const_spec((1, d)),
        _layer_spec(wg.shape, layer),
        _layer_spec(wu.shape, layer),
        _layer_spec(wd.shape, layer),
    ]
    args = [x2, g.reshape(1, d), wg, wu, wd]
    if final_g is not None:
        in_specs.append(_const_spec((1, d)))
        args.append(final_g.reshape(1, d))
    return pl.pallas_call(
        functools.partial(_ffn_kernel, final_g is not None),
        grid=(n // t,),
        in_specs=in_specs,
        out_specs=pl.BlockSpec((t, d), lambda i: (i, 0)),
        out_shape=jax.ShapeDtypeStruct((n, d), _F32),
        scratch_shapes=[pltpu.VMEM((t, d), _BF16), pltpu.VMEM((t, d), _F32)],
        compiler_params=_params(),
        name="swiglu_ffn",
    )(*args)


def _qkv_kernel(x_ref, g_ref, wqv_ref, wk_ref, qt_ref, k_ref, vt_ref):
    h = _rms(x_ref[...], g_ref[...]).astype(_BF16)
    qv = _dot_nt(wqv_ref[...], h)
    q_rows = N_HEADS * HEAD_DIM
    scale = HEAD_DIM ** -0.5 * LOG2_E
    for j in range(x_ref.shape[0] // BLOCK):
        lanes = slice(j * BLOCK, (j + 1) * BLOCK)
        qt_ref[j] = (qv[:q_rows, lanes] * scale).astype(_BF16)
        vt_ref[j] = qv[q_rows:, lanes].astype(_BF16)
    k_ref[...] = _dot(h, wk_ref[...]).astype(_BF16)


def _qkv_layer(x2, g, w_qkv):
    n, d = x2.shape
    t = QKV_TILE
    q_cols = N_HEADS * HEAD_DIM
    kv_cols = N_KV_HEADS * HEAD_DIM
    wq = w_qkv[:, :q_cols]
    wk = w_qkv[:, q_cols:q_cols + kv_cols]
    wv = w_qkv[:, q_cols + kv_cols:]
    wqv_t = jnp.concatenate([wq, wv], axis=1).T.astype(_BF16)
    n_blocks = n // BLOCK
    return pl.pallas_call(
        _qkv_kernel,
        grid=(n // t,),
        in_specs=[
            pl.BlockSpec((t, d), lambda i: (i, 0)),
            _const_spec((1, d)),
            _const_spec(wqv_t.shape),
            _const_spec((d, kv_cols)),
        ],
        out_specs=[
            pl.BlockSpec((t // BLOCK, q_cols, BLOCK), lambda i: (i, 0, 0)),
            pl.BlockSpec((t, kv_cols), lambda i: (i, 0)),
            pl.BlockSpec((t // BLOCK, kv_cols, BLOCK), lambda i: (i, 0, 0)),
        ],
        out_shape=[
            jax.ShapeDtypeStruct((n_blocks, q_cols, BLOCK), _BF16),
            jax.ShapeDtypeStruct((n, kv_cols), _BF16),
            jax.ShapeDtypeStruct((n_blocks, kv_cols, BLOCK), _BF16),
        ],
        compiler_params=_params(),
        name="qkv_proj",
    )(x2, g.reshape(1, d), wqv_t, wk.astype(_BF16))


def _t5_log_thresholds():
    nb = N_BUCKETS // 2
    max_exact = nb // 2
    steps = nb - max_exact
    ratio = MAX_DISTANCE // max_exact
    assert ratio * max_exact == MAX_DISTANCE
    thresholds = []
    n = max_exact
    for k in range(1, steps):
        while n ** steps < max_exact ** steps * ratio ** k:
            n += 1
        thresholds.append(n)
    return thresholds


def _t5_buckets(rel):
    nb = N_BUCKETS // 2
    max_exact = nb // 2
    n = jnp.abs(rel)
    large = jnp.full(rel.shape, max_exact, jnp.int32)
    for thr in _t5_log_thresholds():
        large = large + (n >= thr).astype(jnp.int32)
    return jnp.where(rel > 0, nb, 0) + jnp.where(n < max_exact, n, large)


def _attn_kernel(tiles_per_seq, sink_ref, rb_ref, x_ref, qt_ref, kc_ref, kp_ref, kn_ref,
                 vc_ref, vp_ref, vn_ref, wo_ref, o_ref,
                 bias_ref, kall_ref, vall_ref, sc_ref, p_ref, at_ref):
    i = pl.program_id(0)
    pos = i % tiles_per_seq

    @pl.when(i == 0)
    def _init():
        rel = (lax.broadcasted_iota(jnp.int32, (KEY_SPAN, BLOCK), 0) - BLOCK
               - lax.broadcasted_iota(jnp.int32, (KEY_SPAN, BLOCK), 1))
        band = jnp.abs(rel) <= WINDOW
        bkt = _t5_buckets(rel)

        def fill_head(hd, carry):
            tbl = jnp.zeros((KEY_SPAN, BLOCK), _F32)
            for b in range(N_BUCKETS):
                tbl = jnp.where(bkt == b, rb_ref[b * N_HEADS + hd], tbl)
            bias_ref[hd] = jnp.where(band, tbl * LOG2_E, MASK_VALUE)
            return carry

        lax.fori_loop(0, N_HEADS, fill_head, 0)

    kall_ref[0:BLOCK, :] = kp_ref[...]
    kall_ref[BLOCK:BLOCK + TOKEN_TILE, :] = kc_ref[...]
    kall_ref[BLOCK + TOKEN_TILE:, :] = kn_ref[...]
    vall_ref[0] = vp_ref[0]
    for j in range(SUB_BLOCKS):
        vall_ref[1 + j] = vc_ref[j]
    vall_ref[1 + SUB_BLOCKS] = vn_ref[0]

    pen_first = jnp.where(pos == 0, MASK_VALUE, 0.0)
    pen_last = jnp.where(pos == tiles_per_seq - 1, MASK_VALUE, 0.0)

    pair_lanes = 2 * HEAD_DIM
    slabs = [slice(r0, r0 + ROW_SLAB) for r0 in range(0, KEY_SPAN, ROW_SLAB)]
    heads = [(j, kvh) for j in range(SUB_BLOCKS) for kvh in range(N_KV_HEADS)]

    def scores(t):
        j, kvh = heads[t]
        pair, half = kvh // 2, kvh % 2
        q_heads = jnp.concatenate(
            [qt_ref[j, (kvh * Q_PER_KV + grp) * HEAD_DIM:(kvh * Q_PER_KV + grp + 1) * HEAD_DIM, :]
             for grp in range(Q_PER_KV)], axis=1)
        q_zero = jnp.zeros_like(q_heads)
        q2 = jnp.concatenate([q_heads, q_zero] if half == 0 else [q_zero, q_heads], axis=0)
        kwin = kall_ref[j * BLOCK:j * BLOCK + KEY_SPAN, pair * pair_lanes:(pair + 1) * pair_lanes]
        raw = _dot(kwin, q2)
        sc_buf = sc_ref.at[t % SCORE_AHEAD]
        maxes = []
        for grp in range(Q_PER_KV):
            hd = kvh * Q_PER_KV + grp
            lanes = slice(grp * BLOCK, (grp + 1) * BLOCK)
            mx = None
            for rows in slabs:
                s = raw[rows, lanes] + bias_ref[hd, rows, :]
                if j == 0 and rows.stop <= BLOCK:
                    s = s + pen_first
                if j == SUB_BLOCKS - 1 and rows.start >= 2 * BLOCK:
                    s = s + pen_last
                sc_buf[rows, lanes] = s
                mx = s if mx is None else jnp.maximum(mx, s)
            maxes.append(jnp.maximum(jnp.max(mx, axis=0, keepdims=True), sink_ref[hd] * LOG2_E))
        return maxes

    def softmax(t, maxes):
        j, kvh = heads[t]
        sc_buf = sc_ref.at[t % SCORE_AHEAD]
        p_buf = p_ref.at[t % SCORE_AHEAD]
        inv_l = []
        for grp in range(Q_PER_KV):
            hd = kvh * Q_PER_KV + grp
            lanes = slice(grp * BLOCK, (grp + 1) * BLOCK)
            m = maxes[grp]
            acc = None
            for rows in slabs:
                p = jnp.exp2(sc_buf[rows, lanes] - m)
                acc = p if acc is None else acc + p
                p_buf[rows, lanes] = p.astype(_BF16)
            l = jnp.sum(acc, axis=0, keepdims=True) + jnp.exp2(sink_ref[hd] * LOG2_E - m)
            inv_l.append(1.0 / l)
        return inv_l

    def weighted_values(t, inv_l):
        j, kvh = heads[t]
        vwin = jnp.concatenate(
            [vall_ref[j + b, kvh * HEAD_DIM:(kvh + 1) * HEAD_DIM, :] for b in range(3)], axis=1)
        ot = _dot(vwin, p_ref[t % SCORE_AHEAD])
        for grp in range(Q_PER_KV):
            hd = kvh * Q_PER_KV + grp
            at_ref[j, hd * HEAD_DIM:(hd + 1) * HEAD_DIM, :] = (
                ot[:, grp * BLOCK:(grp + 1) * BLOCK] * inv_l[grp]).astype(_BF16)

    def out_proj_piece(j, piece):
        rows = slice(j * BLOCK, (j + 1) * BLOCK)
        cols = slice(piece * OUT_PIECE, (piece + 1) * OUT_PIECE)
        o_ref[rows, cols] = x_ref[rows, cols] + _dot_tn(at_ref[j], wo_ref[:, cols])

    n_pieces = D_MODEL // OUT_PIECE
    maxes = {t: scores(t) for t in range(SCORE_AHEAD)}
    for t, (j, kvh) in enumerate(heads):
        inv_l = softmax(t, maxes.pop(t))
        weighted_values(t, inv_l)
        if j > 0:
            for piece in range(n_pieces):
                if (piece * N_KV_HEADS) // n_pieces == kvh:
                    out_proj_piece(j - 1, piece)
        if t + SCORE_AHEAD < len(heads):
            maxes[t + SCORE_AHEAD] = scores(t + SCORE_AHEAD)
    for piece in range(D_MODEL // OUT_PIECE):
        out_proj_piece(SUB_BLOCKS - 1, piece)


def _attn_layer(x2, qt, k, vt, sink, rel_bias, w_out, seq_len):
    n, d = x2.shape
    t = TOKEN_TILE
    n_blocks = n // BLOCK
    kv_cols = N_KV_HEADS * HEAD_DIM
    q_cols = N_HEADS * HEAD_DIM

    def prev_block(i):
        return jnp.maximum(i * SUB_BLOCKS - 1, 0)

    def next_block(i):
        return jnp.minimum((i + 1) * SUB_BLOCKS, n_blocks - 1)

    smem = pl.BlockSpec(memory_space=pltpu.SMEM)
    return pl.pallas_call(
        functools.partial(_attn_kernel, seq_len // t),
        grid=(n // t,),
        in_specs=[
            smem,
            smem,
            pl.BlockSpec((t, d), lambda i: (i, 0)),
            pl.BlockSpec((SUB_BLOCKS, q_cols, BLOCK), lambda i: (i, 0, 0)),
            pl.BlockSpec((t, kv_cols), lambda i: (i, 0)),
            pl.BlockSpec((BLOCK, kv_cols), lambda i: (prev_block(i), 0)),
            pl.BlockSpec((BLOCK, kv_cols), lambda i: (next_block(i), 0)),
            pl.BlockSpec((SUB_BLOCKS, kv_cols, BLOCK), lambda i: (i, 0, 0)),
            pl.BlockSpec((1, kv_cols, BLOCK), lambda i: (prev_block(i), 0, 0)),
            pl.BlockSpec((1, kv_cols, BLOCK), lambda i: (next_block(i), 0, 0)),
            _const_spec((d, d)),
        ],
        out_specs=pl.BlockSpec((t, d), lambda i: (i, 0)),
        out_shape=jax.ShapeDtypeStruct((n, d), _F32),
        scratch_shapes=[
            pltpu.VMEM((N_HEADS, KEY_SPAN, BLOCK), _F32),
            pltpu.VMEM((TOKEN_TILE + 2 * BLOCK, kv_cols), _BF16),
            pltpu.VMEM((SUB_BLOCKS + 2, kv_cols, BLOCK), _BF16),
            pltpu.VMEM((SCORE_AHEAD, KEY_SPAN, Q_PER_KV * BLOCK), _F32),
            pltpu.VMEM((SCORE_AHEAD, KEY_SPAN, Q_PER_KV * BLOCK), _BF16),
            pltpu.VMEM((SUB_BLOCKS, q_cols, BLOCK), _BF16),
        ],
        compiler_params=_params(),
        name="window_attention",
    )(sink, rel_bias.reshape(-1), x2, qt, k, k, k, vt, vt, vt, w_out.astype(_BF16))


def kernel(x, norm_mix, norm_ffn, even_w_in, even_v_ln_g, even_v_ln_b, even_w_spatial, even_b_spatial,
           even_conv_w, even_w_out, attn_w_qkv, attn_sink, rel_bias, attn_w_out, ffn_w_gate, ffn_w_up,
           ffn_w_down, final_norm):
    bsz, seq_len, d = x.shape
    assert d == D_MODEL and seq_len % WIDE_TILE == 0 and (bsz * seq_len) % QKV_TILE == 0
    x2 = x.reshape(bsz * seq_len, d)
    x2, wg, wu, wd = _even_layer(x2, norm_mix[0], even_w_in[0], even_v_ln_g[0], even_v_ln_b[0],
                                 even_w_spatial[0], even_b_spatial[0], even_conv_w[0], even_w_out[0],
                                 ffn_w_gate, ffn_w_up, ffn_w_down, seq_len)
    x2 = _ffn_layer(x2, norm_ffn[0], wg, wu, wd, 0)
    qt, k, vt = _qkv_layer(x2, norm_mix[1], attn_w_qkv[0])
    x2 = _attn_layer(x2, qt, k, vt, attn_sink[0], rel_bias, attn_w_out[0], seq_len)
    x2 = _ffn_layer(x2, norm_ffn[1], wg, wu, wd, 1, final_g=final_norm)
    return x2.reshape(bsz, seq_len, d)
```
